```python
import math
import jax, jax.numpy as jnp
from jax import lax
import numpy as np

D_MODEL = 1024
BATCH = 8
SEQ = 2048
DEPTH = 1
DEC_BATCH = 128
DEC_SEQ = 8
PAST_LEN = 16384
PAGE_SIZE = 128

N_META = 16
D_MIX = D_MODEL
N_HEADS_DN = 4
DK = 128
DV = 128
DK_TOT = N_HEADS_DN * DK
DV_TOT = N_HEADS_DN * DV
QKV_DIM = 2 * DK_TOT + DV_TOT
SHORT_CONV = 4
CHUNK = 64
C_CONV = D_MIX - DV_TOT
CONV_WIDTH = 31
D_FF = 2816
D_IN = QKV_DIM + DV_TOT + 2 * N_HEADS_DN + 2 * C_CONV
EPS = 1e-6

kernel_name = 'hybrid_gdn_conformer_conv_decoder_step'


def rmsnorm(x, g):
    x32 = x.astype(jnp.float32)
    y = x32 * lax.rsqrt(jnp.mean(x32 * x32, axis=-1, keepdims=True) + EPS)
    return (y * g.astype(jnp.float32)).astype(x.dtype)


def l2norm(x):
    return x * lax.rsqrt(jnp.sum(x * x, axis=-1, keepdims=True) + EPS)


def swiglu_half(h, g, w1, w3, w2):
    u = rmsnorm(h, g)
    return h + 0.5 * ((jax.nn.silu(u @ w1) * (u @ w3)) @ w2)


def causal_dwconv(x_ext, w):
    c = x_ext.shape[-1]
    return lax.conv_general_dilated(x_ext, w[:, None, :].astype(x_ext.dtype), window_strides=(1,),
                                    padding='VALID', dimension_numbers=('NWC', 'WIO', 'NWC'),
                                    feature_group_count=c)


def gated_delta_chunked(q, k, v, g, beta, s0, chunk):
    b, L, h, _ = q.shape
    pad = (-L) % chunk
    if pad:
        pz = lambda t: jnp.pad(t, [(0, 0), (0, pad)] + [(0, 0)] * (t.ndim - 2))
        q, k, v, g, beta = pz(q), pz(k), pz(v), pz(g), pz(beta)
    n = (L + pad) // chunk
    blk = lambda t: t.reshape(b, n, chunk, h, t.shape[-1]).transpose(0, 3, 1, 2, 4)
    q, k, v = blk(q), blk(k), blk(v)
    g = g.reshape(b, n, chunk, h).transpose(0, 3, 1, 2)
    beta = beta.reshape(b, n, chunk, h).transpose(0, 3, 1, 2)
    gc = jnp.cumsum(g, axis=-1)
    idx = jnp.arange(chunk)
    causal = idx[:, None] >= idx[None, :]
    strict = (idx[:, None] > idx[None, :]).astype(jnp.float32)
    decay = jnp.exp(jnp.where(causal, gc[..., :, None] - gc[..., None, :], -jnp.inf))
    kb = k * beta[..., None]
    vb = v * beta[..., None]
    a_mat = jnp.einsum('bhncd,bhnsd->bhncs', kb, k) * decay * strict
    m_mat = jnp.eye(chunk, dtype=jnp.float32) + a_mat
    rhs = jnp.concatenate([vb, kb * jnp.exp(gc)[..., None]], axis=-1)
    sol = lax.linalg.triangular_solve(m_mat, rhs, left_side=True, lower=True, unit_diagonal=True)
    u_c, w_c = sol[..., :DV], sol[..., DV:]
    att = jnp.einsum('bhncd,bhnsd->bhncs', q, k) * decay

    def step(s, xs):
        q_i, k_i, u_i, w_i, att_i, gc_i = xs
        v_new = u_i - jnp.einsum('bhck,bhkv->bhcv', w_i, s)
        o = (jnp.einsum('bhck,bhkv->bhcv', q_i * jnp.exp(gc_i)[..., None], s)
             + jnp.einsum('bhcs,bhsv->bhcv', att_i, v_new))
        g_last = gc_i[..., -1]
        s = (s * jnp.exp(g_last)[..., None, None]
             + jnp.einsum('bhck,bhcv->bhkv', k_i * jnp.exp(g_last[..., None] - gc_i)[..., None], v_new))
        return s, o

    first = lambda t: jnp.moveaxis(t, 2, 0)
    s_fin, o = lax.scan(step, s0, (first(q), first(k), first(u_c), first(w_c), first(att), first(gc)))
    o = o.transpose(1, 0, 3, 2, 4).reshape(b, n * chunk, h, DV)[:, :L]
    return o, s_fin


def hybrid_mixer(h, s_delta, s_qkv, s_conv, n_lead, norm_g, w_in, qkv_conv_w, a_log, dt_bias,
                 o_norm_g, conv_w, conv_b, conv_norm_g, w_out):
    f32 = jnp.float32
    bn, L, _ = h.shape
    u = rmsnorm(h, norm_g)
    p = u @ w_in
    cut = np.cumsum([QKV_DIM, DV_TOT, N_HEADS_DN, N_HEADS_DN]).tolist()
    qkv_pre, z, b_raw, a_raw, glu = jnp.split(p, cut, axis=-1)
    qkv_ext = jnp.concatenate([s_qkv.astype(p.dtype), qkv_pre], axis=1)
    new_s_qkv = qkv_ext[:, -(SHORT_CONV - 1):]
    qkv = jax.nn.silu(causal_dwconv(qkv_ext, qkv_conv_w)).astype(f32)
    q, k, v = jnp.split(qkv, [DK_TOT, 2 * DK_TOT], axis=-1)
    q = l2norm(q.reshape(bn, L, N_HEADS_DN, DK)) * (DK ** -0.5)
    k = l2norm(k.reshape(bn, L, N_HEADS_DN, DK))
    v = v.reshape(bn, L, N_HEADS_DN, DV)
    beta = jax.nn.sigmoid(b_raw.astype(f32))
    g = -jnp.exp(a_log.astype(f32)) * jax.nn.softplus(a_raw.astype(f32) + dt_bias.astype(f32))
    s = s_delta.astype(f32)
    segments = [(0, n_lead), (n_lead, L)] if n_lead > 0 else [(0, L)]
    outs = []
    for lo, hi in segments:
        o_seg, s = gated_delta_chunked(q[:, lo:hi], k[:, lo:hi], v[:, lo:hi], g[:, lo:hi], beta[:, lo:hi],
                                       s, min(CHUNK, hi - lo))
        outs.append(o_seg)
    o = jnp.concatenate(outs, axis=1)
    o = rmsnorm(o, o_norm_g) * jax.nn.silu(z.astype(f32).reshape(bn, L, N_HEADS_DN, DV))
    o = o.reshape(bn, L, DV_TOT).astype(h.dtype)
    val, gate = jnp.split(glu, 2, axis=-1)
    c_in = val * jax.nn.sigmoid(gate)
    c_ext = jnp.concatenate([s_conv.astype(c_in.dtype), c_in], axis=1)
    new_s_conv = c_ext[:, -(CONV_WIDTH - 1):]
    c = causal_dwconv(c_ext, conv_w) + conv_b
    c = jax.nn.silu(rmsnorm(c, conv_norm_g))
    mix = jnp.concatenate([o, c], axis=-1) @ w_out
    return h + mix, s.astype(h.dtype), new_s_qkv, new_s_conv


def decoder_layer(h, s_delta, s_qkv, s_conv, n_lead, ffn1, mixp, ffn2):
    h = swiglu_half(h, *ffn1)
    h, sd, sq, sc = hybrid_mixer(h, s_delta, s_qkv, s_conv, n_lead, *mixp)
    h = swiglu_half(h, *ffn2)
    return h, sd, sq, sc


def setup_inputs(seed: int = 0) -> dict:
    key = jax.random.key(seed)
    ks = jax.random.split(key, 25)
    f32 = jnp.float32
    nrm = lambda i, shape, scale: jax.random.normal(ks[i], shape, f32) * scale
    gain = lambda i, shape: 1.0 + 0.05 * jax.random.normal(ks[i], shape, f32)
    a_log = jnp.log(jax.random.uniform(ks[13], (DEPTH, N_HEADS_DN), f32, 1.0, 16.0))
    dt = jnp.exp(jax.random.uniform(ks[14], (DEPTH, N_HEADS_DN), f32, math.log(1e-3), math.log(1e-1)))
    dt_bias = dt + jnp.log(-jnp.expm1(-dt))
    return {
        'x_prompt': nrm(0, (BATCH, SEQ, D_MODEL), 1.0),
        'x_sample': nrm(1, (DEC_BATCH, DEC_SEQ, D_MODEL), 1.0),
        'state_delta': nrm(2, (DEPTH, DEC_BATCH, N_HEADS_DN, DK, DV), 0.5),
        'state_qkv_conv': nrm(3, (DEPTH, DEC_BATCH, SHORT_CONV - 1, QKV_DIM), 1.0),
        'state_conv': nrm(4, (DEPTH, DEC_BATCH, CONV_WIDTH - 1, C_CONV), 1.0),
        'meta_tokens': nrm(5, (N_META, D_MODEL), 1.0),
        'ffn1_norm': gain(6, (DEPTH, D_MODEL)),
        'ffn1_w1': nrm(7, (DEPTH, D_MODEL, D_FF), D_MODEL ** -0.5),
        'ffn1_w3': nrm(8, (DEPTH, D_MODEL, D_FF), D_MODEL ** -0.5),
        'ffn1_w2': nrm(9, (DEPTH, D_FF, D_MODEL), D_FF ** -0.5),
        'mix_norm': gain(10, (DEPTH, D_MODEL)),
        'w_in': nrm(11, (DEPTH, D_MODEL, D_IN), D_MODEL ** -0.5),
        'qkv_conv_w': nrm(12, (DEPTH, SHORT_CONV, QKV_DIM), SHORT_CONV ** -0.5),
        'a_log': a_log,
        'dt_bias': dt_bias,
        'o_norm': gain(15, (DEPTH, DV)),
        'conv_w': nrm(16, (DEPTH, CONV_WIDTH, C_CONV), CONV_WIDTH ** -0.5),
        'conv_b': nrm(17, (DEPTH, C_CONV), 0.02),
        'conv_norm': gain(18, (DEPTH, C_CONV)),
        'w_out': nrm(19, (DEPTH, D_MIX, D_MODEL), D_MIX ** -0.5),
        'ffn2_norm': gain(20, (DEPTH, D_MODEL)),
        'ffn2_w1': nrm(21, (DEPTH, D_MODEL, D_FF), D_MODEL ** -0.5),
        'ffn2_w3': nrm(22, (DEPTH, D_MODEL, D_FF), D_MODEL ** -0.5),
        'ffn2_w2': nrm(23, (DEPTH, D_FF, D_MODEL), D_FF ** -0.5),
        'final_norm': gain(24, (D_MODEL,)),
    }


def reference(x_prompt, x_sample, state_delta, state_qkv_conv, state_conv, meta_tokens,
              ffn1_norm, ffn1_w1, ffn1_w3, ffn1_w2, mix_norm, w_in, qkv_conv_w, a_log, dt_bias,
              o_norm, conv_w, conv_b, conv_norm, w_out, ffn2_norm, ffn2_w1, ffn2_w3, ffn2_w2, final_norm):
    dt = x_prompt.dtype
    bp = x_prompt.shape[0]
    meta = jnp.broadcast_to(meta_tokens.astype(dt)[None], (bp, N_META, D_MODEL))
    hp = jnp.concatenate([meta, x_prompt], axis=1)
    hs = x_sample
    dp, qp, cp, ds, qs, cs = [], [], [], [], [], []
    for l in range(DEPTH):
        ffn1 = (ffn1_norm[l], ffn1_w1[l], ffn1_w3[l], ffn1_w2[l])
        ffn2 = (ffn2_norm[l], ffn2_w1[l], ffn2_w3[l], ffn2_w2[l])
        mixp = (mix_norm[l], w_in[l], qkv_conv_w[l], a_log[l], dt_bias[l], o_norm[l],
                conv_w[l], conv_b[l], conv_norm[l], w_out[l])
        zd = jnp.zeros((bp, N_HEADS_DN, DK, DV), dt)
        zq = jnp.zeros((bp, SHORT_CONV - 1, QKV_DIM), dt)
        zc = jnp.zeros((bp, CONV_WIDTH - 1, C_CONV), dt)
        hp, sd, sq, sc = decoder_layer(hp, zd, zq, zc, N_META, ffn1, mixp, ffn2)
        dp.append(sd); qp.append(sq); cp.append(sc)
        hs, sd, sq, sc = decoder_layer(hs, state_delta[l], state_qkv_conv[l], state_conv[l], 0, ffn1, mixp, ffn2)
        ds.append(sd); qs.append(sq); cs.append(sc)
    y_prompt = rmsnorm(hp, final_norm)[:, N_META:]
    y_sample = rmsnorm(hs, final_norm)
    return (y_prompt, y_sample, jnp.stack(dp), jnp.stack(qp), jnp.stack(cp),
            jnp.stack(ds), jnp.stack(qs), jnp.stack(cs))
```

```python
import functools
import math

import jax
import jax.numpy as jnp
from jax import lax
from jax.experimental import pallas as pl
from jax.experimental.pallas import tpu as pltpu

D_MODEL = 1024
D_FF = 2816
N_HEADS = 4
DK = 128
DV = 128
DK_TOT = N_HEADS * DK
DV_TOT = N_HEADS * DV
QKV_DIM = 2 * DK_TOT + DV_TOT
C_CONV = 512
SHORT_CONV = 4
CONV_WIDTH = 31
CHUNK = 64
N_META = 16
EPS = 1e-6

LANES = 128
P_WIDTH = QKV_DIM + DV_TOT + 2 * C_CONV + LANES
QKV_PAD = 8
CONV_PAD = 32
VMEM_LIMIT = 56 * 1024 * 1024

F32 = jnp.float32
BF16 = jnp.bfloat16


def _rms(x, g):
    return x * lax.rsqrt(jnp.mean(x * x, axis=-1, keepdims=True) + EPS) * g


def _silu(x):
    return x * jax.nn.sigmoid(x)


def _wdot(x, w):
    return jnp.dot(x.astype(BF16), w, preferred_element_type=F32)


def _sdot(a, b):
    return jnp.dot(a, b, preferred_element_type=F32)


def _ffn_half(h, g_ref, w1_ref, w3_ref, w2_ref):
    u = _rms(h, g_ref[...]).astype(BF16)
    a = jnp.dot(u, w1_ref[...], preferred_element_type=F32)
    b = jnp.dot(u, w3_ref[...], preferred_element_type=F32)
    hid = (_silu(a) * b).astype(BF16)
    return h + 0.5 * jnp.dot(hid, w2_ref[...], preferred_element_type=F32)


def _front_kernel(x_ref, g1_ref, w1_ref, w3_ref, w2_ref, gm_ref, win_ref,
                  h1_ref, qkv_ref, z_ref, cin_ref, ba_ref):
    h1 = _ffn_half(x_ref[...], g1_ref, w1_ref, w3_ref, w2_ref)
    h1_ref[...] = h1
    p = _wdot(_rms(h1, gm_ref[...]), win_ref[...])
    qkv_ref[...] = p[:, :QKV_DIM]
    z_ref[...] = p[:, QKV_DIM:QKV_DIM + DV_TOT]
    o = QKV_DIM + DV_TOT
    cin_ref[...] = p[:, o:o + C_CONV] * jax.nn.sigmoid(p[:, o + C_CONV:o + 2 * C_CONV])
    ba_ref[...] = p[:, o + 2 * C_CONV:]


def _back_kernel(h_ref, g2_ref, w1_ref, w3_ref, w2_ref, gf_ref, y_ref):
    y_ref[...] = _rms(_ffn_half(h_ref[...], g2_ref, w1_ref, w3_ref, w2_ref), gf_ref[...])


def _const_spec(shape):
    nd = len(shape)
    return pl.BlockSpec(shape, lambda *_: (0,) * nd, pipeline_mode=pl.Buffered(1))


def _row_spec(rows, width):
    return pl.BlockSpec((rows, width), lambda i: (i, 0))


def _front(x, g1, w1, w3, w2, gm, win, rows):
    n = x.shape[0]
    widths = (D_MODEL, QKV_DIM, DV_TOT, C_CONV, LANES)
    return pl.pallas_call(
        _front_kernel,
        grid=(n // rows,),
        in_specs=[_row_spec(rows, D_MODEL), _const_spec(g1.shape), _const_spec(w1.shape), _const_spec(w3.shape),
                  _const_spec(w2.shape), _const_spec(gm.shape), _const_spec(win.shape)],
        out_specs=[_row_spec(rows, w) for w in widths],
        out_shape=[jax.ShapeDtypeStruct((n, w), F32) for w in widths],
        compiler_params=pltpu.CompilerParams(dimension_semantics=("arbitrary",), vmem_limit_bytes=VMEM_LIMIT),
        name="front",
    )(x, g1, w1, w3, w2, gm, win)


def _back(h, g2, w1, w3, w2, gf, rows):
    n = h.shape[0]
    return pl.pallas_call(
        _back_kernel,
        grid=(n // rows,),
        in_specs=[_row_spec(rows, D_MODEL), _const_spec(g2.shape), _const_spec(w1.shape), _const_spec(w3.shape),
                  _const_spec(w2.shape), _const_spec(gf.shape)],
        out_specs=_row_spec(rows, D_MODEL),
        out_shape=jax.ShapeDtypeStruct((n, D_MODEL), F32),
        compiler_params=pltpu.CompilerParams(dimension_semantics=("arbitrary",), vmem_limit_bytes=VMEM_LIMIT),
        name="back",
    )(h, g2, w1, w3, w2, gf)


def _block_cumsum(x, block, reverse=False):
    rows = x.shape[0]
    r = lax.broadcasted_iota(jnp.int32, x.shape, 0) % block
    s = 1
    while s < block:
        if reverse:
            x = x + jnp.where(r < block - s, pltpu.roll(x, rows - s, axis=0), 0.0)
        else:
            x = x + jnp.where(r >= s, pltpu.roll(x, s, axis=0), 0.0)
        s *= 2
    return x


def _transpose_rows(x):
    rows = x.shape[0]
    pad = (-rows) % LANES
    if pad:
        x = jnp.concatenate([x, jnp.zeros((pad, x.shape[1]), x.dtype)], axis=0)
    return x.T[:, :rows]


def _unit_lower_inverse(a, block):
    n = a.shape[0]
    eye = (lax.broadcasted_iota(jnp.int32, (n, n), 0) == lax.broadcasted_iota(jnp.int32, (n, n), 1)).astype(F32)
    p = eye - a
    x = a
    s = 2
    while s < block:
        x = _sdot(x, x)
        p = p + _sdot(p, x)
        s *= 2
    return p


def _mixer_kernel(h1_ref, qkv_ref, z_ref, cin_ref, ba_ref, sd0_ref, sq0_ref, sc0_ref,
                  qw_ref, alog_ref, dtb_ref, onorm_ref, cw_ref, cb_ref, cnorm_ref, wout_ref,
                  h2_ref, sd_ref, sq_ref, sc_ref,
                  extq_ref, extc_ref, q_s, k_s, v_s, mix_s,
                  *, G, T, C, NT, chain):
    R = G * T
    t = pl.program_id(1)

    @pl.when(t == 0)
    def _():
        extq_ref[:, QKV_PAD - (SHORT_CONV - 1):QKV_PAD, :] = sq0_ref[...]
        extc_ref[:, CONV_PAD - (CONV_WIDTH - 1):CONV_PAD, :] = sc0_ref[...]
        sd_ref[...] = sd0_ref[...]

    extq_ref[:, QKV_PAD:QKV_PAD + T, :] = qkv_ref[...]
    extc_ref[:, CONV_PAD:CONV_PAD + T, :] = cin_ref[...]

    qk_scale = DK ** -0.5
    rbq = min(16, T)
    for g in range(G):
        for r0 in range(0, T, rbq):
            base = QKV_PAD - (SHORT_CONV - 1) + r0
            acc = extq_ref[g, base:base + rbq, :] * qw_ref[0:1, :]
            for j in range(1, SHORT_CONV):
                acc = acc + extq_ref[g, base + j:base + j + rbq, :] * qw_ref[j:j + 1, :]
            s = _silu(acc)
            rows = slice(g * T + r0, g * T + r0 + rbq)
            for h in range(N_HEADS):
                qh = s[:, h * DK:(h + 1) * DK]
                kh = s[:, DK_TOT + h * DK:DK_TOT + (h + 1) * DK]
                q_s[rows, h * DK:(h + 1) * DK] = qh * (lax.rsqrt(jnp.sum(qh * qh, -1, keepdims=True) + EPS) * qk_scale)
                k_s[rows, h * DK:(h + 1) * DK] = kh * lax.rsqrt(jnp.sum(kh * kh, -1, keepdims=True) + EPS)
            v_s[rows, :] = s[:, 2 * DK_TOT:]
    sq_ref[...] = extq_ref[:, T + QKV_PAD - (SHORT_CONV - 1):T + QKV_PAD, :]

    rbc = min(32, T)
    for g in range(G):
        for r0 in range(0, T, rbc):
            base = CONV_PAD - (CONV_WIDTH - 1) + r0
            acc = extc_ref[g, base:base + rbc, :] * cw_ref[0:1, :]
            for j in range(1, CONV_WIDTH):
                acc = acc + extc_ref[g, base + j:base + j + rbc, :] * cw_ref[j:j + 1, :]
            c = _silu(_rms(acc + cb_ref[...], cnorm_ref[...]))
            mix_s[g * T + r0:g * T + r0 + rbc, DV_TOT:] = c
    sc_ref[...] = extc_ref[:, T + CONV_PAD - (CONV_WIDTH - 1):T + CONV_PAD, :]

    if NT > 1:
        extq_ref[:, 0:QKV_PAD, :] = extq_ref[:, T:T + QKV_PAD, :]
        extc_ref[:, 0:CONV_PAD, :] = extc_ref[:, T:T + CONV_PAD, :]

    ba = ba_ref[...].reshape(R, LANES)
    beta_all = jax.nn.sigmoid(ba)
    xg = ba + dtb_ref[...]
    g_all = -jnp.exp(alog_ref[...]) * (jnp.maximum(xg, 0.0) + jnp.log1p(jnp.exp(-jnp.abs(xg))))
    gc_all = _block_cumsum(g_all, C)
    gc_all_t = _transpose_rows(gc_all)
    if chain:
        gtail_all = None
    else:
        gtail_all = _block_cumsum(g_all, C, reverse=True) - g_all

    Rs = C if chain else R
    row_i = lax.broadcasted_iota(jnp.int32, (Rs, Rs), 0)
    col_i = lax.broadcasted_iota(jnp.int32, (Rs, Rs), 1)
    same = (row_i // C) == (col_i // C)
    causal = same & (row_i >= col_i)
    strict = same & (row_i > col_i)

    for h in range(N_HEADS):
        hs = slice(h * DK, (h + 1) * DK)
        k_t_all = _transpose_rows(k_s[:, hs])
        for c0 in range(0, R, Rs):
            rs = slice(c0, c0 + Rs)
            q = q_s[rs, hs]
            k = k_s[rs, hs]
            v = v_s[rs, hs]
            k_t = k_t_all[:, c0:c0 + Rs]
            beta = beta_all[rs, h:h + 1]
            gc_c = gc_all[rs, N_HEADS + h:N_HEADS + h + 1]
            gc_r = gc_all_t[N_HEADS + h:N_HEADS + h + 1, c0:c0 + Rs]
            decay = jnp.exp(jnp.where(causal, gc_c - gc_r, -jnp.inf))
            kb = k * beta
            vb = v * beta
            egc = jnp.exp(gc_c)
            a_mat = jnp.where(strict, _sdot(kb, k_t) * decay, 0.0)
            t_inv = _unit_lower_inverse(a_mat, C)
            u = _sdot(t_inv, vb)
            w = _sdot(t_inv, kb * egc)
            att = _sdot(q, k_t) * decay
            qg = q * egc
            if chain:
                s_old = sd_ref[0, h]
                v_new = u - _sdot(w, s_old)
                o = _sdot(qg, s_old) + _sdot(att, v_new)
                g_last = gc_c[Rs - 1:Rs, :]
                k_til_t = k_t * jnp.exp(g_last - gc_r)
                sd_ref[0, h] = s_old * jnp.exp(g_last) + _sdot(k_til_t, v_new)
            else:
                gtail_c = gtail_all[rs, N_HEADS + h:N_HEADS + h + 1]
                k_til_t = _transpose_rows(k * jnp.exp(gtail_c))
                ws, qs = [], []
                for b in range(Rs // C):
                    wq = _sdot(jnp.concatenate([w[b * C:(b + 1) * C], qg[b * C:(b + 1) * C]], axis=0), sd_ref[b, h])
                    ws.append(wq[:C])
                    qs.append(wq[C:])
                v_new = u - jnp.concatenate(ws, axis=0)
                o = jnp.concatenate(qs, axis=0) + _sdot(att, v_new)
                for b in range(Rs // C):
                    e_last = jnp.exp(gc_c[(b + 1) * C - 1:(b + 1) * C, :])
                    sd_ref[b, h] = (sd_ref[b, h] * e_last
                                    + _sdot(k_til_t[:, b * C:(b + 1) * C], v_new[b * C:(b + 1) * C]))
            zh = z_ref[0, rs, hs] if chain else z_ref[:, :, hs].reshape(R, DV)
            mix_s[rs, hs] = _rms(o, onorm_ref[...]) * _silu(zh)

    mix = _wdot(mix_s[...], wout_ref[...])
    h2_ref[...] = (h1_ref[...].reshape(R, D_MODEL) + mix).reshape(G, T, D_MODEL)


def _mixer(h1, qkv, z, cin, ba, sd0, sq0, sc0, prm, *, G, T, C, shared_init):
    B, L, _ = h1.shape
    NB, NT = B // G, L // T
    chain = G == 1
    assert chain or NT == 1
    R = G * T
    seq = lambda w: pl.BlockSpec((G, T, w), lambda b, t: (b, t, 0))
    if shared_init:
        st = lambda shape: pl.BlockSpec((G,) + shape, lambda b, t: (0,) * (1 + len(shape)))
    else:
        st = lambda shape: pl.BlockSpec((G,) + shape, lambda b, t: (b,) + (0,) * len(shape))
    out_st = lambda shape: pl.BlockSpec((G,) + shape, lambda b, t: (b,) + (0,) * len(shape))
    cst = lambda a: pl.BlockSpec(a.shape, lambda b, t: (0,) * a.ndim)
    sd_shape, sq_shape, sc_shape = (N_HEADS, DK, DV), (SHORT_CONV - 1, QKV_DIM), (CONV_WIDTH - 1, C_CONV)
    kern = functools.partial(_mixer_kernel, G=G, T=T, C=C, NT=NT, chain=chain)
    return pl.pallas_call(
        kern,
        grid=(NB, NT),
        in_specs=[seq(D_MODEL), seq(QKV_DIM), seq(DV_TOT), seq(C_CONV), seq(LANES),
                  st(sd_shape), st(sq_shape), st(sc_shape)] + [cst(a) for a in prm],
        out_specs=[seq(D_MODEL), out_st(sd_shape), out_st(sq_shape), out_st(sc_shape)],
        out_shape=[jax.ShapeDtypeStruct((B, L, D_MODEL), F32), jax.ShapeDtypeStruct((B,) + sd_shape, F32),
                   jax.ShapeDtypeStruct((B,) + sq_shape, F32), jax.ShapeDtypeStruct((B,) + sc_shape, F32)],
        scratch_shapes=[pltpu.VMEM((G, QKV_PAD + T, QKV_DIM), F32), pltpu.VMEM((G, CONV_PAD + T, C_CONV), F32),
                        pltpu.VMEM((R, DK_TOT), F32), pltpu.VMEM((R, DK_TOT), F32), pltpu.VMEM((R, DV_TOT), F32),
                        pltpu.VMEM((R, DV_TOT + C_CONV), F32)],
        compiler_params=pltpu.CompilerParams(dimension_semantics=("arbitrary", "arbitrary"),
                                             vmem_limit_bytes=VMEM_LIMIT),
        name="mixer",
    )(h1, qkv, z, cin, ba, sd0, sq0, sc0, *prm)


def kernel(x_prompt, x_sample, state_delta, state_qkv_conv, state_conv, meta_tokens, ffn1_norm, ffn1_w1, ffn1_w3,
           ffn1_w2, mix_norm, w_in, qkv_conv_w, a_log, dt_bias, o_norm, conv_w, conv_b, conv_norm, w_out, ffn2_norm,
           ffn2_w1, ffn2_w3, ffn2_w2, final_norm):
    bp, seq_len, _ = x_prompt.shape
    bs, dec_len, _ = x_sample.shape
    l = 0
    row = lambda v: v.reshape(1, -1).astype(F32)

    cut = QKV_DIM + DV_TOT
    win = jnp.concatenate([w_in[l][:, :cut], w_in[l][:, cut + 2 * N_HEADS:], w_in[l][:, cut:cut + 2 * N_HEADS],
                           jnp.zeros((D_MODEL, LANES - 2 * N_HEADS), w_in.dtype)], axis=1).astype(BF16)
    f1 = (row(ffn1_norm[l]), ffn1_w1[l].astype(BF16), ffn1_w3[l].astype(BF16), ffn1_w2[l].astype(BF16))
    f2 = (row(ffn2_norm[l]), ffn2_w1[l].astype(BF16), ffn2_w3[l].astype(BF16), ffn2_w2[l].astype(BF16))
    lane_row = lambda v: jnp.zeros((1, LANES), F32).at[0, N_HEADS:2 * N_HEADS].set(v.astype(F32))
    prm = (qkv_conv_w[l].astype(F32), lane_row(a_log[l]), lane_row(dt_bias[l]), row(o_norm[l]),
           conv_w[l].astype(F32), row(conv_b[l]), row(conv_norm[l]), w_out[l].astype(BF16))
    gm, gf = row(mix_norm[l]), row(final_norm)

    front = lambda x, rows: _front(x, *f1, gm, win, rows)

    m = [a.reshape(1, N_META, -1) for a in front(meta_tokens.astype(F32), N_META)]
    zeros = lambda *s: jnp.zeros(s, F32)
    _, sd_m, sq_m, sc_m = _mixer(*m, zeros(1, N_HEADS, DK, DV), zeros(1, SHORT_CONV - 1, QKV_DIM),
                                 zeros(1, CONV_WIDTH - 1, C_CONV), prm, G=1, T=N_META, C=N_META, shared_init=True)

    p = [a.reshape(bp, seq_len, -1) for a in front(x_prompt.reshape(bp * seq_len, D_MODEL), 256)]
    h2p, sd_p, sq_p, sc_p = _mixer(*p, sd_m, sq_m, sc_m, prm, G=1, T=256, C=CHUNK, shared_init=True)
    y_prompt = _back(h2p.reshape(bp * seq_len, D_MODEL), *f2, gf, 256).reshape(bp, seq_len, D_MODEL)

    s = [a.reshape(bs, dec_len, -1) for a in front(x_sample.reshape(bs * dec_len, D_MODEL), 256)]
    h2s, sd_s, sq_s, sc_s = _mixer(*s, state_delta[l], state_qkv_conv[l], state_conv[l], prm,
                                   G=16, T=dec_len, C=dec_len, shared_init=False)
    y_sample = _back(h2s.reshape(bs * dec_len, D_MODEL), *f2, gf, 256).reshape(bs, dec_len, D_MODEL)

    return (y_prompt, y_sample, sd_p[None], sq_p[None], sc_p[None], sd_s[None], sq_s[None], sc_s[None])
```

```python
import functools
import math

import jax
import jax.numpy as jnp
from jax import lax
from jax.experimental import pallas as pl
from jax.experimental.pallas import tpu as pltpu

D_MODEL = 1024
D_FF = 2816
N_HEADS = 4
DK = 128
DV = 128
DK_TOT = N_HEADS * DK
DV_TOT = N_HEADS * DV
QKV_DIM = 2 * DK_TOT + DV_TOT
C_CONV = 512
SHORT_CONV = 4
CONV_WIDTH = 31
CHUNK = 64
N_META = 16
EPS = 1e-6

LANES = 128
SUBLANES = 8
P_WIDTH = QKV_DIM + DV_TOT + 2 * C_CONV + LANES
QKV_PAD = 8
CONV_PAD = 32
VMEM_LIMIT = 56 * 1024 * 1024

F32 = jnp.float32
BF16 = jnp.bfloat16


def _rms(x, g):
    return x * lax.rsqrt(jnp.mean(x * x, axis=-1, keepdims=True) + EPS) * g


def _silu(x):
    return x * jax.nn.sigmoid(x)


def _wdot(x, w):
    return jnp.dot(x.astype(BF16), w, preferred_element_type=F32)


def _sdot(a, b):
    return jnp.dot(a, b, preferred_element_type=F32)


def _ffn_half(h, g_ref, w1_ref, w3_ref, w2_ref):
    u = _rms(h, g_ref[...]).astype(BF16)
    a = jnp.dot(u, w1_ref[...], preferred_element_type=F32)
    b = jnp.dot(u, w3_ref[...], preferred_element_type=F32)
    hid = (_silu(a) * b).astype(BF16)
    return h + 0.5 * jnp.dot(hid, w2_ref[...], preferred_element_type=F32)


def _front_kernel(x_ref, g1_ref, w1_ref, w3_ref, w2_ref, gm_ref, win_ref,
                  h1_ref, qkv_ref, z_ref, cin_ref, ba_ref):
    h1 = _ffn_half(x_ref[...], g1_ref, w1_ref, w3_ref, w2_ref)
    h1_ref[...] = h1
    p = _wdot(_rms(h1, gm_ref[...]), win_ref[...])
    qkv_ref[...] = p[:, :QKV_DIM]
    z_ref[...] = p[:, QKV_DIM:QKV_DIM + DV_TOT]
    o = QKV_DIM + DV_TOT
    cin_ref[...] = p[:, o:o + C_CONV] * jax.nn.sigmoid(p[:, o + C_CONV:o + 2 * C_CONV])
    ba_ref[...] = p[:, o + 2 * C_CONV:]


def _back_kernel(h_ref, g2_ref, w1_ref, w3_ref, w2_ref, gf_ref, y_ref):
    y_ref[...] = _rms(_ffn_half(h_ref[...], g2_ref, w1_ref, w3_ref, w2_ref), gf_ref[...])


def _const_spec(shape):
    nd = len(shape)
    return pl.BlockSpec(shape, lambda *_: (0,) * nd, pipeline_mode=pl.Buffered(1))


def _row_spec(rows, width):
    return pl.BlockSpec((rows, width), lambda i: (i, 0))


def _front(x, g1, w1, w3, w2, gm, win, rows):
    n = x.shape[0]
    widths = (D_MODEL, QKV_DIM, DV_TOT, C_CONV, LANES)
    return pl.pallas_call(
        _front_kernel,
        grid=(n // rows,),
        in_specs=[_row_spec(rows, D_MODEL), _const_spec(g1.shape), _const_spec(w1.shape), _const_spec(w3.shape),
                  _const_spec(w2.shape), _const_spec(gm.shape), _const_spec(win.shape)],
        out_specs=[_row_spec(rows, w) for w in widths],
        out_shape=[jax.ShapeDtypeStruct((n, w), F32) for w in widths],
        compiler_params=pltpu.CompilerParams(dimension_semantics=("arbitrary",), vmem_limit_bytes=VMEM_LIMIT),
        name="front",
    )(x, g1, w1, w3, w2, gm, win)


def _back(h, g2, w1, w3, w2, gf, rows):
    n = h.shape[0]
    return pl.pallas_call(
        _back_kernel,
        grid=(n // rows,),
        in_specs=[_row_spec(rows, D_MODEL), _const_spec(g2.shape), _const_spec(w1.shape), _const_spec(w3.shape),
                  _const_spec(w2.shape), _const_spec(gf.shape)],
        out_specs=_row_spec(rows, D_MODEL),
        out_shape=jax.ShapeDtypeStruct((n, D_MODEL), F32),
        compiler_params=pltpu.CompilerParams(dimension_semantics=("arbitrary",), vmem_limit_bytes=VMEM_LIMIT),
        name="back",
    )(h, g2, w1, w3, w2, gf)


def _block_cumsum(x, block, reverse=False):
    rows = x.shape[0]
    r = lax.broadcasted_iota(jnp.int32, x.shape, 0) % block
    s = 1
    while s < block:
        if reverse:
            x = x + jnp.where(r < block - s, pltpu.roll(x, rows - s, axis=0), 0.0)
        else:
            x = x + jnp.where(r >= s, pltpu.roll(x, s, axis=0), 0.0)
        s *= 2
    return x


def _tap_sum(read, w_ref, n_taps, rows):
    accs = [None] * (rows // SUBLANES)
    for j in range(n_taps):
        w8 = w_ref[j]
        for i in range(len(accs)):
            tap = read(j, i) * w8
            accs[i] = tap if accs[i] is None else accs[i] + tap
    return accs[0] if len(accs) == 1 else jnp.concatenate(accs, axis=0)


def _transpose_rows(x):
    rows = x.shape[0]
    pad = (-rows) % LANES
    if pad:
        x = jnp.concatenate([x, jnp.zeros((pad, x.shape[1]), x.dtype)], axis=0)
    return x.T[:, :rows]


def _mixer_kernel(h1_ref, qkv_ref, z_ref, cin_ref, ba_ref, sd0_ref, sq0_ref, sc0_ref,
                  qw_ref, alog_ref, dtb_ref, onorm_ref, cw_ref, cb_ref, cnorm_ref, wout_ref,
                  h2_ref, sd_ref, sq_ref, sc_ref,
                  extq_ref, extc_ref, shc_ref, q_s, k_s, v_s, mix_s,
                  *, G, T, C, NT, chain):
    R = G * T
    t = pl.program_id(1)

    @pl.when(t == 0)
    def _():
        extq_ref[:, QKV_PAD - (SHORT_CONV - 1):QKV_PAD, :] = sq0_ref[...]
        extc_ref[:, CONV_PAD - (CONV_WIDTH - 1):CONV_PAD, :] = sc0_ref[...]
        sd_ref[...] = sd0_ref[...]

    extq_ref[:, QKV_PAD:QKV_PAD + T, :] = qkv_ref[...]
    extc_ref[:, CONV_PAD:CONV_PAD + T, :] = cin_ref[...]

    qk_scale = DK ** -0.5
    rbq = min(16, T)
    for g in range(G):
        for r0 in range(0, T, rbq):
            base = QKV_PAD - (SHORT_CONV - 1) + r0
            read = lambda j, i, g=g, base=base: extq_ref[g, pl.ds(base + j + SUBLANES * i, SUBLANES), :]
            s = _silu(_tap_sum(read, qw_ref, SHORT_CONV, rbq))
            rows = slice(g * T + r0, g * T + r0 + rbq)
            for h in range(N_HEADS):
                qh = s[:, h * DK:(h + 1) * DK]
                kh = s[:, DK_TOT + h * DK:DK_TOT + (h + 1) * DK]
                q_s[rows, h * DK:(h + 1) * DK] = qh * (lax.rsqrt(jnp.sum(qh * qh, -1, keepdims=True) + EPS) * qk_scale)
                k_s[rows, h * DK:(h + 1) * DK] = kh * lax.rsqrt(jnp.sum(kh * kh, -1, keepdims=True) + EPS)
            v_s[rows, :] = s[:, 2 * DK_TOT:]
    sq_ref[...] = extq_ref[:, T + QKV_PAD - (SHORT_CONV - 1):T + QKV_PAD, :]

    first = CONV_PAD - (CONV_WIDTH - 1)
    sh_rows = T + CONV_PAD - SUBLANES
    for g in range(G):
        for b in range(1, SUBLANES):
            for r0 in range(0, sh_rows, 32):
                rb = min(32, sh_rows - r0)
                shc_ref[g, b - 1, r0:r0 + rb, :] = extc_ref[g, r0 + b:r0 + b + rb, :]
    rbc = min(32, T)
    for g in range(G):
        for r0 in range(0, T, rbc):
            def read(j, i, g=g, r0=r0):
                a, b = divmod(j + first, SUBLANES)
                rows = pl.ds(r0 + SUBLANES * (a + i), SUBLANES)
                return extc_ref[g, rows, :] if b == 0 else shc_ref[g, b - 1, rows, :]
            acc = _tap_sum(read, cw_ref, CONV_WIDTH, rbc)
            c = _silu(_rms(acc + cb_ref[...], cnorm_ref[...]))
            mix_s[g * T + r0:g * T + r0 + rbc, DV_TOT:] = c
    sc_ref[...] = extc_ref[:, T + CONV_PAD - (CONV_WIDTH - 1):T + CONV_PAD, :]

    if NT > 1:
        extq_ref[:, 0:QKV_PAD, :] = extq_ref[:, T:T + QKV_PAD, :]
        extc_ref[:, 0:CONV_PAD, :] = extc_ref[:, T:T + CONV_PAD, :]

    ba = ba_ref[...].reshape(R, LANES)
    beta_all = jax.nn.sigmoid(ba)
    xg = ba + dtb_ref[...]
    g_all = -jnp.exp(alog_ref[...]) * (jnp.maximum(xg, 0.0) + jnp.log1p(jnp.exp(-jnp.abs(xg))))
    gc_all = _block_cumsum(g_all, C)
    gc_all_t = _transpose_rows(gc_all)
    if chain:
        gtail_all = None
    else:
        gtail_all = _block_cumsum(g_all, C, reverse=True) - g_all

    Rs = C if chain else R
    row_i = lax.broadcasted_iota(jnp.int32, (Rs, Rs), 0)
    col_i = lax.broadcasted_iota(jnp.int32, (Rs, Rs), 1)
    same = (row_i // C) == (col_i // C)
    causal = same & (row_i >= col_i)
    strict = same & (row_i > col_i)

    pairs = [(h, c0) for c0 in range(0, R, Rs) for h in range(N_HEADS)]
    hsl = lambda h: slice(h * DK, (h + 1) * DK)
    k_t_all = [_transpose_rows(k_s[:, hsl(h)]) for h in range(N_HEADS)]
    gc_cs, gc_rs, kbs, decays, kks = {}, {}, {}, {}, {}
    for p in pairs:
        h, c0 = p
        rs = slice(c0, c0 + Rs)
        gc_cs[p] = gc_all[rs, N_HEADS + h:N_HEADS + h + 1]
        gc_rs[p] = gc_all_t[N_HEADS + h:N_HEADS + h + 1, c0:c0 + Rs]
        decays[p] = jnp.exp(jnp.where(causal, gc_cs[p] - gc_rs[p], -jnp.inf))
        kbs[p] = k_s[rs, hsl(h)] * beta_all[rs, h:h + 1]
        kks[p] = _sdot(kbs[p], k_t_all[h][:, c0:c0 + Rs])

    eye = (row_i == col_i).astype(F32)
    xs = {p: jnp.where(strict, kks[p] * decays[p], 0.0) for p in pairs}
    t_inv = {p: eye - xs[p] for p in pairs}
    s = 2
    while s < C:
        xs = {p: _sdot(xs[p], xs[p]) for p in pairs}
        t_inv = {p: t_inv[p] + _sdot(t_inv[p], xs[p]) for p in pairs}
        s *= 2

    us, ws, atts, qgs, o_loc, n_loc, kws, e_last = {}, {}, {}, {}, {}, {}, {}, {}
    for p in pairs:
        h, c0 = p
        rs = slice(c0, c0 + Rs)
        egc = jnp.exp(gc_cs[p])
        vb = v_s[rs, hsl(h)] * beta_all[rs, h:h + 1]
        uw = _sdot(t_inv[p], jnp.concatenate([vb, kbs[p] * egc], axis=1))
        q = q_s[rs, hsl(h)]
        k_t = k_t_all[h][:, c0:c0 + Rs]
        att = _sdot(q, k_t) * decays[p]
        if chain:
            g_last = gc_cs[p][Rs - 1:Rs, :]
            kuw = _sdot(k_t * jnp.exp(g_last - gc_rs[p]), uw)
            auw = _sdot(att, uw)
            n_loc[p], kws[p] = kuw[:, :DV], kuw[:, DV:]
            o_loc[p], qgs[p] = auw[:, :DV], q * egc - auw[:, DV:]
            e_last[p] = jnp.exp(g_last)
        else:
            us[p], ws[p], atts[p], qgs[p] = uw[:, :DV], uw[:, DV:], att, q * egc

    for p in pairs:
        h, c0 = p
        rs = slice(c0, c0 + Rs)
        if chain:
            s_old = sd_ref[0, h]
            o = _sdot(qgs[p], s_old) + o_loc[p]
            sd_ref[0, h] = s_old * e_last[p] - _sdot(kws[p], s_old) + n_loc[p]
        else:
            gtail_c = gtail_all[rs, N_HEADS + h:N_HEADS + h + 1]
            k_til_t = _transpose_rows(k_s[rs, hsl(h)] * jnp.exp(gtail_c))
            wss, qss = [], []
            for b in range(Rs // C):
                bs = slice(b * C, (b + 1) * C)
                wq = _sdot(jnp.concatenate([ws[p][bs], qgs[p][bs]], axis=0), sd_ref[b, h])
                wss.append(wq[:C])
                qss.append(wq[C:])
            v_new = us[p] - jnp.concatenate(wss, axis=0)
            o = jnp.concatenate(qss, axis=0) + _sdot(atts[p], v_new)
            for b in range(Rs // C):
                bs = slice(b * C, (b + 1) * C)
                e_last = jnp.exp(gc_cs[p][(b + 1) * C - 1:(b + 1) * C, :])
                sd_ref[b, h] = sd_ref[b, h] * e_last + _sdot(k_til_t[:, bs], v_new[bs])
        zh = z_ref[0, rs, hsl(h)] if chain else z_ref[:, :, hsl(h)].reshape(R, DV)
        mix_s[rs, hsl(h)] = _rms(o, onorm_ref[...]) * _silu(zh)

    mix = _wdot(mix_s[...], wout_ref[...])
    h2_ref[...] = (h1_ref[...].reshape(R, D_MODEL) + mix).reshape(G, T, D_MODEL)


def _mixer(h1, qkv, z, cin, ba, sd0, sq0, sc0, prm, *, G, T, C, shared_init):
    B, L, _ = h1.shape
    NB, NT = B // G, L // T
    chain = G == 1
    assert chain or NT == 1
    R = G * T
    seq = lambda w: pl.BlockSpec((G, T, w), lambda b, t: (b, t, 0))
    if shared_init:
        st = lambda shape: pl.BlockSpec((G,) + shape, lambda b, t: (0,) * (1 + len(shape)))
    else:
        st = lambda shape: pl.BlockSpec((G,) + shape, lambda b, t: (b,) + (0,) * len(shape))
    out_st = lambda shape: pl.BlockSpec((G,) + shape, lambda b, t: (b,) + (0,) * len(shape))
    cst = lambda a: pl.BlockSpec(a.shape, lambda b, t: (0,) * a.ndim)
    sd_shape, sq_shape, sc_shape = (N_HEADS, DK, DV), (SHORT_CONV - 1, QKV_DIM), (CONV_WIDTH - 1, C_CONV)
    kern = functools.partial(_mixer_kernel, G=G, T=T, C=C, NT=NT, chain=chain)
    return pl.pallas_call(
        kern,
        grid=(NB, NT),
        in_specs=[seq(D_MODEL), seq(QKV_DIM), seq(DV_TOT), seq(C_CONV), seq(LANES),
                  st(sd_shape), st(sq_shape), st(sc_shape)] + [cst(a) for a in prm],
        out_specs=[seq(D_MODEL), out_st(sd_shape), out_st(sq_shape), out_st(sc_shape)],
        out_shape=[jax.ShapeDtypeStruct((B, L, D_MODEL), F32), jax.ShapeDtypeStruct((B,) + sd_shape, F32),
                   jax.ShapeDtypeStruct((B,) + sq_shape, F32), jax.ShapeDtypeStruct((B,) + sc_shape, F32)],
        scratch_shapes=[pltpu.VMEM((G, QKV_PAD + T, QKV_DIM), F32), pltpu.VMEM((G, CONV_PAD + T, C_CONV), F32),
                        pltpu.VMEM((G, SUBLANES - 1, T + CONV_PAD - SUBLANES, C_CONV), F32),
                        pltpu.VMEM((R, DK_TOT), F32), pltpu.VMEM((R, DK_TOT), F32), pltpu.VMEM((R, DV_TOT), F32),
                        pltpu.VMEM((R, DV_TOT + C_CONV), F32)],
        compiler_params=pltpu.CompilerParams(dimension_semantics=("arbitrary", "arbitrary"),
                                             vmem_limit_bytes=VMEM_LIMIT),
        name="mixer",
    )(h1, qkv, z, cin, ba, sd0, sq0, sc0, *prm)


def kernel(x_prompt, x_sample, state_delta, state_qkv_conv, state_conv, meta_tokens, ffn1_norm, ffn1_w1, ffn1_w3,
           ffn1_w2, mix_norm, w_in, qkv_conv_w, a_log, dt_bias, o_norm, conv_w, conv_b, conv_norm, w_out, ffn2_norm,
           ffn2_w1, ffn2_w3, ffn2_w2, final_norm):
    bp, seq_len, _ = x_prompt.shape
    bs, dec_len, _ = x_sample.shape
    l = 0
    row = lambda v: v.reshape(1, -1).astype(F32)

    cut = QKV_DIM + DV_TOT
    win = jnp.concatenate([w_in[l][:, :cut], w_in[l][:, cut + 2 * N_HEADS:], w_in[l][:, cut:cut + 2 * N_HEADS],
                           jnp.zeros((D_MODEL, LANES - 2 * N_HEADS), w_in.dtype)], axis=1).astype(BF16)
    f1 = (row(ffn1_norm[l]), ffn1_w1[l].astype(BF16), ffn1_w3[l].astype(BF16), ffn1_w2[l].astype(BF16))
    f2 = (row(ffn2_norm[l]), ffn2_w1[l].astype(BF16), ffn2_w3[l].astype(BF16), ffn2_w2[l].astype(BF16))
    lane_row = lambda v: jnp.zeros((1, LANES), F32).at[0, N_HEADS:2 * N_HEADS].set(v.astype(F32))
    taps8 = lambda w: jnp.broadcast_to(w.astype(F32)[:, None, :], (w.shape[0], SUBLANES, w.shape[1]))
    prm = (taps8(qkv_conv_w[l]), lane_row(a_log[l]), lane_row(dt_bias[l]), row(o_norm[l]),
           taps8(conv_w[l]), row(conv_b[l]), row(conv_norm[l]), w_out[l].astype(BF16))
    gm, gf = row(mix_norm[l]), row(final_norm)

    front = lambda x, rows: _front(x, *f1, gm, win, rows)

    m = [a.reshape(1, N_META, -1) for a in front(meta_tokens.astype(F32), N_META)]
    zeros = lambda *s: jnp.zeros(s, F32)
    _, sd_m, sq_m, sc_m = _mixer(*m, zeros(1, N_HEADS, DK, DV), zeros(1, SHORT_CONV - 1, QKV_DIM),
                                 zeros(1, CONV_WIDTH - 1, C_CONV), prm, G=1, T=N_META, C=N_META, shared_init=True)

    p = [a.reshape(bp, seq_len, -1) for a in front(x_prompt.reshape(bp * seq_len, D_MODEL), 256)]
    h2p, sd_p, sq_p, sc_p = _mixer(*p, sd_m, sq_m, sc_m, prm, G=1, T=256, C=CHUNK, shared_init=True)
    y_prompt = _back(h2p.reshape(bp * seq_len, D_MODEL), *f2, gf, 256).reshape(bp, seq_len, D_MODEL)

    s = [a.reshape(bs, dec_len, -1) for a in front(x_sample.reshape(bs * dec_len, D_MODEL), 256)]
    h2s, sd_s, sq_s, sc_s = _mixer(*s, state_delta[l], state_qkv_conv[l], state_conv[l], prm,
                                   G=16, T=dec_len, C=dec_len, shared_init=False)
    y_sample = _back(h2s.reshape(bs * dec_len, D_MODEL), *f2, gf, 256).reshape(bs, dec_len, D_MODEL)

    return (y_prompt, y_sample, sd_p[None], sq_p[None], sc_p[None], sd_s[None], sq_s[None], sc_s[None])
```

```python
import functools

import jax
import jax.numpy as jnp
from jax import lax
from jax.experimental import pallas as pl
from jax.experimental.pallas import tpu as pltpu

D_MODEL = 1024
D_FF = 2816
N_HEADS = 4
DK = 128
DV = 128
DK_TOT = N_HEADS * DK
DV_TOT = N_HEADS * DV
QKV_DIM = 2 * DK_TOT + DV_TOT
C_CONV = 512
SHORT_CONV = 4
CONV_WIDTH = 31
CHUNK = 64
N_META = 16
EPS = 1e-6

LANES = 128
SUBLANES = 8
Z_OFF = QKV_DIM
VAL_OFF = Z_OFF + DV_TOT
GATE_OFF = VAL_OFF + C_CONV
BA_OFF = GATE_OFF + C_CONV
QKV_PAD = SUBLANES
CONV_PAD = 4 * SUBLANES
FRONT_ROWS = 256
BACK_ROWS = 512
MIX_ROWS = 256
SAMPLE_GROUP = 16
VMEM_LIMIT = 56 * 1024 * 1024

F32 = jnp.float32
BF16 = jnp.bfloat16


def _rms(x, g):
    return x * lax.rsqrt(jnp.mean(x * x, axis=-1, keepdims=True) + EPS) * g


def _silu(x):
    return x * jax.nn.sigmoid(x)


def _wdot(x, w):
    return jnp.dot(x.astype(BF16), w, preferred_element_type=F32)


def _sdot(a, b):
    return jnp.dot(a, b, preferred_element_type=F32)


def _stack(*xs):
    return jnp.concatenate(xs, axis=0)


def _ffn_half(h, g_ref, w1_ref, w3_ref, w2_ref):
    u = _rms(h, g_ref[...]).astype(BF16)
    a = jnp.dot(u, w1_ref[...], preferred_element_type=F32)
    b = jnp.dot(u, w3_ref[...], preferred_element_type=F32)
    hid = (_silu(a) * b).astype(BF16)
    return h + 0.5 * jnp.dot(hid, w2_ref[...], preferred_element_type=F32)


def _front_kernel(x_ref, g1_ref, w1_ref, w3_ref, w2_ref, gm_ref, win_ref,
                  h1_ref, qkv_ref, z_ref, cin_ref, ba_ref):
    h1 = _ffn_half(x_ref[...], g1_ref, w1_ref, w3_ref, w2_ref)
    h1_ref[...] = h1
    p = _wdot(_rms(h1, gm_ref[...]), win_ref[...])
    qkv_ref[...] = p[:, :Z_OFF]
    z_ref[...] = p[:, Z_OFF:VAL_OFF]
    cin_ref[...] = p[:, VAL_OFF:GATE_OFF] * jax.nn.sigmoid(p[:, GATE_OFF:BA_OFF])
    ba_ref[...] = p[:, BA_OFF:]


def _back_kernel(h_ref, g2_ref, w1_ref, w3_ref, w2_ref, gf_ref, y_ref):
    y_ref[...] = _rms(_ffn_half(h_ref[...], g2_ref, w1_ref, w3_ref, w2_ref), gf_ref[...])


def _const_spec(shape):
    nd = len(shape)
    return pl.BlockSpec(shape, lambda *_: (0,) * nd, pipeline_mode=pl.Buffered(1))


def _row_spec(rows, width):
    return pl.BlockSpec((rows, width), lambda i: (i, 0))


def _front(x, g1, w1, w3, w2, gm, win, rows):
    n = x.shape[0]
    widths = (D_MODEL, QKV_DIM, DV_TOT, C_CONV, LANES)
    return pl.pallas_call(
        _front_kernel,
        grid=(n // rows,),
        in_specs=[_row_spec(rows, D_MODEL), _const_spec(g1.shape), _const_spec(w1.shape), _const_spec(w3.shape),
                  _const_spec(w2.shape), _const_spec(gm.shape), _const_spec(win.shape)],
        out_specs=[_row_spec(rows, w) for w in widths],
        out_shape=[jax.ShapeDtypeStruct((n, w), F32) for w in widths],
        compiler_params=pltpu.CompilerParams(dimension_semantics=("arbitrary",), vmem_limit_bytes=VMEM_LIMIT),
        name="front",
    )(x, g1, w1, w3, w2, gm, win)


def _back(h, g2, w1, w3, w2, gf, rows):
    n = h.shape[0]
    return pl.pallas_call(
        _back_kernel,
        grid=(n // rows,),
        in_specs=[_row_spec(rows, D_MODEL), _const_spec(g2.shape), _const_spec(w1.shape), _const_spec(w3.shape),
                  _const_spec(w2.shape), _const_spec(gf.shape)],
        out_specs=_row_spec(rows, D_MODEL),
        out_shape=jax.ShapeDtypeStruct((n, D_MODEL), F32),
        compiler_params=pltpu.CompilerParams(dimension_semantics=("arbitrary",), vmem_limit_bytes=VMEM_LIMIT),
        name="back",
    )(h, g2, w1, w3, w2, gf)


def _block_cumsum(x, block, reverse=False):
    rows = x.shape[0]
    r = lax.broadcasted_iota(jnp.int32, x.shape, 0) % block
    s = 1
    while s < block:
        if reverse:
            x = x + jnp.where(r < block - s, pltpu.roll(x, rows - s, axis=0), 0.0)
        else:
            x = x + jnp.where(r >= s, pltpu.roll(x, s, axis=0), 0.0)
        s *= 2
    return x


def _tap_sum(read, w_ref, n_taps, rows):
    accs = [None] * (rows // SUBLANES)
    for j in range(n_taps):
        w8 = w_ref[j]
        for i in range(len(accs)):
            tap = read(j, i) * w8
            accs[i] = tap if accs[i] is None else accs[i] + tap
    return accs[0] if len(accs) == 1 else jnp.concatenate(accs, axis=0)


def _transpose_rows(x):
    rows = x.shape[0]
    pad = (-rows) % LANES
    if pad:
        x = jnp.concatenate([x, jnp.zeros((pad, x.shape[1]), x.dtype)], axis=0)
    return x.T[:, :rows]


def _mixer_kernel(h1_ref, qkv_ref, z_ref, cin_ref, ba_ref, sd0_ref, sq0_ref, sc0_ref,
                  qw_ref, alog_ref, dtb_ref, onorm_ref, cw_ref, cb_ref, cnorm_ref, wout_ref,
                  h2_ref, sd_ref, sq_ref, sc_ref,
                  extq_ref, extc_ref, shc_ref, q_s, k_s, v_s, mix_s,
                  *, G, T, C, NT, chain):
    R = G * T
    t = pl.program_id(1)

    @pl.when(t == 0)
    def _():
        extq_ref[:, QKV_PAD - (SHORT_CONV - 1):QKV_PAD, :] = sq0_ref[...]
        extc_ref[:, CONV_PAD - (CONV_WIDTH - 1):CONV_PAD, :] = sc0_ref[...]
        sd_ref[...] = sd0_ref[...]

    extq_ref[:, QKV_PAD:QKV_PAD + T, :] = qkv_ref[...]
    extc_ref[:, CONV_PAD:CONV_PAD + T, :] = cin_ref[...]

    qk_scale = DK ** -0.5
    rbq = min(2 * SUBLANES, T)
    for g in range(G):
        for r0 in range(0, T, rbq):
            base = QKV_PAD - (SHORT_CONV - 1) + r0
            read = lambda j, i, g=g, base=base: extq_ref[g, pl.ds(base + j + SUBLANES * i, SUBLANES), :]
            s = _silu(_tap_sum(read, qw_ref, SHORT_CONV, rbq))
            rows = slice(g * T + r0, g * T + r0 + rbq)
            for h in range(N_HEADS):
                qh = s[:, h * DK:(h + 1) * DK]
                kh = s[:, DK_TOT + h * DK:DK_TOT + (h + 1) * DK]
                q_s[rows, h * DK:(h + 1) * DK] = qh * (lax.rsqrt(jnp.sum(qh * qh, -1, keepdims=True) + EPS) * qk_scale)
                k_s[rows, h * DK:(h + 1) * DK] = kh * lax.rsqrt(jnp.sum(kh * kh, -1, keepdims=True) + EPS)
            v_s[rows, :] = s[:, 2 * DK_TOT:]
    sq_ref[...] = extq_ref[:, T + QKV_PAD - (SHORT_CONV - 1):T + QKV_PAD, :]

    first = CONV_PAD - (CONV_WIDTH - 1)
    sh_rows = T + CONV_PAD - SUBLANES
    for g in range(G):
        for b in range(1, SUBLANES):
            for r0 in range(0, sh_rows, 32):
                rb = min(32, sh_rows - r0)
                shc_ref[g, b - 1, r0:r0 + rb, :] = extc_ref[g, r0 + b:r0 + b + rb, :]
    rbc = min(8 * SUBLANES, T)
    for g in range(G):
        for r0 in range(0, T, rbc):
            def read(j, i, g=g, r0=r0):
                a, b = divmod(j + first, SUBLANES)
                rows = pl.ds(r0 + SUBLANES * (a + i), SUBLANES)
                return extc_ref[g, rows, :] if b == 0 else shc_ref[g, b - 1, rows, :]
            acc = _tap_sum(read, cw_ref, CONV_WIDTH, rbc)
            mix_s[g * T + r0:g * T + r0 + rbc, DV_TOT:] = _silu(_rms(acc + cb_ref[...], cnorm_ref[...]))
    sc_ref[...] = extc_ref[:, T + CONV_PAD - (CONV_WIDTH - 1):T + CONV_PAD, :]

    if NT > 1:
        extq_ref[:, 0:QKV_PAD, :] = extq_ref[:, T:T + QKV_PAD, :]
        extc_ref[:, 0:CONV_PAD, :] = extc_ref[:, T:T + CONV_PAD, :]

    ba = ba_ref[...].reshape(R, LANES)
    beta_all = jax.nn.sigmoid(ba)
    xg = ba + dtb_ref[...]
    g_all = -jnp.exp(alog_ref[...]) * (jnp.maximum(xg, 0.0) + jnp.log1p(jnp.exp(-jnp.abs(xg))))
    gc_all = _block_cumsum(g_all, C)
    gc_all_t = _transpose_rows(gc_all)
    if not chain:
        gtail_all = _block_cumsum(g_all, C, reverse=True) - g_all

    Rs = C if chain else R
    row_i = lax.broadcasted_iota(jnp.int32, (Rs, Rs), 0)
    col_i = lax.broadcasted_iota(jnp.int32, (Rs, Rs), 1)
    same = (row_i // C) == (col_i // C)
    causal = same & (row_i >= col_i)
    strict = same & (row_i > col_i)

    pairs = [(h, c0) for c0 in range(0, R, Rs) for h in range(N_HEADS)]
    hsl = lambda h: slice(h * DK, (h + 1) * DK)
    k_t_all = [_transpose_rows(k_s[:, hsl(h)]) for h in range(N_HEADS)]
    gc_cs, gc_rs, kbs, decays, qs, kq = {}, {}, {}, {}, {}, {}
    for p in pairs:
        h, c0 = p
        rs = slice(c0, c0 + Rs)
        gc_cs[p] = gc_all[rs, N_HEADS + h:N_HEADS + h + 1]
        gc_rs[p] = gc_all_t[N_HEADS + h:N_HEADS + h + 1, c0:c0 + Rs]
        decays[p] = jnp.exp(jnp.where(causal, gc_cs[p] - gc_rs[p], -jnp.inf))
        kbs[p] = k_s[rs, hsl(h)] * beta_all[rs, h:h + 1]
        qs[p] = q_s[rs, hsl(h)]
        kq[p] = _sdot(_stack(kbs[p], qs[p]), k_t_all[h][:, c0:c0 + Rs])

    eye = (row_i == col_i).astype(F32)
    a_mat = {p: jnp.where(strict, kq[p][:Rs] * decays[p], 0.0) for p in pairs}
    t_inv = {p: eye - a_mat[p] for p in pairs}
    if C > 2:
        xs = {p: _sdot(a_mat[p], a_mat[p]) for p in pairs}
        m = 2
        while 2 * m < C:
            prod = {p: _sdot(_stack(xs[p], t_inv[p]), xs[p]) for p in pairs}
            t_inv = {p: t_inv[p] + prod[p][Rs:] for p in pairs}
            xs = {p: prod[p][:Rs] for p in pairs}
            m *= 2
        t_inv = {p: t_inv[p] + _sdot(t_inv[p], xs[p]) for p in pairs}

    us, ws, atts, qgs, o_loc, n_loc, kws, e_last = {}, {}, {}, {}, {}, {}, {}, {}
    for p in pairs:
        h, c0 = p
        rs = slice(c0, c0 + Rs)
        egc = jnp.exp(gc_cs[p])
        vb = v_s[rs, hsl(h)] * beta_all[rs, h:h + 1]
        uw = _sdot(t_inv[p], jnp.concatenate([vb, kbs[p] * egc], axis=1))
        att = kq[p][Rs:] * decays[p]
        if chain:
            g_last = gc_cs[p][Rs - 1:Rs, :]
            k_til_t = k_t_all[h][:, c0:c0 + Rs] * jnp.exp(g_last - gc_rs[p])
            loc = _sdot(_stack(k_til_t, att), uw)
            n_loc[p], kws[p] = loc[:DK, :DV], loc[:DK, DV:]
            o_loc[p], qgs[p] = loc[DK:, :DV], qs[p] * egc - loc[DK:, DV:]
            e_last[p] = jnp.exp(g_last)
        else:
            us[p], ws[p], atts[p], qgs[p] = uw[:, :DV], uw[:, DV:], att, qs[p] * egc

    for p in pairs:
        h, c0 = p
        rs = slice(c0, c0 + Rs)
        if chain:
            s_old = sd_ref[0, h]
            prod = _sdot(_stack(kws[p], qgs[p]), s_old)
            o = prod[DK:] + o_loc[p]
            sd_ref[0, h] = s_old * e_last[p] - prod[:DK] + n_loc[p]
        else:
            gtail_c = gtail_all[rs, N_HEADS + h:N_HEADS + h + 1]
            k_til_t = _transpose_rows(k_s[rs, hsl(h)] * jnp.exp(gtail_c))
            wss, qss = [], []
            for b in range(Rs // C):
                bs = slice(b * C, (b + 1) * C)
                wq = _sdot(_stack(ws[p][bs], qgs[p][bs]), sd_ref[b, h])
                wss.append(wq[:C])
                qss.append(wq[C:])
            v_new = us[p] - jnp.concatenate(wss, axis=0)
            o = jnp.concatenate(qss, axis=0) + _sdot(atts[p], v_new)
            for b in range(Rs // C):
                bs = slice(b * C, (b + 1) * C)
                e_b = jnp.exp(gc_cs[p][(b + 1) * C - 1:(b + 1) * C, :])
                sd_ref[b, h] = sd_ref[b, h] * e_b + _sdot(k_til_t[:, bs], v_new[bs])
        zh = z_ref[0, rs, hsl(h)] if chain else z_ref[:, :, hsl(h)].reshape(R, DV)
        mix_s[rs, hsl(h)] = _rms(o, onorm_ref[...]) * _silu(zh)

    mix = _wdot(mix_s[...], wout_ref[...])
    h2_ref[...] = (h1_ref[...].reshape(R, D_MODEL) + mix).reshape(G, T, D_MODEL)


def _mixer(h1, qkv, z, cin, ba, sd0, sq0, sc0, prm, *, G, T, C, shared_init):
    B, L, _ = h1.shape
    NB, NT = B // G, L // T
    chain = G == 1
    assert chain or NT == 1
    R = G * T
    seq = lambda w: pl.BlockSpec((G, T, w), lambda b, t: (b, t, 0))
    if shared_init:
        st = lambda shape: pl.BlockSpec((G,) + shape, lambda b, t: (0,) * (1 + len(shape)))
    else:
        st = lambda shape: pl.BlockSpec((G,) + shape, lambda b, t: (b,) + (0,) * len(shape))
    out_st = lambda shape: pl.BlockSpec((G,) + shape, lambda b, t: (b,) + (0,) * len(shape))
    cst = lambda a: pl.BlockSpec(a.shape, lambda b, t: (0,) * a.ndim)
    sd_shape, sq_shape, sc_shape = (N_HEADS, DK, DV), (SHORT_CONV - 1, QKV_DIM), (CONV_WIDTH - 1, C_CONV)
    kern = functools.partial(_mixer_kernel, G=G, T=T, C=C, NT=NT, chain=chain)
    return pl.pallas_call(
        kern,
        grid=(NB, NT),
        in_specs=[seq(D_MODEL), seq(QKV_DIM), seq(DV_TOT), seq(C_CONV), seq(LANES),
                  st(sd_shape), st(sq_shape), st(sc_shape)] + [cst(a) for a in prm],
        out_specs=[seq(D_MODEL), out_st(sd_shape), out_st(sq_shape), out_st(sc_shape)],
        out_shape=[jax.ShapeDtypeStruct((B, L, D_MODEL), F32), jax.ShapeDtypeStruct((B,) + sd_shape, F32),
                   jax.ShapeDtypeStruct((B,) + sq_shape, F32), jax.ShapeDtypeStruct((B,) + sc_shape, F32)],
        scratch_shapes=[pltpu.VMEM((G, QKV_PAD + T, QKV_DIM), F32), pltpu.VMEM((G, CONV_PAD + T, C_CONV), F32),
                        pltpu.VMEM((G, SUBLANES - 1, T + CONV_PAD - SUBLANES, C_CONV), F32),
                        pltpu.VMEM((R, DK_TOT), F32), pltpu.VMEM((R, DK_TOT), F32), pltpu.VMEM((R, DV_TOT), F32),
                        pltpu.VMEM((R, DV_TOT + C_CONV), F32)],
        compiler_params=pltpu.CompilerParams(dimension_semantics=("arbitrary", "arbitrary"),
                                             vmem_limit_bytes=VMEM_LIMIT),
        name="mixer",
    )(h1, qkv, z, cin, ba, sd0, sq0, sc0, *prm)


def kernel(x_prompt, x_sample, state_delta, state_qkv_conv, state_conv, meta_tokens, ffn1_norm, ffn1_w1, ffn1_w3,
           ffn1_w2, mix_norm, w_in, qkv_conv_w, a_log, dt_bias, o_norm, conv_w, conv_b, conv_norm, w_out, ffn2_norm,
           ffn2_w1, ffn2_w3, ffn2_w2, final_norm):
    bp, seq_len, _ = x_prompt.shape
    bs, dec_len, _ = x_sample.shape
    assert state_delta.shape[0] == 1
    row = lambda v: v.reshape(1, -1).astype(F32)
    mat = lambda w: w.reshape(w.shape[1:]).astype(BF16)

    w_in0 = w_in.reshape(w_in.shape[1:])
    win = jnp.concatenate([w_in0[:, :VAL_OFF], w_in0[:, VAL_OFF + 2 * N_HEADS:], w_in0[:, VAL_OFF:VAL_OFF + 2 * N_HEADS],
                           jnp.zeros((D_MODEL, LANES - 2 * N_HEADS), w_in.dtype)], axis=1).astype(BF16)
    f1 = (row(ffn1_norm), mat(ffn1_w1), mat(ffn1_w3), mat(ffn1_w2))
    f2 = (row(ffn2_norm), mat(ffn2_w1), mat(ffn2_w3), mat(ffn2_w2))
    lane_row = lambda v: jnp.zeros((1, LANES), F32).at[0, N_HEADS:2 * N_HEADS].set(v.reshape(-1).astype(F32))
    taps8 = lambda w: jnp.broadcast_to(w.astype(F32).reshape(w.shape[1], 1, w.shape[2]),
                                       (w.shape[1], SUBLANES, w.shape[2]))
    prm = (taps8(qkv_conv_w), lane_row(a_log), lane_row(dt_bias), row(o_norm),
           taps8(conv_w), row(conv_b), row(conv_norm), mat(w_out))
    gm, gf = row(mix_norm), row(final_norm)

    front = lambda x, rows: _front(x, *f1, gm, win, rows)

    m = [a.reshape(1, N_META, -1) for a in front(meta_tokens.astype(F32), N_META)]
    zeros = lambda *s: jnp.zeros(s, F32)
    _, sd_m, sq_m, sc_m = _mixer(*m, zeros(1, N_HEADS, DK, DV), zeros(1, SHORT_CONV - 1, QKV_DIM),
                                 zeros(1, CONV_WIDTH - 1, C_CONV), prm, G=1, T=N_META, C=N_META, shared_init=True)

    p = [a.reshape(bp, seq_len, -1) for a in front(x_prompt.reshape(bp * seq_len, D_MODEL), FRONT_ROWS)]
    h2p, sd_p, sq_p, sc_p = _mixer(*p, sd_m, sq_m, sc_m, prm, G=1, T=MIX_ROWS, C=CHUNK, shared_init=True)
    y_prompt = _back(h2p.reshape(bp * seq_len, D_MODEL), *f2, gf, BACK_ROWS).reshape(bp, seq_len, D_MODEL)

    s = [a.reshape(bs, dec_len, -1) for a in front(x_sample.reshape(bs * dec_len, D_MODEL), FRONT_ROWS)]
    h2s, sd_s, sq_s, sc_s = _mixer(*s, state_delta.reshape(bs, N_HEADS, DK, DV),
                                   state_qkv_conv.reshape(bs, SHORT_CONV - 1, QKV_DIM),
                                   state_conv.reshape(bs, CONV_WIDTH - 1, C_CONV), prm,
                                   G=SAMPLE_GROUP, T=dec_len, C=dec_len, shared_init=False)
    y_sample = _back(h2s.reshape(bs * dec_len, D_MODEL), *f2, gf, BACK_ROWS).reshape(bs, dec_len, D_MODEL)

    return (y_prompt, y_sample, sd_p[None], sq_p[None], sc_p[None], sd_s[None], sq_s[None], sc_s[None])
```

```python
import functools

import jax
import jax.numpy as jnp
from jax import lax
from jax.experimental import pallas as pl
from jax.experimental.pallas import tpu as pltpu

D_MODEL = 1024
D_FF = 2816
N_HEADS = 4
DK = 128
DV = 128
DK_TOT = N_HEADS * DK
DV_TOT = N_HEADS * DV
QKV_DIM = 2 * DK_TOT + DV_TOT
C_CONV = 512
SHORT_CONV = 4
CONV_WIDTH = 31
CHUNK = 64
N_META = 16
EPS = 1e-6

LANES = 128
SUBLANES = 8
QKV_PAD = SUBLANES
CONV_PAD = 4 * SUBLANES
FRONT_ROWS = 256
BACK_ROWS = 512
MIX_ROWS = 256
SAMPLE_GROUP = 16
VMEM_LIMIT = 56 * 1024 * 1024

F32 = jnp.float32
BF16 = jnp.bfloat16


def _rms(x, g):
    return x * lax.rsqrt(jnp.mean(x * x, axis=-1, keepdims=True) + EPS) * g


def _silu(x):
    return x * jax.nn.sigmoid(x)


def _wdot(x, w):
    return jnp.dot(x.astype(BF16), w, preferred_element_type=F32)


def _sdot(a, b):
    return jnp.dot(a, b, preferred_element_type=F32)


def _stack(*xs):
    return jnp.concatenate(xs, axis=0)


def _ffn_half(h, g_ref, w1_ref, w3_ref, w2_ref):
    u = _rms(h, g_ref[...]).astype(BF16)
    a = jnp.dot(u, w1_ref[...], preferred_element_type=F32)
    b = jnp.dot(u, w3_ref[...], preferred_element_type=F32)
    hid = (_silu(a) * b).astype(BF16)
    return h + 0.5 * jnp.dot(hid, w2_ref[...], preferred_element_type=F32)


def _front_kernel(x_ref, g1_ref, w1_ref, w3_ref, w2_ref, gm_ref, wqkvz_ref, wglu_ref, wba_ref,
                  h1_ref, qkv_ref, z_ref, cin_ref, ba_ref):
    h1 = _ffn_half(x_ref[...], g1_ref, w1_ref, w3_ref, w2_ref)
    h1_ref[...] = h1
    u = _rms(h1, gm_ref[...]).astype(BF16)
    p = jnp.dot(u, wqkvz_ref[...], preferred_element_type=F32)
    qkv_ref[...] = p[:, :QKV_DIM]
    z_ref[...] = p[:, QKV_DIM:]
    glu = jnp.dot(u, wglu_ref[...], preferred_element_type=F32)
    cin_ref[...] = glu[:, :C_CONV] * jax.nn.sigmoid(glu[:, C_CONV:])
    ba_ref[...] = jnp.dot(u, wba_ref[...], preferred_element_type=F32)


def _back_kernel(h_ref, g2_ref, w1_ref, w3_ref, w2_ref, gf_ref, y_ref):
    y_ref[...] = _rms(_ffn_half(h_ref[...], g2_ref, w1_ref, w3_ref, w2_ref), gf_ref[...])


def _const_spec(shape):
    nd = len(shape)
    return pl.BlockSpec(shape, lambda *_: (0,) * nd, pipeline_mode=pl.Buffered(1))


def _row_spec(rows, width):
    return pl.BlockSpec((rows, width), lambda i: (i, 0))


def _front(x, prm, rows):
    n = x.shape[0]
    widths = (D_MODEL, QKV_DIM, DV_TOT, C_CONV, LANES)
    return pl.pallas_call(
        _front_kernel,
        grid=(n // rows,),
        in_specs=[_row_spec(rows, D_MODEL)] + [_const_spec(a.shape) for a in prm],
        out_specs=[_row_spec(rows, w) for w in widths],
        out_shape=[jax.ShapeDtypeStruct((n, w), F32) for w in widths],
        compiler_params=pltpu.CompilerParams(dimension_semantics=("arbitrary",), vmem_limit_bytes=VMEM_LIMIT),
        name="front",
    )(x, *prm)


def _back(h, g2, w1, w3, w2, gf, rows):
    n = h.shape[0]
    return pl.pallas_call(
        _back_kernel,
        grid=(n // rows,),
        in_specs=[_row_spec(rows, D_MODEL), _const_spec(g2.shape), _const_spec(w1.shape), _const_spec(w3.shape),
                  _const_spec(w2.shape), _const_spec(gf.shape)],
        out_specs=_row_spec(rows, D_MODEL),
        out_shape=jax.ShapeDtypeStruct((n, D_MODEL), F32),
        compiler_params=pltpu.CompilerParams(dimension_semantics=("arbitrary",), vmem_limit_bytes=VMEM_LIMIT),
        name="back",
    )(h, g2, w1, w3, w2, gf)


def _block_cumsum(x, block, reverse=False):
    rows = x.shape[0]
    r = lax.broadcasted_iota(jnp.int32, x.shape, 0) % block
    s = 1
    while s < block:
        if reverse:
            x = x + jnp.where(r < block - s, pltpu.roll(x, rows - s, axis=0), 0.0)
        else:
            x = x + jnp.where(r >= s, pltpu.roll(x, s, axis=0), 0.0)
        s *= 2
    return x


def _tap_sum(read, w_ref, n_taps, rows):
    accs = [None] * (rows // SUBLANES)
    for j in range(n_taps):
        w8 = w_ref[j]
        for i in range(len(accs)):
            tap = read(j, i) * w8
            accs[i] = tap if accs[i] is None else accs[i] + tap
    return accs[0] if len(accs) == 1 else jnp.concatenate(accs, axis=0)


def _transpose_rows(x):
    rows = x.shape[0]
    pad = (-rows) % LANES
    if pad:
        x = jnp.concatenate([x, jnp.zeros((pad, x.shape[1]), x.dtype)], axis=0)
    return x.T[:, :rows]


def _lane_heads(cols, offset, n_lanes):
    rows = cols.shape[0]
    lane_head = lax.broadcasted_iota(jnp.int32, (rows, n_lanes), 1) // (n_lanes // N_HEADS)
    out = jnp.broadcast_to(cols[:, offset + N_HEADS - 1:offset + N_HEADS], (rows, n_lanes))
    for h in range(N_HEADS - 2, -1, -1):
        out = jnp.where(lane_head == h, cols[:, offset + h:offset + h + 1], out)
    return out


def _head_blocks(x):
    rows, n = x.shape
    lane_head = lax.broadcasted_iota(jnp.int32, (rows, n), 1) // (n // N_HEADS)
    return jnp.concatenate([jnp.where(lane_head == h, x, 0.0) for h in range(N_HEADS)], axis=0)


def _delta_chain(q_s, k_s, v_s, z_ref, sd_ref, mix_s, onorm_ref, beta_all, gc_all, gc_all_t, *, R, C):
    HL = N_HEADS * C
    row_i = lax.broadcasted_iota(jnp.int32, (C, HL), 0)
    col_i = lax.broadcasted_iota(jnp.int32, (C, HL), 1) % C
    eye_p, causal_p, strict_p = row_i == col_i, row_i >= col_i, row_i > col_i
    hsl = lambda h: slice(h * DK, (h + 1) * DK)
    nt_dims = (((1,), (1,)), ((), ()))
    chunks = list(range(0, R, C))
    k_t_all = [_transpose_rows(k_s[:, hsl(h)]) for h in range(N_HEADS)]

    kbs, egcs, decays, a_mat, atts = {}, {}, {}, {}, {}
    for c0 in chunks:
        rs = slice(c0, c0 + C)
        gcol = _lane_heads(gc_all[rs], N_HEADS, HL)
        grow = jnp.sum(jnp.where(eye_p, gcol, 0.0), axis=0, keepdims=True)
        decays[c0] = jnp.exp(jnp.where(causal_p, gcol - grow, -jnp.inf))
        beta = beta_all[rs]
        egcs[c0] = jnp.exp(gc_all[rs])
        k = k_s[rs, :]
        kbs[c0] = jnp.concatenate([k[:, hsl(h)] * beta[:, h:h + 1] for h in range(N_HEADS)], axis=1)
        kq = lax.dot_general(_stack(kbs[c0], q_s[rs, :]), _head_blocks(k), nt_dims,
                             preferred_element_type=F32)
        a_mat[c0] = jnp.where(strict_p, kq[:C] * decays[c0], 0.0)
        atts[c0] = kq[C:] * decays[c0]

    eye = eye_p.astype(F32)
    t_inv = {c0: eye - a_mat[c0] for c0 in chunks}
    if C > 2:
        xs = {c0: _sdot(a_mat[c0], _head_blocks(a_mat[c0])) for c0 in chunks}
        m = 2
        while 2 * m < C:
            prod = {c0: _sdot(_stack(xs[c0], t_inv[c0]), _head_blocks(xs[c0])) for c0 in chunks}
            t_inv = {c0: t_inv[c0] + prod[c0][C:] for c0 in chunks}
            xs = {c0: prod[c0][:C] for c0 in chunks}
            m *= 2
        t_inv = {c0: t_inv[c0] + _sdot(t_inv[c0], _head_blocks(xs[c0])) for c0 in chunks}

    o_loc, q_til, n_loc, kws, e_last = {}, {}, {}, {}, {}
    for c0 in chunks:
        rs = slice(c0, c0 + C)
        beta, egc = beta_all[rs], egcs[c0]
        per_head = lambda x, cols, off: jnp.concatenate(
            [x[:, hsl(h)] * cols[:, off + h:off + h + 1] for h in range(N_HEADS)], axis=1)
        vb = per_head(v_s[rs, :], beta, 0)
        kbg = per_head(kbs[c0], egc, N_HEADS)
        uw = _sdot(t_inv[c0], jnp.concatenate([_head_blocks(vb), _head_blocks(kbg)], axis=1))
        u, w = uw[:, :DV_TOT], uw[:, DV_TOT:]
        au = _sdot(atts[c0], jnp.concatenate([_head_blocks(u), _head_blocks(w)], axis=1))
        o_loc[c0] = au[:, :DV_TOT]
        q_til[c0] = per_head(q_s[rs, :], egc, N_HEADS) - au[:, DV_TOT:]
        for h in range(N_HEADS):
            g_last = gc_all[c0 + C - 1:c0 + C, N_HEADS + h:N_HEADS + h + 1]
            k_til_t = k_t_all[h][:, c0:c0 + C] * jnp.exp(g_last - gc_all_t[N_HEADS + h:N_HEADS + h + 1, c0:c0 + C])
            kuw = _sdot(k_til_t, jnp.concatenate([u[:, hsl(h)], w[:, hsl(h)]], axis=1))
            n_loc[c0, h], kws[c0, h], e_last[c0, h] = kuw[:, :DV], kuw[:, DV:], jnp.exp(g_last)

    zero = jnp.zeros((DK, DV), F32)
    for c0 in chunks:
        rs = slice(c0, c0 + C)
        for h0 in range(0, N_HEADS, 2):
            h1 = h0 + 1
            s0, s1 = sd_ref[0, h0], sd_ref[0, h1]
            s_pair = jnp.concatenate([jnp.concatenate([s0, zero], axis=1), jnp.concatenate([zero, s1], axis=1)], axis=0)
            lhs = _stack(jnp.concatenate([kws[c0, h0], kws[c0, h1]], axis=1), q_til[c0][:, h0 * DK:(h1 + 1) * DK])
            prod = _sdot(lhs, s_pair)
            for i, (h, s_old) in enumerate(((h0, s0), (h1, s1))):
                o = prod[DK:, i * DV:(i + 1) * DV] + o_loc[c0][:, hsl(h)]
                sd_ref[0, h] = s_old * e_last[c0, h] - prod[:DK, i * DV:(i + 1) * DV] + n_loc[c0, h]
                mix_s[rs, hsl(h)] = _rms(o, onorm_ref[...]) * _silu(z_ref[0, rs, hsl(h)])


def _delta_blocks(q_s, k_s, v_s, z_ref, sd_ref, mix_s, onorm_ref, beta_all, gc_all, gc_all_t, gtail_all, *, R, C):
    row_i = lax.broadcasted_iota(jnp.int32, (R, R), 0)
    col_i = lax.broadcasted_iota(jnp.int32, (R, R), 1)
    same = (row_i // C) == (col_i // C)
    causal = same & (row_i >= col_i)
    strict = same & (row_i > col_i)
    eye = (row_i == col_i).astype(F32)
    hsl = lambda h: slice(h * DK, (h + 1) * DK)
    heads = range(N_HEADS)

    gtail_t = _transpose_rows(gtail_all)
    gc_cs, kbs, decays, qs, kq, k_ts = {}, {}, {}, {}, {}, {}
    for h in heads:
        gc_cs[h] = gc_all[:, N_HEADS + h:N_HEADS + h + 1]
        gc_r = gc_all_t[N_HEADS + h:N_HEADS + h + 1, :]
        decays[h] = jnp.exp(jnp.where(causal, gc_cs[h] - gc_r, -jnp.inf))
        kbs[h] = k_s[:, hsl(h)] * beta_all[:, h:h + 1]
        qs[h] = q_s[:, hsl(h)]
        k_ts[h] = _transpose_rows(k_s[:, hsl(h)])
        kq[h] = _sdot(_stack(kbs[h], qs[h]), k_ts[h])

    a_mat = {h: jnp.where(strict, kq[h][:R] * decays[h], 0.0) for h in heads}
    t_inv = {h: eye - a_mat[h] for h in heads}
    if C > 2:
        xs = {h: _sdot(a_mat[h], a_mat[h]) for h in heads}
        m = 2
        while 2 * m < C:
            prod = {h: _sdot(_stack(xs[h], t_inv[h]), xs[h]) for h in heads}
            t_inv = {h: t_inv[h] + prod[h][R:] for h in heads}
            xs = {h: prod[h][:R] for h in heads}
            m *= 2
        t_inv = {h: t_inv[h] + _sdot(t_inv[h], xs[h]) for h in heads}

    for h in heads:
        egc = jnp.exp(gc_cs[h])
        vb = v_s[:, hsl(h)] * beta_all[:, h:h + 1]
        uw = _sdot(t_inv[h], jnp.concatenate([vb, kbs[h] * egc], axis=1))
        u, w, att, qg = uw[:, :DV], uw[:, DV:], kq[h][R:] * decays[h], qs[h] * egc
        k_til_t = k_ts[h] * jnp.exp(gtail_t[N_HEADS + h:N_HEADS + h + 1, :])
        wss, qss = [], []
        for b in range(R // C):
            bs = slice(b * C, (b + 1) * C)
            wq = _sdot(_stack(w[bs], qg[bs]), sd_ref[b, h])
            wss.append(wq[:C])
            qss.append(wq[C:])
        v_new = u - jnp.concatenate(wss, axis=0)
        o = jnp.concatenate(qss, axis=0) + _sdot(att, v_new)
        for b in range(R // C):
            bs = slice(b * C, (b + 1) * C)
            e_b = jnp.exp(gc_cs[h][(b + 1) * C - 1:(b + 1) * C, :])
            sd_ref[b, h] = sd_ref[b, h] * e_b + _sdot(k_til_t[:, bs], v_new[bs])
        mix_s[:, hsl(h)] = _rms(o, onorm_ref[...]) * _silu(z_ref[:, :, hsl(h)].reshape(R, DV))


def _mixer_kernel(h1_ref, qkv_ref, z_ref, cin_ref, ba_ref, sd0_ref, sq0_ref, sc0_ref,
                  qw_ref, alog_ref, dtb_ref, onorm_ref, cw_ref, cb_ref, cnorm_ref, wout_ref,
                  h2_ref, sd_ref, sq_ref, sc_ref,
                  extq_ref, extc_ref, shc_ref, q_s, k_s, v_s, mix_s,
                  *, G, T, C, NT, chain):
    R = G * T
    t = pl.program_id(1)

    @pl.when(t == 0)
    def _():
        extq_ref[:, QKV_PAD - (SHORT_CONV - 1):QKV_PAD, :] = sq0_ref[...]
        extc_ref[:, CONV_PAD - (CONV_WIDTH - 1):CONV_PAD, :] = sc0_ref[...]
        sd_ref[...] = sd0_ref[...]

    extq_ref[:, QKV_PAD:QKV_PAD + T, :] = qkv_ref[...]
    extc_ref[:, CONV_PAD:CONV_PAD + T, :] = cin_ref[...]

    qk_scale = DK ** -0.5
    rbq = min(2 * SUBLANES, T)
    for g in range(G):
        for r0 in range(0, T, rbq):
            base = QKV_PAD - (SHORT_CONV - 1) + r0
            read = lambda j, i, g=g, base=base: extq_ref[g, pl.ds(base + j + SUBLANES * i, SUBLANES), :]
            s = _silu(_tap_sum(read, qw_ref, SHORT_CONV, rbq))
            rows = slice(g * T + r0, g * T + r0 + rbq)
            for h in range(N_HEADS):
                qh = s[:, h * DK:(h + 1) * DK]
                kh = s[:, DK_TOT + h * DK:DK_TOT + (h + 1) * DK]
                q_s[rows, h * DK:(h + 1) * DK] = qh * (lax.rsqrt(jnp.sum(qh * qh, -1, keepdims=True) + EPS) * qk_scale)
                k_s[rows, h * DK:(h + 1) * DK] = kh * lax.rsqrt(jnp.sum(kh * kh, -1, keepdims=True) + EPS)
            v_s[rows, :] = s[:, 2 * DK_TOT:]
    sq_ref[...] = extq_ref[:, T + QKV_PAD - (SHORT_CONV - 1):T + QKV_PAD, :]

    first = CONV_PAD - (CONV_WIDTH - 1)
    sh_rows = T + CONV_PAD - SUBLANES
    for g in range(G):
        for b in range(1, SUBLANES):
            for r0 in range(0, sh_rows, 32):
                rb = min(32, sh_rows - r0)
                shc_ref[g, b - 1, r0:r0 + rb, :] = extc_ref[g, r0 + b:r0 + b + rb, :]
    rbc = min(8 * SUBLANES, T)
    for g in range(G):
        for r0 in range(0, T, rbc):
            def read(j, i, g=g, r0=r0):
                a, b = divmod(j + first, SUBLANES)
                rows = pl.ds(r0 + SUBLANES * (a + i), SUBLANES)
                return extc_ref[g, rows, :] if b == 0 else shc_ref[g, b - 1, rows, :]
            acc = _tap_sum(read, cw_ref, CONV_WIDTH, rbc)
            mix_s[g * T + r0:g * T + r0 + rbc, DV_TOT:] = _silu(_rms(acc + cb_ref[...], cnorm_ref[...]))
    sc_ref[...] = extc_ref[:, T + CONV_PAD - (CONV_WIDTH - 1):T + CONV_PAD, :]

    if NT > 1:
        extq_ref[:, 0:QKV_PAD, :] = extq_ref[:, T:T + QKV_PAD, :]
        extc_ref[:, 0:CONV_PAD, :] = extc_ref[:, T:T + CONV_PAD, :]

    ba = ba_ref[...].reshape(R, LANES)
    beta_all = jax.nn.sigmoid(ba)
    xg = ba + dtb_ref[...]
    g_all = -jnp.exp(alog_ref[...]) * (jnp.maximum(xg, 0.0) + jnp.log1p(jnp.exp(-jnp.abs(xg))))
    gc_all = _block_cumsum(g_all, C)
    gc_all_t = _transpose_rows(gc_all)
    if chain:
        _delta_chain(q_s, k_s, v_s, z_ref, sd_ref, mix_s, onorm_ref, beta_all, gc_all, gc_all_t, R=R, C=C)
    else:
        gtail_all = _block_cumsum(g_all, C, reverse=True) - g_all
        _delta_blocks(q_s, k_s, v_s, z_ref, sd_ref, mix_s, onorm_ref, beta_all, gc_all, gc_all_t, gtail_all, R=R, C=C)

    mix = _wdot(mix_s[...], wout_ref[...])
    h2_ref[...] = (h1_ref[...].reshape(R, D_MODEL) + mix).reshape(G, T, D_MODEL)


def _mixer(h1, qkv, z, cin, ba, sd0, sq0, sc0, prm, *, G, T, C, shared_init):
    B, L, _ = h1.shape
    NB, NT = B // G, L // T
    chain = G == 1
    assert chain or NT == 1
    R = G * T
    seq = lambda w: pl.BlockSpec((G, T, w), lambda b, t: (b, t, 0))
    if shared_init:
        st = lambda shape: pl.BlockSpec((None, G) + shape, lambda b, t: (0,) * (2 + len(shape)))
    else:
        st = lambda shape: pl.BlockSpec((None, G) + shape, lambda b, t: (0, b) + (0,) * len(shape))
    out_st = lambda shape: pl.BlockSpec((None, G) + shape, lambda b, t: (0, b) + (0,) * len(shape))
    cst = lambda a: pl.BlockSpec(a.shape, lambda b, t: (0,) * a.ndim)
    sd_shape, sq_shape, sc_shape = (N_HEADS, DK, DV), (SHORT_CONV - 1, QKV_DIM), (CONV_WIDTH - 1, C_CONV)
    kern = functools.partial(_mixer_kernel, G=G, T=T, C=C, NT=NT, chain=chain)
    return pl.pallas_call(
        kern,
        grid=(NB, NT),
        in_specs=[seq(D_MODEL), seq(QKV_DIM), seq(DV_TOT), seq(C_CONV), seq(LANES),
                  st(sd_shape), st(sq_shape), st(sc_shape)] + [cst(a) for a in prm],
        out_specs=[seq(D_MODEL), out_st(sd_shape), out_st(sq_shape), out_st(sc_shape)],
        out_shape=[jax.ShapeDtypeStruct((B, L, D_MODEL), F32), jax.ShapeDtypeStruct((1, B) + sd_shape, F32),
                   jax.ShapeDtypeStruct((1, B) + sq_shape, F32), jax.ShapeDtypeStruct((1, B) + sc_shape, F32)],
        scratch_shapes=[pltpu.VMEM((G, QKV_PAD + T, QKV_DIM), F32), pltpu.VMEM((G, CONV_PAD + T, C_CONV), F32),
                        pltpu.VMEM((G, SUBLANES - 1, T + CONV_PAD - SUBLANES, C_CONV), F32),
                        pltpu.VMEM((R, DK_TOT), F32), pltpu.VMEM((R, DK_TOT), F32), pltpu.VMEM((R, DV_TOT), F32),
                        pltpu.VMEM((R, DV_TOT + C_CONV), F32)],
        compiler_params=pltpu.CompilerParams(dimension_semantics=("arbitrary", "arbitrary"),
                                             vmem_limit_bytes=VMEM_LIMIT),
        name="mixer",
    )(h1, qkv, z, cin, ba, sd0, sq0, sc0, *prm)


def kernel(x_prompt, x_sample, state_delta, state_qkv_conv, state_conv, meta_tokens, ffn1_norm, ffn1_w1, ffn1_w3,
           ffn1_w2, mix_norm, w_in, qkv_conv_w, a_log, dt_bias, o_norm, conv_w, conv_b, conv_norm, w_out, ffn2_norm,
           ffn2_w1, ffn2_w3, ffn2_w2, final_norm):
    bp, seq_len, _ = x_prompt.shape
    bs, dec_len, _ = x_sample.shape
    assert state_delta.shape[0] == 1
    row = lambda v: v.reshape(1, -1).astype(F32)
    mat = lambda w: w.reshape(w.shape[1:]).astype(BF16)

    w_in0 = w_in.reshape(w_in.shape[1:])
    n_gate = 2 * N_HEADS
    w_qkvz = w_in0[:, :QKV_DIM + DV_TOT].astype(BF16)
    w_glu = w_in0[:, QKV_DIM + DV_TOT + n_gate:].astype(BF16)
    w_ba = jnp.pad(w_in0[:, QKV_DIM + DV_TOT:QKV_DIM + DV_TOT + n_gate], ((0, 0), (0, LANES - n_gate))).astype(BF16)
    f1 = (row(ffn1_norm), mat(ffn1_w1), mat(ffn1_w3), mat(ffn1_w2), row(mix_norm), w_qkvz, w_glu, w_ba)
    f2 = (row(ffn2_norm), mat(ffn2_w1), mat(ffn2_w3), mat(ffn2_w2))
    lane_row = lambda v: jnp.zeros((1, LANES), F32).at[0, N_HEADS:2 * N_HEADS].set(v.reshape(-1).astype(F32))
    taps8 = lambda w: jnp.broadcast_to(w.astype(F32).reshape(w.shape[1], 1, w.shape[2]),
                                       (w.shape[1], SUBLANES, w.shape[2]))
    prm = (taps8(qkv_conv_w), lane_row(a_log), lane_row(dt_bias), row(o_norm),
           taps8(conv_w), row(conv_b), row(conv_norm), mat(w_out))
    gf = row(final_norm)

    front = lambda x, rows: _front(x, f1, rows)

    m = [a.reshape(1, N_META, -1) for a in front(meta_tokens.astype(F32), N_META)]
    zeros = lambda *s: jnp.zeros(s, F32)
    _, sd_m, sq_m, sc_m = _mixer(*m, zeros(1, 1, N_HEADS, DK, DV), zeros(1, 1, SHORT_CONV - 1, QKV_DIM),
                                 zeros(1, 1, CONV_WIDTH - 1, C_CONV), prm, G=1, T=N_META, C=N_META, shared_init=True)

    p = [a.reshape(bp, seq_len, -1) for a in front(x_prompt.reshape(bp * seq_len, D_MODEL), FRONT_ROWS)]
    h2p, sd_p, sq_p, sc_p = _mixer(*p, sd_m, sq_m, sc_m, prm, G=1, T=MIX_ROWS, C=CHUNK, shared_init=True)
    y_prompt = _back(h2p.reshape(bp * seq_len, D_MODEL), *f2, gf, BACK_ROWS).reshape(bp, seq_len, D_MODEL)

    s = [a.reshape(bs, dec_len, -1) for a in front(x_sample.reshape(bs * dec_len, D_MODEL), FRONT_ROWS)]
    h2s, sd_s, sq_s, sc_s = _mixer(*s, state_delta, state_qkv_conv, state_conv, prm,
                                   G=SAMPLE_GROUP, T=dec_len, C=dec_len, shared_init=False)
    y_sample = _back(h2s.reshape(bs * dec_len, D_MODEL), *f2, gf, BACK_ROWS).reshape(bs, dec_len, D_MODEL)

    return (y_prompt, y_sample, sd_p, sq_p, sc_p, sd_s, sq_s, sc_s)
```

```python
import functools

import jax
import jax.numpy as jnp
from jax import lax
from jax.experimental import pallas as pl
from jax.experimental.pallas import tpu as pltpu

D_MODEL = 1024
D_FF = 2816
N_HEADS = 4
DK = 128
DV = 128
DK_TOT = N_HEADS * DK
DV_TOT = N_HEADS * DV
QKV_DIM = 2 * DK_TOT + DV_TOT
C_CONV = 512
SHORT_CONV = 4
CONV_WIDTH = 31
CHUNK = 64
N_META = 16
EPS = 1e-6

LANES = 128
SUBLANES = 8
QKV_PAD = SUBLANES
CONV_PAD = 4 * SUBLANES
FRONT_ROWS = 256
BACK_ROWS = 512
MIX_ROWS = 256
SAMPLE_GROUP = 16
VMEM_LIMIT = 56 * 1024 * 1024

F32 = jnp.float32
BF16 = jnp.bfloat16


def _rms(x, g):
    return x * lax.rsqrt(jnp.mean(x * x, axis=-1, keepdims=True) + EPS) * g


def _silu(x):
    return x * jax.nn.sigmoid(x)


def _wdot(x, w):
    return jnp.dot(x.astype(BF16), w, preferred_element_type=F32)


def _sdot(a, b):
    return jnp.dot(a, b, preferred_element_type=F32)


def _stack(*xs):
    return jnp.concatenate(xs, axis=0)


def _ffn_half(h, g_ref, w1_ref, w3_ref, w2_ref):
    u = _rms(h, g_ref[...]).astype(BF16)
    a = jnp.dot(u, w1_ref[...], preferred_element_type=F32)
    b = jnp.dot(u, w3_ref[...], preferred_element_type=F32)
    hid = (_silu(a) * b).astype(BF16)
    return h + 0.5 * jnp.dot(hid, w2_ref[...], preferred_element_type=F32)


def _front_kernel(x_ref, g1_ref, w1_ref, w3_ref, w2_ref, gm_ref, wqkvz_ref, wglu_ref, wba_ref,
                  h1_ref, qkv_ref, z_ref, cin_ref, ba_ref):
    h1 = _ffn_half(x_ref[...], g1_ref, w1_ref, w3_ref, w2_ref)
    h1_ref[...] = h1
    u = _rms(h1, gm_ref[...]).astype(BF16)
    p = jnp.dot(u, wqkvz_ref[...], preferred_element_type=F32)
    qkv_ref[...] = p[:, :QKV_DIM]
    z_ref[...] = p[:, QKV_DIM:]
    glu = jnp.dot(u, wglu_ref[...], preferred_element_type=F32)
    cin_ref[...] = glu[:, :C_CONV] * jax.nn.sigmoid(glu[:, C_CONV:])
    ba_ref[...] = jnp.dot(u, wba_ref[...], preferred_element_type=F32)


def _back_kernel(h_ref, g2_ref, w1_ref, w3_ref, w2_ref, gf_ref, y_ref):
    y_ref[...] = _rms(_ffn_half(h_ref[...], g2_ref, w1_ref, w3_ref, w2_ref), gf_ref[...])


def _const_spec(shape):
    nd = len(shape)
    return pl.BlockSpec(shape, lambda *_: (0,) * nd, pipeline_mode=pl.Buffered(1))


def _row_spec(rows, width):
    return pl.BlockSpec((rows, width), lambda i: (i, 0))


def _front(x, prm, rows):
    n = x.shape[0]
    widths = (D_MODEL, QKV_DIM, DV_TOT, C_CONV, LANES)
    return pl.pallas_call(
        _front_kernel,
        grid=(n // rows,),
        in_specs=[_row_spec(rows, D_MODEL)] + [_const_spec(a.shape) for a in prm],
        out_specs=[_row_spec(rows, w) for w in widths],
        out_shape=[jax.ShapeDtypeStruct((n, w), F32) for w in widths],
        compiler_params=pltpu.CompilerParams(dimension_semantics=("arbitrary",), vmem_limit_bytes=VMEM_LIMIT),
        name="front",
    )(x, *prm)


def _back(h, g2, w1, w3, w2, gf, rows):
    n = h.shape[0]
    return pl.pallas_call(
        _back_kernel,
        grid=(n // rows,),
        in_specs=[_row_spec(rows, D_MODEL), _const_spec(g2.shape), _const_spec(w1.shape), _const_spec(w3.shape),
                  _const_spec(w2.shape), _const_spec(gf.shape)],
        out_specs=_row_spec(rows, D_MODEL),
        out_shape=jax.ShapeDtypeStruct((n, D_MODEL), F32),
        compiler_params=pltpu.CompilerParams(dimension_semantics=("arbitrary",), vmem_limit_bytes=VMEM_LIMIT),
        name="back",
    )(h, g2, w1, w3, w2, gf)


def _split_w_in_kernel(w_ref, qkvz_ref, glu_ref, ba_ref):
    n_gate = 2 * N_HEADS
    w = w_ref[...]
    qkvz_ref[...] = w[:, :QKV_DIM + DV_TOT].astype(BF16)
    glu_ref[...] = w[:, QKV_DIM + DV_TOT + n_gate:].astype(BF16)
    ba = w[:, QKV_DIM + DV_TOT:QKV_DIM + DV_TOT + LANES]
    lane = lax.broadcasted_iota(jnp.int32, ba.shape, 1)
    ba_ref[...] = jnp.where(lane < n_gate, ba, 0.0).astype(BF16)


def _split_w_in(w, rows):
    d, n = w.shape
    widths = (QKV_DIM + DV_TOT, 2 * C_CONV, LANES)
    return pl.pallas_call(
        _split_w_in_kernel,
        grid=(d // rows,),
        in_specs=[pl.BlockSpec((rows, n), lambda i: (i, 0))],
        out_specs=[pl.BlockSpec((rows, wd), lambda i: (i, 0)) for wd in widths],
        out_shape=[jax.ShapeDtypeStruct((d, wd), BF16) for wd in widths],
        name="split_w_in",
    )(w)


def _block_cumsum(x, block, reverse=False):
    rows = x.shape[0]
    r = lax.broadcasted_iota(jnp.int32, x.shape, 0) % block
    s = 1
    while s < block:
        if reverse:
            x = x + jnp.where(r < block - s, pltpu.roll(x, rows - s, axis=0), 0.0)
        else:
            x = x + jnp.where(r >= s, pltpu.roll(x, s, axis=0), 0.0)
        s *= 2
    return x


def _tap_sum(read, w_ref, n_taps, rows):
    accs = [None] * (rows // SUBLANES)
    for j in range(n_taps):
        w8 = w_ref[j]
        for i in range(len(accs)):
            tap = read(j, i) * w8
            accs[i] = tap if accs[i] is None else accs[i] + tap
    return accs[0] if len(accs) == 1 else jnp.concatenate(accs, axis=0)


def _transpose_rows(x):
    rows = x.shape[0]
    pad = (-rows) % LANES
    if pad:
        x = jnp.concatenate([x, jnp.zeros((pad, x.shape[1]), x.dtype)], axis=0)
    return x.T[:, :rows]


def _gates(ba, alog_ref, dtb_ref, block):
    beta_all = jax.nn.sigmoid(ba)
    xg = ba + dtb_ref[...]
    g_all = -jnp.exp(alog_ref[...]) * (jnp.maximum(xg, 0.0) + jnp.log1p(jnp.exp(-jnp.abs(xg))))
    return beta_all, g_all, _block_cumsum(g_all, block)


def _lane_heads(cols, offset, n_lanes):
    rows = cols.shape[0]
    lane_head = lax.broadcasted_iota(jnp.int32, (rows, n_lanes), 1) // (n_lanes // N_HEADS)
    out = jnp.broadcast_to(cols[:, offset + N_HEADS - 1:offset + N_HEADS], (rows, n_lanes))
    for h in range(N_HEADS - 2, -1, -1):
        out = jnp.where(lane_head == h, cols[:, offset + h:offset + h + 1], out)
    return out


def _head_blocks(x):
    rows, n = x.shape
    lane_head = lax.broadcasted_iota(jnp.int32, (rows, n), 1) // (n // N_HEADS)
    return jnp.concatenate([jnp.where(lane_head == h, x, 0.0) for h in range(N_HEADS)], axis=0)


def _delta_chain(q_s, k_s, v_s, z_ref, sd_ref, mix_s, onorm_ref, beta_all, gc_all, gc_all_t, *, R, C):
    HL = N_HEADS * C
    row_i = lax.broadcasted_iota(jnp.int32, (C, HL), 0)
    col_i = lax.broadcasted_iota(jnp.int32, (C, HL), 1) % C
    eye_p, causal_p, strict_p = row_i == col_i, row_i >= col_i, row_i > col_i
    hsl = lambda h: slice(h * DK, (h + 1) * DK)
    nt_dims = (((1,), (1,)), ((), ()))
    chunks = list(range(0, R, C))
    k_t_all = [_transpose_rows(k_s[:, hsl(h)]) for h in range(N_HEADS)]

    kbs, egcs, decays, a_mat, atts = {}, {}, {}, {}, {}
    for c0 in chunks:
        rs = slice(c0, c0 + C)
        gcol = _lane_heads(gc_all[rs], N_HEADS, HL)
        grow = jnp.sum(jnp.where(eye_p, gcol, 0.0), axis=0, keepdims=True)
        decays[c0] = jnp.exp(jnp.where(causal_p, gcol - grow, -jnp.inf))
        beta = beta_all[rs]
        egcs[c0] = jnp.exp(gc_all[rs])
        k = k_s[rs, :]
        kbs[c0] = jnp.concatenate([k[:, hsl(h)] * beta[:, h:h + 1] for h in range(N_HEADS)], axis=1)
        kq = lax.dot_general(_stack(kbs[c0], q_s[rs, :]), _head_blocks(k), nt_dims,
                             preferred_element_type=F32)
        a_mat[c0] = jnp.where(strict_p, kq[:C] * decays[c0], 0.0)
        atts[c0] = kq[C:] * decays[c0]

    eye = eye_p.astype(F32)
    t_inv = {c0: eye - a_mat[c0] for c0 in chunks}
    if C > 2:
        xs = {c0: _sdot(a_mat[c0], _head_blocks(a_mat[c0])) for c0 in chunks}
        m = 2
        while 2 * m < C:
            prod = {c0: _sdot(_stack(xs[c0], t_inv[c0]), _head_blocks(xs[c0])) for c0 in chunks}
            t_inv = {c0: t_inv[c0] + prod[c0][C:] for c0 in chunks}
            xs = {c0: prod[c0][:C] for c0 in chunks}
            m *= 2
        t_inv = {c0: t_inv[c0] + _sdot(t_inv[c0], _head_blocks(xs[c0])) for c0 in chunks}

    o_loc, q_til, n_loc, kws, e_last = {}, {}, {}, {}, {}
    for c0 in chunks:
        rs = slice(c0, c0 + C)
        beta, egc = beta_all[rs], egcs[c0]
        per_head = lambda x, cols, off: jnp.concatenate(
            [x[:, hsl(h)] * cols[:, off + h:off + h + 1] for h in range(N_HEADS)], axis=1)
        vb = per_head(v_s[rs, :], beta, 0)
        kbg = per_head(kbs[c0], egc, N_HEADS)
        uw = _sdot(t_inv[c0], jnp.concatenate([_head_blocks(vb), _head_blocks(kbg)], axis=1))
        u, w = uw[:, :DV_TOT], uw[:, DV_TOT:]
        au = _sdot(atts[c0], jnp.concatenate([_head_blocks(u), _head_blocks(w)], axis=1))
        o_loc[c0] = au[:, :DV_TOT]
        q_til[c0] = per_head(q_s[rs, :], egc, N_HEADS) - au[:, DV_TOT:]
        for h in range(N_HEADS):
            g_last = gc_all[c0 + C - 1:c0 + C, N_HEADS + h:N_HEADS + h + 1]
            k_til_t = k_t_all[h][:, c0:c0 + C] * jnp.exp(g_last - gc_all_t[N_HEADS + h:N_HEADS + h + 1, c0:c0 + C])
            kuw = _sdot(k_til_t, jnp.concatenate([u[:, hsl(h)], w[:, hsl(h)]], axis=1))
            n_loc[c0, h], kws[c0, h], e_last[c0, h] = kuw[:, :DV], kuw[:, DV:], jnp.exp(g_last)

    zero = jnp.zeros((DK, DV), F32)
    for c0 in chunks:
        rs = slice(c0, c0 + C)
        for h0 in range(0, N_HEADS, 2):
            h1 = h0 + 1
            s0, s1 = sd_ref[0, h0], sd_ref[0, h1]
            s_pair = jnp.concatenate([jnp.concatenate([s0, zero], axis=1), jnp.concatenate([zero, s1], axis=1)], axis=0)
            lhs = _stack(jnp.concatenate([kws[c0, h0], kws[c0, h1]], axis=1), q_til[c0][:, h0 * DK:(h1 + 1) * DK])
            prod = _sdot(lhs, s_pair)
            for i, (h, s_old) in enumerate(((h0, s0), (h1, s1))):
                o = prod[DK:, i * DV:(i + 1) * DV] + o_loc[c0][:, hsl(h)]
                sd_ref[0, h] = s_old * e_last[c0, h] - prod[:DK, i * DV:(i + 1) * DV] + n_loc[c0, h]
                mix_s[rs, hsl(h)] = _rms(o, onorm_ref[...]) * _silu(z_ref[0, rs, hsl(h)])


def _delta_blocks(q_s, k_s, v_s, z_ref, sd_ref, mix_s, onorm_ref, beta_all, gc_all, gc_all_t, gtail_all, *, R, C):
    rows2d = lambda ref, cols: ref[:, :, cols].reshape(R, cols.stop - cols.start)
    row_i = lax.broadcasted_iota(jnp.int32, (R, R), 0)
    col_i = lax.broadcasted_iota(jnp.int32, (R, R), 1)
    same = (row_i // C) == (col_i // C)
    causal = same & (row_i >= col_i)
    strict = same & (row_i > col_i)
    eye = (row_i == col_i).astype(F32)
    hsl = lambda h: slice(h * DK, (h + 1) * DK)
    heads = range(N_HEADS)

    gtail_t = _transpose_rows(gtail_all)
    gc_cs, kbs, decays, qs, kq, k_ts = {}, {}, {}, {}, {}, {}
    for h in heads:
        gc_cs[h] = gc_all[:, N_HEADS + h:N_HEADS + h + 1]
        gc_r = gc_all_t[N_HEADS + h:N_HEADS + h + 1, :]
        decays[h] = jnp.exp(jnp.where(causal, gc_cs[h] - gc_r, -jnp.inf))
        k = rows2d(k_s, hsl(h))
        kbs[h] = k * beta_all[:, h:h + 1]
        qs[h] = rows2d(q_s, hsl(h))
        k_ts[h] = _transpose_rows(k)
        kq[h] = _sdot(_stack(kbs[h], qs[h]), k_ts[h])

    a_mat = {h: jnp.where(strict, kq[h][:R] * decays[h], 0.0) for h in heads}
    t_inv = {h: eye - a_mat[h] for h in heads}
    if C > 2:
        xs = {h: _sdot(a_mat[h], a_mat[h]) for h in heads}
        m = 2
        while 2 * m < C:
            prod = {h: _sdot(_stack(xs[h], t_inv[h]), xs[h]) for h in heads}
            t_inv = {h: t_inv[h] + prod[h][R:] for h in heads}
            xs = {h: prod[h][:R] for h in heads}
            m *= 2
        t_inv = {h: t_inv[h] + _sdot(t_inv[h], xs[h]) for h in heads}

    for h in heads:
        egc = jnp.exp(gc_cs[h])
        vb = rows2d(v_s, hsl(h)) * beta_all[:, h:h + 1]
        uw = _sdot(t_inv[h], jnp.concatenate([vb, kbs[h] * egc], axis=1))
        u, w, att, qg = uw[:, :DV], uw[:, DV:], kq[h][R:] * decays[h], qs[h] * egc
        k_til_t = k_ts[h] * jnp.exp(gtail_t[N_HEADS + h:N_HEADS + h + 1, :])
        wss, qss = [], []
        for b in range(R // C):
            bs = slice(b * C, (b + 1) * C)
            wq = _sdot(_stack(w[bs], qg[bs]), sd_ref[b, h])
            wss.append(wq[:C])
            qss.append(wq[C:])
        v_new = u - jnp.concatenate(wss, axis=0)
        o = jnp.concatenate(qss, axis=0) + _sdot(att, v_new)
        for b in range(R // C):
            bs = slice(b * C, (b + 1) * C)
            e_b = jnp.exp(gc_cs[h][(b + 1) * C - 1:(b + 1) * C, :])
            sd_ref[b, h] = sd_ref[b, h] * e_b + _sdot(k_til_t[:, bs], v_new[bs])
        mix_s[:, :, hsl(h)] = (_rms(o, onorm_ref[...]) * _silu(rows2d(z_ref, hsl(h)))).reshape(R // C, C, DV)


def _mixer_kernel(h1_ref, qkv_ref, z_ref, cin_ref, ba_ref, sd0_ref, sq0_ref, sc0_ref,
                  qw_ref, alog_ref, dtb_ref, onorm_ref, cw_ref, cb_ref, cnorm_ref, wout_ref,
                  h2_ref, sd_ref, sq_ref, sc_ref,
                  extq_ref, extc_ref, shc_ref, q_s, k_s, v_s, mix_s,
                  *, G, T, C, NT):
    R = G * T
    t = pl.program_id(1)

    @pl.when(t == 0)
    def _():
        extq_ref[:, QKV_PAD - (SHORT_CONV - 1):QKV_PAD, :] = sq0_ref[...]
        extc_ref[:, CONV_PAD - (CONV_WIDTH - 1):CONV_PAD, :] = sc0_ref[...]
        sd_ref[...] = sd0_ref[...]

    extq_ref[:, QKV_PAD:QKV_PAD + T, :] = qkv_ref[...]
    extc_ref[:, CONV_PAD:CONV_PAD + T, :] = cin_ref[...]

    qk_scale = DK ** -0.5
    rbq = min(2 * SUBLANES, T)
    for g in range(G):
        for r0 in range(0, T, rbq):
            base = QKV_PAD - (SHORT_CONV - 1) + r0
            read = lambda j, i, g=g, base=base: extq_ref[g, pl.ds(base + j + SUBLANES * i, SUBLANES), :]
            s = _silu(_tap_sum(read, qw_ref, SHORT_CONV, rbq))
            rows = slice(g * T + r0, g * T + r0 + rbq)
            for h in range(N_HEADS):
                qh = s[:, h * DK:(h + 1) * DK]
                kh = s[:, DK_TOT + h * DK:DK_TOT + (h + 1) * DK]
                q_s[rows, h * DK:(h + 1) * DK] = qh * (lax.rsqrt(jnp.sum(qh * qh, -1, keepdims=True) + EPS) * qk_scale)
                k_s[rows, h * DK:(h + 1) * DK] = kh * lax.rsqrt(jnp.sum(kh * kh, -1, keepdims=True) + EPS)
            v_s[rows, :] = s[:, 2 * DK_TOT:]
    sq_ref[...] = extq_ref[:, T + QKV_PAD - (SHORT_CONV - 1):T + QKV_PAD, :]

    first = CONV_PAD - (CONV_WIDTH - 1)
    sh_rows = T + CONV_PAD - SUBLANES
    for g in range(G):
        for b in range(1, SUBLANES):
            for r0 in range(0, sh_rows, 32):
                rb = min(32, sh_rows - r0)
                shc_ref[g, b - 1, r0:r0 + rb, :] = extc_ref[g, r0 + b:r0 + b + rb, :]
    rbc = min(8 * SUBLANES, T)
    for g in range(G):
        for r0 in range(0, T, rbc):
            def read(j, i, g=g, r0=r0):
                a, b = divmod(j + first, SUBLANES)
                rows = pl.ds(r0 + SUBLANES * (a + i), SUBLANES)
                return extc_ref[g, rows, :] if b == 0 else shc_ref[g, b - 1, rows, :]
            acc = _tap_sum(read, cw_ref, CONV_WIDTH, rbc)
            mix_s[g * T + r0:g * T + r0 + rbc, DV_TOT:] = _silu(_rms(acc + cb_ref[...], cnorm_ref[...]))
    sc_ref[...] = extc_ref[:, T + CONV_PAD - (CONV_WIDTH - 1):T + CONV_PAD, :]

    if NT > 1:
        extq_ref[:, 0:QKV_PAD, :] = extq_ref[:, T:T + QKV_PAD, :]
        extc_ref[:, 0:CONV_PAD, :] = extc_ref[:, T:T + CONV_PAD, :]

    beta_all, g_all, gc_all = _gates(ba_ref[...].reshape(R, LANES), alog_ref, dtb_ref, C)
    _delta_chain(q_s, k_s, v_s, z_ref, sd_ref, mix_s, onorm_ref, beta_all, gc_all, _transpose_rows(gc_all), R=R, C=C)

    mix = _wdot(mix_s[...], wout_ref[...])
    h2_ref[...] = (h1_ref[...].reshape(R, D_MODEL) + mix).reshape(G, T, D_MODEL)


def _sample_mixer_kernel(h1_ref, qkv_ref, z_ref, cin_ref, ba_ref, sd0_ref, sq0_ref, sc0_ref,
                         qw_ref, alog_ref, dtb_ref, onorm_ref, cw_ref, cb_ref, cnorm_ref, wout_ref,
                         h2_ref, sd_ref, sq_ref, sc_ref, q_s, k_s, v_s, mix_s, *, G, T):
    R = G * T
    n_q, n_c = SHORT_CONV - 1, CONV_WIDTH - 1
    sd_ref[...] = sd0_ref[...]
    hsl = lambda h: slice(h * DK, (h + 1) * DK)
    qk_scale = DK ** -0.5
    for gi in range(G // SUBLANES):
        gs = slice(gi * SUBLANES, (gi + 1) * SUBLANES)
        qkv_row = lambda r: sq0_ref[r, gs, :] if r < n_q else qkv_ref[gs, r - n_q, :]
        cin_row = lambda r: sc0_ref[r, gs, :] if r < n_c else cin_ref[gs, r - n_c, :]

        for t in range(T):
            acc = None
            for j in range(SHORT_CONV):
                tap = qkv_row(t + j) * qw_ref[j]
                acc = tap if acc is None else acc + tap
            s = _silu(acc)
            for h in range(N_HEADS):
                qh = s[:, hsl(h)]
                kh = s[:, DK_TOT + h * DK:DK_TOT + (h + 1) * DK]
                q_s[gs, t, hsl(h)] = qh * (lax.rsqrt(jnp.sum(qh * qh, -1, keepdims=True) + EPS) * qk_scale)
                k_s[gs, t, hsl(h)] = kh * lax.rsqrt(jnp.sum(kh * kh, -1, keepdims=True) + EPS)
            v_s[gs, t, :] = s[:, 2 * DK_TOT:]

        accs = [None] * T
        for r in range(n_c + T):
            row = cin_row(r)
            for t in range(max(0, r - n_c), min(T, r + 1)):
                tap = row * cw_ref[r - t]
                accs[t] = tap if accs[t] is None else accs[t] + tap
        for t in range(T):
            mix_s[gs, t, DV_TOT:] = _silu(_rms(accs[t] + cb_ref[...], cnorm_ref[...]))

    for r in range(n_q):
        sq_ref[r] = sq0_ref[r + T] if r + T < n_q else qkv_ref[:, r + T - n_q, :]
    for r in range(n_c):
        sc_ref[r] = sc0_ref[r + T] if r + T < n_c else cin_ref[:, r + T - n_c, :]

    beta_all, g_all, gc_all = _gates(ba_ref[...].reshape(R, LANES), alog_ref, dtb_ref, T)
    gtail_all = _block_cumsum(g_all, T, reverse=True) - g_all
    _delta_blocks(q_s, k_s, v_s, z_ref, sd_ref, mix_s, onorm_ref, beta_all, gc_all, _transpose_rows(gc_all), gtail_all,
                  R=R, C=T)

    mix = _wdot(mix_s[...].reshape(R, DV_TOT + C_CONV), wout_ref[...])
    h2_ref[...] = (h1_ref[...].reshape(R, D_MODEL) + mix).reshape(G, T, D_MODEL)


def _sample_mixer(h1, qkv, z, cin, ba, sd0, sq0, sc0, prm, *, G):
    B, T, _ = h1.shape
    R = G * T
    seq = lambda w: pl.BlockSpec((G, T, w), lambda b: (b, 0, 0))
    tm = lambda a: pl.BlockSpec((a.shape[0], G, a.shape[2]), lambda b: (0, b, 0))
    sd_spec = pl.BlockSpec((None, G) + sd0.shape[2:], lambda b: (0, b, 0, 0, 0))
    cst = lambda a: pl.BlockSpec(a.shape, lambda b: (0,) * a.ndim)
    return pl.pallas_call(
        functools.partial(_sample_mixer_kernel, G=G, T=T),
        grid=(B // G,),
        in_specs=[seq(D_MODEL), seq(QKV_DIM), seq(DV_TOT), seq(C_CONV), seq(LANES), sd_spec, tm(sq0), tm(sc0)]
                 + [cst(a) for a in prm],
        out_specs=[seq(D_MODEL), sd_spec, tm(sq0), tm(sc0)],
        out_shape=[jax.ShapeDtypeStruct((B, T, D_MODEL), F32), jax.ShapeDtypeStruct(sd0.shape, F32),
                   jax.ShapeDtypeStruct(sq0.shape, F32), jax.ShapeDtypeStruct(sc0.shape, F32)],
        scratch_shapes=[pltpu.VMEM((G, T, DK_TOT), F32), pltpu.VMEM((G, T, DK_TOT), F32),
                        pltpu.VMEM((G, T, DV_TOT), F32), pltpu.VMEM((G, T, DV_TOT + C_CONV), F32)],
        compiler_params=pltpu.CompilerParams(dimension_semantics=("arbitrary",), vmem_limit_bytes=VMEM_LIMIT),
        name="sample_mixer",
    )(h1, qkv, z, cin, ba, sd0, sq0, sc0, *prm)


def _mixer(h1, qkv, z, cin, ba, sd0, sq0, sc0, prm, *, T, C):
    B, L, _ = h1.shape
    G, NB, NT = 1, B, L // T
    R = G * T
    seq = lambda w: pl.BlockSpec((G, T, w), lambda b, t: (b, t, 0))
    st = lambda shape: pl.BlockSpec((None, G) + shape, lambda b, t: (0,) * (2 + len(shape)))
    out_st = lambda shape: pl.BlockSpec((None, G) + shape, lambda b, t: (0, b) + (0,) * len(shape))
    cst = lambda a: pl.BlockSpec(a.shape, lambda b, t: (0,) * a.ndim)
    sd_shape, sq_shape, sc_shape = (N_HEADS, DK, DV), (SHORT_CONV - 1, QKV_DIM), (CONV_WIDTH - 1, C_CONV)
    kern = functools.partial(_mixer_kernel, G=G, T=T, C=C, NT=NT)
    return pl.pallas_call(
        kern,
        grid=(NB, NT),
        in_specs=[seq(D_MODEL), seq(QKV_DIM), seq(DV_TOT), seq(C_CONV), seq(LANES),
                  st(sd_shape), st(sq_shape), st(sc_shape)] + [cst(a) for a in prm],
        out_specs=[seq(D_MODEL), out_st(sd_shape), out_st(sq_shape), out_st(sc_shape)],
        out_shape=[jax.ShapeDtypeStruct((B, L, D_MODEL), F32), jax.ShapeDtypeStruct((1, B) + sd_shape, F32),
                   jax.ShapeDtypeStruct((1, B) + sq_shape, F32), jax.ShapeDtypeStruct((1, B) + sc_shape, F32)],
        scratch_shapes=[pltpu.VMEM((G, QKV_PAD + T, QKV_DIM), F32), pltpu.VMEM((G, CONV_PAD + T, C_CONV), F32),
                        pltpu.VMEM((G, SUBLANES - 1, T + CONV_PAD - SUBLANES, C_CONV), F32),
                        pltpu.VMEM((R, DK_TOT), F32), pltpu.VMEM((R, DK_TOT), F32), pltpu.VMEM((R, DV_TOT), F32),
                        pltpu.VMEM((R, DV_TOT + C_CONV), F32)],
        compiler_params=pltpu.CompilerParams(dimension_semantics=("arbitrary", "arbitrary"),
                                             vmem_limit_bytes=VMEM_LIMIT),
        name="mixer",
    )(h1, qkv, z, cin, ba, sd0, sq0, sc0, *prm)


def kernel(x_prompt, x_sample, state_delta, state_qkv_conv, state_conv, meta_tokens, ffn1_norm, ffn1_w1, ffn1_w3,
           ffn1_w2, mix_norm, w_in, qkv_conv_w, a_log, dt_bias, o_norm, conv_w, conv_b, conv_norm, w_out, ffn2_norm,
           ffn2_w1, ffn2_w3, ffn2_w2, final_norm):
    bp, seq_len, _ = x_prompt.shape
    bs, dec_len, _ = x_sample.shape
    assert state_delta.shape[0] == 1
    row = lambda v: v.reshape(1, -1).astype(F32)
    mat = lambda w: w.reshape(w.shape[1:]).astype(BF16)

    w_qkvz, w_glu, w_ba = _split_w_in(w_in.reshape(w_in.shape[1:]), FRONT_ROWS)
    f1 = (row(ffn1_norm), mat(ffn1_w1), mat(ffn1_w3), mat(ffn1_w2), row(mix_norm), w_qkvz, w_glu, w_ba)
    f2 = (row(ffn2_norm), mat(ffn2_w1), mat(ffn2_w3), mat(ffn2_w2))
    lane_row = lambda v: jnp.zeros((1, LANES), F32).at[0, N_HEADS:2 * N_HEADS].set(v.reshape(-1).astype(F32))
    taps8 = lambda w: jnp.broadcast_to(w.astype(F32).reshape(w.shape[1], 1, w.shape[2]),
                                       (w.shape[1], SUBLANES, w.shape[2]))
    prm = (taps8(qkv_conv_w), lane_row(a_log), lane_row(dt_bias), row(o_norm),
           taps8(conv_w), row(conv_b), row(conv_norm), mat(w_out))
    gf = row(final_norm)

    front = lambda x, rows: _front(x, f1, rows)

    m = [a.reshape(1, N_META, -1) for a in front(meta_tokens.astype(F32), N_META)]
    zeros = lambda *s: jnp.zeros(s, F32)
    _, sd_m, sq_m, sc_m = _mixer(*m, zeros(1, 1, N_HEADS, DK, DV), zeros(1, 1, SHORT_CONV - 1, QKV_DIM),
                                 zeros(1, 1, CONV_WIDTH - 1, C_CONV), prm, T=N_META, C=N_META)

    p = [a.reshape(bp, seq_len, -1) for a in front(x_prompt.reshape(bp * seq_len, D_MODEL), FRONT_ROWS)]
    h2p, sd_p, sq_p, sc_p = _mixer(*p, sd_m, sq_m, sc_m, prm, T=MIX_ROWS, C=CHUNK)
    y_prompt = _back(h2p.reshape(bp * seq_len, D_MODEL), *f2, gf, BACK_ROWS).reshape(bp, seq_len, D_MODEL)

    time_major = lambda a: jnp.transpose(a.reshape(a.shape[1:]), (1, 0, 2))
    seq_major = lambda a: jnp.transpose(a, (1, 0, 2))[None]
    s = [a.reshape(bs, dec_len, -1) for a in front(x_sample.reshape(bs * dec_len, D_MODEL), FRONT_ROWS)]
    h2s, sd_s, sq_s, sc_s = _sample_mixer(*s, state_delta, time_major(state_qkv_conv), time_major(state_conv), prm,
                                          G=SAMPLE_GROUP)
    y_sample = _back(h2s.reshape(bs * dec_len, D_MODEL), *f2, gf, BACK_ROWS).reshape(bs, dec_len, D_MODEL)

    return (y_prompt, y_sample, sd_p, sq_p, sc_p, sd_s, seq_major(sq_s), seq_major(sc_s))
```

```python
import functools

import jax
import jax.numpy as jnp
from jax import lax
from jax.experimental import pallas as pl
from jax.experimental.pallas import tpu as pltpu

D_MODEL = 1024
D_FF = 2816
N_HEADS = 4
DK = 128
DV = 128
DK_TOT = N_HEADS * DK
DV_TOT = N_HEADS * DV
QKV_DIM = 2 * DK_TOT + DV_TOT
C_CONV = 512
SHORT_CONV = 4
CONV_WIDTH = 31
CHUNK = 64
N_META = 16
EPS = 1e-6

LANES = 128
SUBLANES = 8
QKV_PAD = SUBLANES
CONV_PAD = 4 * SUBLANES
FRONT_ROWS = 256
BACK_ROWS = 512
MIX_ROWS = 512
SAMPLE_GROUP = 16
VMEM_LIMIT = 56 * 1024 * 1024

F32 = jnp.float32
BF16 = jnp.bfloat16


def _rms(x, g):
    return x * lax.rsqrt(jnp.mean(x * x, axis=-1, keepdims=True) + EPS) * g


def _silu(x):
    return x * jax.nn.sigmoid(x)


def _wdot(x, w):
    return jnp.dot(x.astype(BF16), w, preferred_element_type=F32)


def _sdot(a, b):
    return jnp.dot(a, b, preferred_element_type=F32)


def _stack(*xs):
    return jnp.concatenate(xs, axis=0)


def _ffn_half(h, g_ref, w1_ref, w3_ref, w2_ref):
    u = _rms(h, g_ref[...]).astype(BF16)
    a = jnp.dot(u, w1_ref[...], preferred_element_type=F32)
    b = jnp.dot(u, w3_ref[...], preferred_element_type=F32)
    hid = (_silu(a) * b).astype(BF16)
    return h + 0.5 * jnp.dot(hid, w2_ref[...], preferred_element_type=F32)


def _front_kernel(xa_ref, xb_ref, g1_ref, w1_ref, w3_ref, w2_ref, gm_ref, wqkvz_ref, wglu_ref, wba_ref,
                  h1_ref, qkv_ref, z_ref, cin_ref, ba_ref, *, na):
    x = jnp.where(pl.program_id(0) < na, xa_ref[...], xb_ref[...])
    h1 = _ffn_half(x, g1_ref, w1_ref, w3_ref, w2_ref)
    h1_ref[...] = h1
    u = _rms(h1, gm_ref[...]).astype(BF16)
    p = jnp.dot(u, wqkvz_ref[...], preferred_element_type=F32)
    qkv_ref[...] = p[:, :QKV_DIM]
    z_ref[...] = p[:, QKV_DIM:]
    glu = jnp.dot(u, wglu_ref[...], preferred_element_type=F32)
    cin_ref[...] = glu[:, :C_CONV] * jax.nn.sigmoid(glu[:, C_CONV:])
    ba_ref[...] = jnp.dot(u, wba_ref[...], preferred_element_type=F32)


def _back_kernel(ha_ref, hb_ref, g2_ref, w1_ref, w3_ref, w2_ref, gf_ref, ya_ref, yb_ref, *, na):
    i = pl.program_id(0)

    def run(h_ref, y_ref):
        y_ref[...] = _rms(_ffn_half(h_ref[...], g2_ref, w1_ref, w3_ref, w2_ref), gf_ref[...])

    pl.when(i < na)(functools.partial(run, ha_ref, ya_ref))
    pl.when(i >= na)(functools.partial(run, hb_ref, yb_ref))


def _const_spec(shape):
    nd = len(shape)
    return pl.BlockSpec(shape, lambda *_: (0,) * nd, pipeline_mode=pl.Buffered(1))


def _two_source_specs(rows, na):
    first = pl.BlockSpec((rows, D_MODEL), lambda i: (jnp.minimum(i, na - 1), 0))
    second = pl.BlockSpec((rows, D_MODEL), lambda i: (jnp.maximum(i - na, 0), 0))
    return first, second


def _front(xa, xb, prm, rows):
    na, nb = xa.shape[0] // rows, xb.shape[0] // rows
    n = xa.shape[0] + xb.shape[0]
    widths = (D_MODEL, QKV_DIM, DV_TOT, C_CONV, LANES)
    return pl.pallas_call(
        functools.partial(_front_kernel, na=na),
        grid=(na + nb,),
        in_specs=list(_two_source_specs(rows, na)) + [_const_spec(a.shape) for a in prm],
        out_specs=[pl.BlockSpec((rows, w), lambda i: (i, 0)) for w in widths],
        out_shape=[jax.ShapeDtypeStruct((n, w), F32) for w in widths],
        compiler_params=pltpu.CompilerParams(dimension_semantics=("arbitrary",), vmem_limit_bytes=VMEM_LIMIT),
        name="front",
    )(xa, xb, *prm)


def _back(ha, hb, prm, rows):
    na, nb = ha.shape[0] // rows, hb.shape[0] // rows
    return pl.pallas_call(
        functools.partial(_back_kernel, na=na),
        grid=(na + nb,),
        in_specs=list(_two_source_specs(rows, na)) + [_const_spec(a.shape) for a in prm],
        out_specs=list(_two_source_specs(rows, na)),
        out_shape=[jax.ShapeDtypeStruct(ha.shape, F32), jax.ShapeDtypeStruct(hb.shape, F32)],
        compiler_params=pltpu.CompilerParams(dimension_semantics=("arbitrary",), vmem_limit_bytes=VMEM_LIMIT),
        name="back",
    )(ha, hb, *prm)


def _split_w_in_kernel(w_ref, qkvz_ref, glu_ref, ba_ref):
    n_gate = 2 * N_HEADS
    w = w_ref[...]
    qkvz_ref[...] = w[:, :QKV_DIM + DV_TOT].astype(BF16)
    glu_ref[...] = w[:, QKV_DIM + DV_TOT + n_gate:].astype(BF16)
    ba = w[:, QKV_DIM + DV_TOT:QKV_DIM + DV_TOT + LANES]
    lane = lax.broadcasted_iota(jnp.int32, ba.shape, 1)
    ba_ref[...] = jnp.where(lane < n_gate, ba, 0.0).astype(BF16)


def _split_w_in(w, rows):
    d, n = w.shape
    widths = (QKV_DIM + DV_TOT, 2 * C_CONV, LANES)
    return pl.pallas_call(
        _split_w_in_kernel,
        grid=(d // rows,),
        in_specs=[pl.BlockSpec((rows, n), lambda i: (i, 0))],
        out_specs=[pl.BlockSpec((rows, wd), lambda i: (i, 0)) for wd in widths],
        out_shape=[jax.ShapeDtypeStruct((d, wd), BF16) for wd in widths],
        name="split_w_in",
    )(w)


def _block_cumsum(x, block, reverse=False):
    rows = x.shape[0]
    r = lax.broadcasted_iota(jnp.int32, x.shape, 0) % block
    s = 1
    while s < block:
        if reverse:
            x = x + jnp.where(r < block - s, pltpu.roll(x, rows - s, axis=0), 0.0)
        else:
            x = x + jnp.where(r >= s, pltpu.roll(x, s, axis=0), 0.0)
        s *= 2
    return x


def _tap_sum(read, w_ref, n_taps, rows):
    accs = [None] * (rows // SUBLANES)
    for j in range(n_taps):
        w8 = w_ref[j]
        for i in range(len(accs)):
            tap = read(j, i) * w8
            accs[i] = tap if accs[i] is None else accs[i] + tap
    return accs[0] if len(accs) == 1 else jnp.concatenate(accs, axis=0)


def _transpose_rows(x):
    rows = x.shape[0]
    pad = (-rows) % LANES
    if pad:
        x = jnp.concatenate([x, jnp.zeros((pad, x.shape[1]), x.dtype)], axis=0)
    return x.T[:, :rows]


def _gates(ba, alog_ref, dtb_ref, block):
    beta_all = jax.nn.sigmoid(ba)
    xg = ba + dtb_ref[...]
    g_all = -jnp.exp(alog_ref[...]) * (jnp.maximum(xg, 0.0) + jnp.log1p(jnp.exp(-jnp.abs(xg))))
    return beta_all, g_all, _block_cumsum(g_all, block)


def _lane_heads(cols, offset, n_lanes):
    rows = cols.shape[0]
    lane_head = lax.broadcasted_iota(jnp.int32, (rows, n_lanes), 1) // (n_lanes // N_HEADS)
    out = jnp.broadcast_to(cols[:, offset + N_HEADS - 1:offset + N_HEADS], (rows, n_lanes))
    for h in range(N_HEADS - 2, -1, -1):
        out = jnp.where(lane_head == h, cols[:, offset + h:offset + h + 1], out)
    return out


def _head_blocks(x):
    rows, n = x.shape
    lane_head = lax.broadcasted_iota(jnp.int32, (rows, n), 1) // (n // N_HEADS)
    return jnp.concatenate([jnp.where(lane_head == h, x, 0.0) for h in range(N_HEADS)], axis=0)


def _delta_chain(q_s, k_s, v_s, z_ref, sd_ref, mix_s, onorm_ref, beta_all, gc_all, gc_all_t, *, R, C):
    HL = N_HEADS * C
    row_i = lax.broadcasted_iota(jnp.int32, (C, HL), 0)
    col_i = lax.broadcasted_iota(jnp.int32, (C, HL), 1) % C
    eye_p, causal_p, strict_p = row_i == col_i, row_i >= col_i, row_i > col_i
    hsl = lambda h: slice(h * DK, (h + 1) * DK)
    nt_dims = (((1,), (1,)), ((), ()))
    chunks = list(range(0, R, C))
    k_t_all = [_transpose_rows(k_s[:, hsl(h)]) for h in range(N_HEADS)]

    kbs, egcs, decays, a_mat, atts = {}, {}, {}, {}, {}
    for c0 in chunks:
        rs = slice(c0, c0 + C)
        gcol = _lane_heads(gc_all[rs], N_HEADS, HL)
        grow = jnp.sum(jnp.where(eye_p, gcol, 0.0), axis=0, keepdims=True)
        decays[c0] = jnp.exp(jnp.where(causal_p, gcol - grow, -jnp.inf))
        beta = beta_all[rs]
        egcs[c0] = jnp.exp(gc_all[rs])
        k = k_s[rs, :]
        kbs[c0] = jnp.concatenate([k[:, hsl(h)] * beta[:, h:h + 1] for h in range(N_HEADS)], axis=1)
        kq = lax.dot_general(_stack(kbs[c0], q_s[rs, :]), _head_blocks(k), nt_dims,
                             preferred_element_type=F32)
        a_mat[c0] = jnp.where(strict_p, kq[:C] * decays[c0], 0.0)
        atts[c0] = kq[C:] * decays[c0]

    eye = eye_p.astype(F32)
    t_inv = {c0: eye - a_mat[c0] for c0 in chunks}
    if C > 2:
        xs = {c0: _sdot(a_mat[c0], _head_blocks(a_mat[c0])) for c0 in chunks}
        m = 2
        while 2 * m < C:
            prod = {c0: _sdot(_stack(xs[c0], t_inv[c0]), _head_blocks(xs[c0])) for c0 in chunks}
            t_inv = {c0: t_inv[c0] + prod[c0][C:] for c0 in chunks}
            xs = {c0: prod[c0][:C] for c0 in chunks}
            m *= 2
        t_inv = {c0: t_inv[c0] + _sdot(t_inv[c0], _head_blocks(xs[c0])) for c0 in chunks}

    o_loc, q_til, n_loc, kws, e_last = {}, {}, {}, {}, {}
    for c0 in chunks:
        rs = slice(c0, c0 + C)
        beta, egc = beta_all[rs], egcs[c0]
        per_head = lambda x, cols, off: jnp.concatenate(
            [x[:, hsl(h)] * cols[:, off + h:off + h + 1] for h in range(N_HEADS)], axis=1)
        vb = per_head(v_s[rs, :], beta, 0)
        kbg = per_head(kbs[c0], egc, N_HEADS)
        uw = _sdot(t_inv[c0], jnp.concatenate([_head_blocks(vb), _head_blocks(kbg)], axis=1))
        u, w = uw[:, :DV_TOT], uw[:, DV_TOT:]
        au = _sdot(atts[c0], jnp.concatenate([_head_blocks(u), _head_blocks(w)], axis=1))
        o_loc[c0] = au[:, :DV_TOT]
        q_til[c0] = per_head(q_s[rs, :], egc, N_HEADS) - au[:, DV_TOT:]
        for h in range(N_HEADS):
            g_last = gc_all[c0 + C - 1:c0 + C, N_HEADS + h:N_HEADS + h + 1]
            k_til_t = k_t_all[h][:, c0:c0 + C] * jnp.exp(g_last - gc_all_t[N_HEADS + h:N_HEADS + h + 1, c0:c0 + C])
            kuw = _sdot(k_til_t, jnp.concatenate([u[:, hsl(h)], w[:, hsl(h)]], axis=1))
            n_loc[c0, h], kws[c0, h], e_last[c0, h] = kuw[:, :DV], kuw[:, DV:], jnp.exp(g_last)

    zero = jnp.zeros((DK, DV), F32)
    for c0 in chunks:
        rs = slice(c0, c0 + C)
        for h0 in range(0, N_HEADS, 2):
            h1 = h0 + 1
            s0, s1 = sd_ref[0, h0], sd_ref[0, h1]
            s_pair = jnp.concatenate([jnp.concatenate([s0, zero], axis=1), jnp.concatenate([zero, s1], axis=1)], axis=0)
            lhs = _stack(jnp.concatenate([kws[c0, h0], kws[c0, h1]], axis=1), q_til[c0][:, h0 * DK:(h1 + 1) * DK])
            prod = _sdot(lhs, s_pair)
            for i, (h, s_old) in enumerate(((h0, s0), (h1, s1))):
                o = prod[DK:, i * DV:(i + 1) * DV] + o_loc[c0][:, hsl(h)]
                sd_ref[0, h] = s_old * e_last[c0, h] - prod[:DK, i * DV:(i + 1) * DV] + n_loc[c0, h]
                mix_s[rs, hsl(h)] = _rms(o, onorm_ref[...]) * _silu(z_ref[rs, hsl(h)])


def _delta_blocks(q_s, k_s, v_s, z_ref, sd_ref, mix_s, onorm_ref, beta_all, gc_all, gc_all_t, gtail_all, *, R, C):
    rows2d = lambda ref, cols: ref[:, :, cols].reshape(R, cols.stop - cols.start)
    row_i = lax.broadcasted_iota(jnp.int32, (R, R), 0)
    col_i = lax.broadcasted_iota(jnp.int32, (R, R), 1)
    same = (row_i // C) == (col_i // C)
    causal = same & (row_i >= col_i)
    strict = same & (row_i > col_i)
    eye = (row_i == col_i).astype(F32)
    hsl = lambda h: slice(h * DK, (h + 1) * DK)
    heads = range(N_HEADS)

    gtail_t = _transpose_rows(gtail_all)
    gc_cs, kbs, decays, qs, kq, k_ts = {}, {}, {}, {}, {}, {}
    for h in heads:
        gc_cs[h] = gc_all[:, N_HEADS + h:N_HEADS + h + 1]
        gc_r = gc_all_t[N_HEADS + h:N_HEADS + h + 1, :]
        decays[h] = jnp.exp(jnp.where(causal, gc_cs[h] - gc_r, -jnp.inf))
        k = rows2d(k_s, hsl(h))
        kbs[h] = k * beta_all[:, h:h + 1]
        qs[h] = rows2d(q_s, hsl(h))
        k_ts[h] = _transpose_rows(k)
        kq[h] = _sdot(_stack(kbs[h], qs[h]), k_ts[h])

    a_mat = {h: jnp.where(strict, kq[h][:R] * decays[h], 0.0) for h in heads}
    t_inv = {h: eye - a_mat[h] for h in heads}
    if C > 2:
        xs = {h: _sdot(a_mat[h], a_mat[h]) for h in heads}
        m = 2
        while 2 * m < C:
            prod = {h: _sdot(_stack(xs[h], t_inv[h]), xs[h]) for h in heads}
            t_inv = {h: t_inv[h] + prod[h][R:] for h in heads}
            xs = {h: prod[h][:R] for h in heads}
            m *= 2
        t_inv = {h: t_inv[h] + _sdot(t_inv[h], xs[h]) for h in heads}

    for h in heads:
        egc = jnp.exp(gc_cs[h])
        vb = rows2d(v_s, hsl(h)) * beta_all[:, h:h + 1]
        uw = _sdot(t_inv[h], jnp.concatenate([vb, kbs[h] * egc], axis=1))
        u, w, att, qg = uw[:, :DV], uw[:, DV:], kq[h][R:] * decays[h], qs[h] * egc
        k_til_t = k_ts[h] * jnp.exp(gtail_t[N_HEADS + h:N_HEADS + h + 1, :])
        wss, qss = [], []
        for b in range(R // C):
            bs = slice(b * C, (b + 1) * C)
            wq = _sdot(_stack(w[bs], qg[bs]), sd_ref[b, h])
            wss.append(wq[:C])
            qss.append(wq[C:])
        v_new = u - jnp.concatenate(wss, axis=0)
        o = jnp.concatenate(qss, axis=0) + _sdot(att, v_new)
        for b in range(R // C):
            bs = slice(b * C, (b + 1) * C)
            e_b = jnp.exp(gc_cs[h][(b + 1) * C - 1:(b + 1) * C, :])
            sd_ref[b, h] = sd_ref[b, h] * e_b + _sdot(k_til_t[:, bs], v_new[bs])
        mix_s[:, :, hsl(h)] = (_rms(o, onorm_ref[...]) * _silu(rows2d(z_ref, hsl(h)))).reshape(R // C, C, DV)


def _mixer_kernel(h1_ref, qkv_ref, z_ref, cin_ref, ba_ref, sd0_ref, sq0_ref, sc0_ref,
                  qw_ref, alog_ref, dtb_ref, onorm_ref, cw_ref, cb_ref, cnorm_ref, wout_ref,
                  h2_ref, sd_ref, sq_ref, sc_ref,
                  extq_ref, extc_ref, shc_ref, q_s, k_s, v_s, mix_s,
                  *, G, T, C, NT):
    R = G * T
    t = pl.program_id(1)

    @pl.when(t == 0)
    def _():
        extq_ref[:, QKV_PAD - (SHORT_CONV - 1):QKV_PAD, :] = sq0_ref[...]
        extc_ref[:, CONV_PAD - (CONV_WIDTH - 1):CONV_PAD, :] = sc0_ref[...]
        sd_ref[...] = sd0_ref[...]

    extq_ref[0, QKV_PAD:QKV_PAD + T, :] = qkv_ref[...]
    extc_ref[0, CONV_PAD:CONV_PAD + T, :] = cin_ref[...]

    qk_scale = DK ** -0.5
    rbq = min(2 * SUBLANES, T)
    for g in range(G):
        for r0 in range(0, T, rbq):
            base = QKV_PAD - (SHORT_CONV - 1) + r0
            read = lambda j, i, g=g, base=base: extq_ref[g, pl.ds(base + j + SUBLANES * i, SUBLANES), :]
            s = _silu(_tap_sum(read, qw_ref, SHORT_CONV, rbq))
            rows = slice(g * T + r0, g * T + r0 + rbq)
            for h in range(N_HEADS):
                qh = s[:, h * DK:(h + 1) * DK]
                kh = s[:, DK_TOT + h * DK:DK_TOT + (h + 1) * DK]
                q_s[rows, h * DK:(h + 1) * DK] = qh * (lax.rsqrt(jnp.sum(qh * qh, -1, keepdims=True) + EPS) * qk_scale)
                k_s[rows, h * DK:(h + 1) * DK] = kh * lax.rsqrt(jnp.sum(kh * kh, -1, keepdims=True) + EPS)
            v_s[rows, :] = s[:, 2 * DK_TOT:]
    sq_ref[...] = extq_ref[:, T + QKV_PAD - (SHORT_CONV - 1):T + QKV_PAD, :]

    first = CONV_PAD - (CONV_WIDTH - 1)
    sh_rows = T + CONV_PAD - SUBLANES
    for g in range(G):
        for b in range(1, SUBLANES):
            for r0 in range(0, sh_rows, 32):
                rb = min(32, sh_rows - r0)
                shc_ref[g, b - 1, r0:r0 + rb, :] = extc_ref[g, r0 + b:r0 + b + rb, :]
    rbc = min(8 * SUBLANES, T)
    for g in range(G):
        for r0 in range(0, T, rbc):
            def read(j, i, g=g, r0=r0):
                a, b = divmod(j + first, SUBLANES)
                rows = pl.ds(r0 + SUBLANES * (a + i), SUBLANES)
                return extc_ref[g, rows, :] if b == 0 else shc_ref[g, b - 1, rows, :]
            acc = _tap_sum(read, cw_ref, CONV_WIDTH, rbc)
            mix_s[g * T + r0:g * T + r0 + rbc, DV_TOT:] = _silu(_rms(acc + cb_ref[...], cnorm_ref[...]))
    sc_ref[...] = extc_ref[:, T + CONV_PAD - (CONV_WIDTH - 1):T + CONV_PAD, :]

    if NT > 1:
        extq_ref[:, 0:QKV_PAD, :] = extq_ref[:, T:T + QKV_PAD, :]
        extc_ref[:, 0:CONV_PAD, :] = extc_ref[:, T:T + CONV_PAD, :]

    beta_all, g_all, gc_all = _gates(ba_ref[...], alog_ref, dtb_ref, C)
    _delta_chain(q_s, k_s, v_s, z_ref, sd_ref, mix_s, onorm_ref, beta_all, gc_all, _transpose_rows(gc_all), R=R, C=C)

    h2_ref[...] = h1_ref[...] + _wdot(mix_s[...], wout_ref[...])


def _sample_mixer_kernel(h1_ref, qkv_ref, z_ref, cin_ref, ba_ref, sd0_ref, sq0_ref, sc0_ref,
                         qw_ref, alog_ref, dtb_ref, onorm_ref, cw_ref, cb_ref, cnorm_ref, wout_ref,
                         h2_ref, sd_ref, sq_ref, sc_ref, q_s, k_s, v_s, mix_s, *, G, T):
    R = G * T
    n_q, n_c = SHORT_CONV - 1, CONV_WIDTH - 1
    sd_ref[...] = sd0_ref[...]
    hsl = lambda h: slice(h * DK, (h + 1) * DK)
    qk_scale = DK ** -0.5
    for gi in range(G // SUBLANES):
        gs = slice(gi * SUBLANES, (gi + 1) * SUBLANES)
        qkv_row = lambda r: sq0_ref[r, gs, :] if r < n_q else qkv_ref[gs, r - n_q, :]
        cin_row = lambda r: sc0_ref[r, gs, :] if r < n_c else cin_ref[gs, r - n_c, :]

        for t in range(T):
            acc = None
            for j in range(SHORT_CONV):
                tap = qkv_row(t + j) * qw_ref[j]
                acc = tap if acc is None else acc + tap
            s = _silu(acc)
            for h in range(N_HEADS):
                qh = s[:, hsl(h)]
                kh = s[:, DK_TOT + h * DK:DK_TOT + (h + 1) * DK]
                q_s[gs, t, hsl(h)] = qh * (lax.rsqrt(jnp.sum(qh * qh, -1, keepdims=True) + EPS) * qk_scale)
                k_s[gs, t, hsl(h)] = kh * lax.rsqrt(jnp.sum(kh * kh, -1, keepdims=True) + EPS)
            v_s[gs, t, :] = s[:, 2 * DK_TOT:]

        accs = [None] * T
        for r in range(n_c + T):
            row = cin_row(r)
            for t in range(max(0, r - n_c), min(T, r + 1)):
                tap = row * cw_ref[r - t]
                accs[t] = tap if accs[t] is None else accs[t] + tap
        for t in range(T):
            mix_s[gs, t, DV_TOT:] = _silu(_rms(accs[t] + cb_ref[...], cnorm_ref[...]))

    for r in range(n_q):
        sq_ref[r] = sq0_ref[r + T] if r + T < n_q else qkv_ref[:, r + T - n_q, :]
    for r in range(n_c):
        sc_ref[r] = sc0_ref[r + T] if r + T < n_c else cin_ref[:, r + T - n_c, :]

    beta_all, g_all, gc_all = _gates(ba_ref[...].reshape(R, LANES), alog_ref, dtb_ref, T)
    gtail_all = _block_cumsum(g_all, T, reverse=True) - g_all
    _delta_blocks(q_s, k_s, v_s, z_ref, sd_ref, mix_s, onorm_ref, beta_all, gc_all, _transpose_rows(gc_all), gtail_all,
                  R=R, C=T)

    mix = _wdot(mix_s[...].reshape(R, DV_TOT + C_CONV), wout_ref[...])
    h2_ref[...] = h1_ref[...].reshape(R, D_MODEL) + mix


def _sample_mixer(rows, sd0, sq0, sc0, prm, *, first_row, T, G):
    B = sd0.shape[1]
    R = G * T
    rows = [a.reshape(a.shape[0] // T, T, a.shape[1]) for a in rows]
    seq = lambda a: pl.BlockSpec((G, T, a.shape[2]), lambda b: (first_row // R + b, 0, 0))
    tm = lambda a: pl.BlockSpec((a.shape[0], G, a.shape[2]), lambda b: (0, b, 0))
    sd_spec = pl.BlockSpec((None, G) + sd0.shape[2:], lambda b: (0, b, 0, 0, 0))
    cst = lambda a: pl.BlockSpec(a.shape, lambda b: (0,) * a.ndim)
    return pl.pallas_call(
        functools.partial(_sample_mixer_kernel, G=G, T=T),
        grid=(B // G,),
        in_specs=[seq(a) for a in rows] + [sd_spec, tm(sq0), tm(sc0)] + [cst(a) for a in prm],
        out_specs=[pl.BlockSpec((R, D_MODEL), lambda b: (b, 0)), sd_spec, tm(sq0), tm(sc0)],
        out_shape=[jax.ShapeDtypeStruct((B * T, D_MODEL), F32), jax.ShapeDtypeStruct(sd0.shape, F32),
                   jax.ShapeDtypeStruct(sq0.shape, F32), jax.ShapeDtypeStruct(sc0.shape, F32)],
        scratch_shapes=[pltpu.VMEM((G, T, DK_TOT), F32), pltpu.VMEM((G, T, DK_TOT), F32),
                        pltpu.VMEM((G, T, DV_TOT), F32), pltpu.VMEM((G, T, DV_TOT + C_CONV), F32)],
        compiler_params=pltpu.CompilerParams(dimension_semantics=("arbitrary",), vmem_limit_bytes=VMEM_LIMIT),
        name="sample_mixer",
    )(*rows, sd0, sq0, sc0, *prm)


def _mixer(rows, sd0, sq0, sc0, prm, *, first_row, n_seq, seq_len, T, C):
    B, L = n_seq, seq_len
    G, NB, NT = 1, B, L // T
    R = G * T
    seq = lambda a: pl.BlockSpec((T, a.shape[1]), lambda b, t: (first_row // T + b * NT + t, 0))
    st = lambda shape: pl.BlockSpec((None, G) + shape, lambda b, t: (0,) * (2 + len(shape)))
    out_st = lambda shape: pl.BlockSpec((None, G) + shape, lambda b, t: (0, b) + (0,) * len(shape))
    cst = lambda a: pl.BlockSpec(a.shape, lambda b, t: (0,) * a.ndim)
    sd_shape, sq_shape, sc_shape = (N_HEADS, DK, DV), (SHORT_CONV - 1, QKV_DIM), (CONV_WIDTH - 1, C_CONV)
    kern = functools.partial(_mixer_kernel, G=G, T=T, C=C, NT=NT)
    return pl.pallas_call(
        kern,
        grid=(NB, NT),
        in_specs=[seq(a) for a in rows] + [st(sd_shape), st(sq_shape), st(sc_shape)] + [cst(a) for a in prm],
        out_specs=[pl.BlockSpec((T, D_MODEL), lambda b, t: (b * NT + t, 0)),
                   out_st(sd_shape), out_st(sq_shape), out_st(sc_shape)],
        out_shape=[jax.ShapeDtypeStruct((B * L, D_MODEL), F32), jax.ShapeDtypeStruct((1, B) + sd_shape, F32),
                   jax.ShapeDtypeStruct((1, B) + sq_shape, F32), jax.ShapeDtypeStruct((1, B) + sc_shape, F32)],
        scratch_shapes=[pltpu.VMEM((G, QKV_PAD + T, QKV_DIM), F32), pltpu.VMEM((G, CONV_PAD + T, C_CONV), F32),
                        pltpu.VMEM((G, SUBLANES - 1, T + CONV_PAD - SUBLANES, C_CONV), F32),
                        pltpu.VMEM((R, DK_TOT), F32), pltpu.VMEM((R, DK_TOT), F32), pltpu.VMEM((R, DV_TOT), F32),
                        pltpu.VMEM((R, DV_TOT + C_CONV), F32)],
        compiler_params=pltpu.CompilerParams(dimension_semantics=("arbitrary", "arbitrary"),
                                             vmem_limit_bytes=VMEM_LIMIT),
        name="mixer",
    )(*rows, sd0, sq0, sc0, *prm)


def kernel(x_prompt, x_sample, state_delta, state_qkv_conv, state_conv, meta_tokens, ffn1_norm, ffn1_w1, ffn1_w3,
           ffn1_w2, mix_norm, w_in, qkv_conv_w, a_log, dt_bias, o_norm, conv_w, conv_b, conv_norm, w_out, ffn2_norm,
           ffn2_w1, ffn2_w3, ffn2_w2, final_norm):
    bp, seq_len, _ = x_prompt.shape
    bs, dec_len, _ = x_sample.shape
    assert state_delta.shape[0] == 1
    row = lambda v: v.reshape(1, -1).astype(F32)
    mat = lambda w: w.reshape(w.shape[1:]).astype(BF16)

    w_qkvz, w_glu, w_ba = _split_w_in(w_in.reshape(w_in.shape[1:]), FRONT_ROWS)
    f1 = (row(ffn1_norm), mat(ffn1_w1), mat(ffn1_w3), mat(ffn1_w2), row(mix_norm), w_qkvz, w_glu, w_ba)
    f2 = (row(ffn2_norm), mat(ffn2_w1), mat(ffn2_w3), mat(ffn2_w2))
    lane_row = lambda v: jnp.zeros((1, LANES), F32).at[0, N_HEADS:2 * N_HEADS].set(v.reshape(-1).astype(F32))
    taps8 = lambda w: jnp.broadcast_to(w.astype(F32).reshape(w.shape[1], 1, w.shape[2]),
                                       (w.shape[1], SUBLANES, w.shape[2]))
    prm = (taps8(qkv_conv_w), lane_row(a_log), lane_row(dt_bias), row(o_norm),
           taps8(conv_w), row(conv_b), row(conv_norm), mat(w_out))
    f2 = f2 + (row(final_norm),)

    n_prompt, n_sample = bp * seq_len, bs * dec_len
    pad = (-(n_sample + N_META)) % FRONT_ROWS
    tail = jnp.concatenate([x_sample.reshape(n_sample, D_MODEL), meta_tokens.astype(F32),
                            jnp.zeros((pad, D_MODEL), F32)], axis=0)
    rows = _front(x_prompt.reshape(n_prompt, D_MODEL), tail, f1, FRONT_ROWS)

    zeros = lambda *s: jnp.zeros(s, F32)
    _, sd_m, sq_m, sc_m = _mixer(rows, zeros(1, 1, N_HEADS, DK, DV), zeros(1, 1, SHORT_CONV - 1, QKV_DIM),
                                 zeros(1, 1, CONV_WIDTH - 1, C_CONV), prm, first_row=n_prompt + n_sample,
                                 n_seq=1, seq_len=N_META, T=N_META, C=N_META)

    h2p, sd_p, sq_p, sc_p = _mixer(rows, sd_m, sq_m, sc_m, prm, first_row=0, n_seq=bp, seq_len=seq_len,
                                   T=MIX_ROWS, C=CHUNK)

    time_major = lambda a: jnp.transpose(a.reshape(a.shape[1:]), (1, 0, 2))
    seq_major = lambda a: jnp.transpose(a, (1, 0, 2))[None]
    h2s, sd_s, sq_s, sc_s = _sample_mixer(rows, state_delta, time_major(state_qkv_conv), time_major(state_conv), prm,
                                          first_row=n_prompt, T=dec_len, G=SAMPLE_GROUP)

    y_prompt, y_sample = _back(h2p, h2s, f2, BACK_ROWS)
    return (y_prompt.reshape(bp, seq_len, D_MODEL), y_sample.reshape(bs, dec_len, D_MODEL),
            sd_p, sq_p, sc_p, sd_s, seq_major(sq_s), seq_major(sc_s))
```

```python
import functools

import jax
import jax.numpy as jnp
from jax import lax
from jax.experimental import pallas as pl
from jax.experimental.pallas import tpu as pltpu

D_MODEL = 1024
D_FF = 2816
N_HEADS = 4
DK = 128
DV = 128
DK_TOT = N_HEADS * DK
DV_TOT = N_HEADS * DV
QKV_DIM = 2 * DK_TOT + DV_TOT
C_CONV = 512
SHORT_CONV = 4
CONV_WIDTH = 31
CHUNK = 64
N_META = 16
EPS = 1e-6

LANES = 128
SUBLANES = 8
QKV_PAD = SUBLANES
CONV_PAD = 4 * SUBLANES
FRONT_ROWS = 256
BACK_ROWS = 512
MIX_ROWS = 512
SAMPLE_GROUP = 16
N_CAST = 16
VMEM_LIMIT = 56 * 1024 * 1024

F32 = jnp.float32
BF16 = jnp.bfloat16


def _rms(x, g):
    return x * lax.rsqrt(jnp.mean(x * x, axis=-1, keepdims=True) + EPS) * g


def _silu(x):
    return x * jax.nn.sigmoid(x)


def _wdot(x, w):
    return jnp.dot(x.astype(BF16), w, preferred_element_type=F32)


def _sdot(a, b):
    return jnp.dot(a, b, preferred_element_type=F32)


def _stack(*xs):
    return jnp.concatenate(xs, axis=0)


def _ffn_half(h, g_ref, w1_ref, w3_ref, w2_ref):
    u = _rms(h, g_ref[...]).astype(BF16)
    a = jnp.dot(u, w1_ref[...], preferred_element_type=F32)
    b = jnp.dot(u, w3_ref[...], preferred_element_type=F32)
    hid = (_silu(a) * b).astype(BF16)
    return h + 0.5 * jnp.dot(hid, w2_ref[...], preferred_element_type=F32)


def _cast_ffn_chunk(s, srcs, dsts):
    for src, dst in zip(srcs, dsts):
        rc = src.shape[0]
        dst[pl.ds(pl.multiple_of(s * rc, rc), rc), :] = src[...].astype(BF16)


def _front_kernel(xa_ref, xb_ref, w1f_ref, w3f_ref, w2f_ref, g1_ref, gm_ref, wqkvz_ref, wglu_ref, wba_ref,
                  h1_ref, qkv_ref, z_ref, cin_ref, ba_ref, w1_ref, w3_ref, w2_ref, *, na):
    s = pl.program_id(0)

    @pl.when(s < N_CAST)
    def _():
        _cast_ffn_chunk(s, (w1f_ref, w3f_ref, w2f_ref), (w1_ref, w3_ref, w2_ref))

    @pl.when(s >= N_CAST)
    def _():
        x = jnp.where(s - N_CAST < na, xa_ref[...], xb_ref[...])
        h1 = _ffn_half(x, g1_ref, w1_ref, w3_ref, w2_ref)
        h1_ref[...] = h1
        u = _rms(h1, gm_ref[...]).astype(BF16)
        p = jnp.dot(u, wqkvz_ref[...], preferred_element_type=F32)
        qkv_ref[...] = p[:, :QKV_DIM]
        z_ref[...] = p[:, QKV_DIM:]
        glu = jnp.dot(u, wglu_ref[...], preferred_element_type=F32)
        cin_ref[...] = glu[:, :C_CONV] * jax.nn.sigmoid(glu[:, C_CONV:])
        ba_ref[...] = jnp.dot(u, wba_ref[...], preferred_element_type=F32)


def _back_kernel(ha_ref, hb_ref, w1f_ref, w3f_ref, w2f_ref, g2_ref, gf_ref, ya_ref, yb_ref,
                 w1_ref, w3_ref, w2_ref, *, na):
    s = pl.program_id(0)

    @pl.when(s < N_CAST)
    def _():
        _cast_ffn_chunk(s, (w1f_ref, w3f_ref, w2f_ref), (w1_ref, w3_ref, w2_ref))

    def run(h_ref, y_ref):
        y_ref[...] = _rms(_ffn_half(h_ref[...], g2_ref, w1_ref, w3_ref, w2_ref), gf_ref[...])

    pl.when((s >= N_CAST) & (s - N_CAST < na))(functools.partial(run, ha_ref, ya_ref))
    pl.when(s - N_CAST >= na)(functools.partial(run, hb_ref, yb_ref))


def _const_spec(shape):
    nd = len(shape)
    return pl.BlockSpec(shape, lambda *_: (0,) * nd, pipeline_mode=pl.Buffered(1))


def _two_source_specs(rows, na):
    first = pl.BlockSpec((rows, D_MODEL), lambda s: (jnp.clip(s - N_CAST, 0, na - 1), 0))
    second = pl.BlockSpec((rows, D_MODEL), lambda s: (jnp.maximum(s - N_CAST - na, 0), 0))
    return first, second


def _ffn_weight_specs(ws):
    chunk = lambda w: pl.BlockSpec((w.shape[0] // N_CAST, w.shape[1]), lambda s: (jnp.minimum(s, N_CAST - 1), 0))
    return [chunk(w) for w in ws], [pltpu.VMEM(w.shape, BF16) for w in ws]


def _front(xa, xb, ffn_w, prm, rows):
    na, nb = xa.shape[0] // rows, xb.shape[0] // rows
    n = xa.shape[0] + xb.shape[0]
    widths = (D_MODEL, QKV_DIM, DV_TOT, C_CONV, LANES)
    w_specs, w_scratch = _ffn_weight_specs(ffn_w)
    return pl.pallas_call(
        functools.partial(_front_kernel, na=na),
        grid=(N_CAST + na + nb,),
        in_specs=list(_two_source_specs(rows, na)) + w_specs + [_const_spec(a.shape) for a in prm],
        out_specs=[pl.BlockSpec((rows, w), lambda s: (jnp.maximum(s - N_CAST, 0), 0)) for w in widths],
        out_shape=[jax.ShapeDtypeStruct((n, w), F32) for w in widths],
        scratch_shapes=w_scratch,
        compiler_params=pltpu.CompilerParams(dimension_semantics=("arbitrary",), vmem_limit_bytes=VMEM_LIMIT),
        name="front",
    )(xa, xb, *ffn_w, *prm)


def _back(ha, hb, ffn_w, prm, rows):
    na, nb = ha.shape[0] // rows, hb.shape[0] // rows
    w_specs, w_scratch = _ffn_weight_specs(ffn_w)
    return pl.pallas_call(
        functools.partial(_back_kernel, na=na),
        grid=(N_CAST + na + nb,),
        in_specs=list(_two_source_specs(rows, na)) + w_specs + [_const_spec(a.shape) for a in prm],
        out_specs=list(_two_source_specs(rows, na)),
        out_shape=[jax.ShapeDtypeStruct(ha.shape, F32), jax.ShapeDtypeStruct(hb.shape, F32)],
        scratch_shapes=w_scratch,
        compiler_params=pltpu.CompilerParams(dimension_semantics=("arbitrary",), vmem_limit_bytes=VMEM_LIMIT),
        name="back",
    )(ha, hb, *ffn_w, *prm)


def _split_w_in_kernel(w_ref, qkvz_ref, glu_ref, ba_ref):
    n_gate = 2 * N_HEADS
    w = w_ref[...]
    qkvz_ref[...] = w[:, :QKV_DIM + DV_TOT].astype(BF16)
    glu_ref[...] = w[:, QKV_DIM + DV_TOT + n_gate:].astype(BF16)
    ba = w[:, QKV_DIM + DV_TOT:QKV_DIM + DV_TOT + LANES]
    lane = lax.broadcasted_iota(jnp.int32, ba.shape, 1)
    ba_ref[...] = jnp.where(lane < n_gate, ba, 0.0).astype(BF16)


def _split_w_in(w, rows):
    d, n = w.shape
    widths = (QKV_DIM + DV_TOT, 2 * C_CONV, LANES)
    return pl.pallas_call(
        _split_w_in_kernel,
        grid=(d // rows,),
        in_specs=[pl.BlockSpec((rows, n), lambda i: (i, 0))],
        out_specs=[pl.BlockSpec((rows, wd), lambda i: (i, 0)) for wd in widths],
        out_shape=[jax.ShapeDtypeStruct((d, wd), BF16) for wd in widths],
        name="split_w_in",
    )(w)


def _block_cumsum(x, block, reverse=False):
    rows = x.shape[0]
    r = lax.broadcasted_iota(jnp.int32, x.shape, 0) % block
    s = 1
    while s < block:
        if reverse:
            x = x + jnp.where(r < block - s, pltpu.roll(x, rows - s, axis=0), 0.0)
        else:
            x = x + jnp.where(r >= s, pltpu.roll(x, s, axis=0), 0.0)
        s *= 2
    return x


def _tap_sum(read, w_ref, n_taps, rows):
    accs = [None] * (rows // SUBLANES)
    for j in range(n_taps):
        w8 = w_ref[j]
        for i in range(len(accs)):
            tap = read(j, i) * w8
            accs[i] = tap if accs[i] is None else accs[i] + tap
    return accs[0] if len(accs) == 1 else jnp.concatenate(accs, axis=0)


def _transpose_rows(x):
    rows = x.shape[0]
    pad = (-rows) % LANES
    if pad:
        x = jnp.concatenate([x, jnp.zeros((pad, x.shape[1]), x.dtype)], axis=0)
    return x.T[:, :rows]


def _gates(ba, alog_ref, dtb_ref, block):
    beta_all = jax.nn.sigmoid(ba)
    xg = ba + dtb_ref[...]
    g_all = -jnp.exp(alog_ref[...]) * (jnp.maximum(xg, 0.0) + jnp.log1p(jnp.exp(-jnp.abs(xg))))
    return beta_all, g_all, _block_cumsum(g_all, block)


def _lane_heads(cols, offset, n_lanes):
    rows = cols.shape[0]
    lane_head = lax.broadcasted_iota(jnp.int32, (rows, n_lanes), 1) // (n_lanes // N_HEADS)
    out = jnp.broadcast_to(cols[:, offset + N_HEADS - 1:offset + N_HEADS], (rows, n_lanes))
    for h in range(N_HEADS - 2, -1, -1):
        out = jnp.where(lane_head == h, cols[:, offset + h:offset + h + 1], out)
    return out


def _head_blocks(x):
    rows, n = x.shape
    lane_head = lax.broadcasted_iota(jnp.int32, (rows, n), 1) // (n // N_HEADS)
    return jnp.concatenate([jnp.where(lane_head == h, x, 0.0) for h in range(N_HEADS)], axis=0)


def _delta_chain(q_s, k_s, v_s, z_ref, sd_ref, mix_s, onorm_ref, beta_all, gc_all, gc_all_t, *, R, C):
    HL = N_HEADS * C
    row_i = lax.broadcasted_iota(jnp.int32, (C, HL), 0)
    col_i = lax.broadcasted_iota(jnp.int32, (C, HL), 1) % C
    eye_p, causal_p, strict_p = row_i == col_i, row_i >= col_i, row_i > col_i
    hsl = lambda h: slice(h * DK, (h + 1) * DK)
    nt_dims = (((1,), (1,)), ((), ()))
    chunks = list(range(0, R, C))
    k_t_all = [_transpose_rows(k_s[:, hsl(h)]) for h in range(N_HEADS)]

    kbs, egcs, decays, a_mat, atts = {}, {}, {}, {}, {}
    for c0 in chunks:
        rs = slice(c0, c0 + C)
        gcol = _lane_heads(gc_all[rs], N_HEADS, HL)
        grow = jnp.sum(jnp.where(eye_p, gcol, 0.0), axis=0, keepdims=True)
        decays[c0] = jnp.exp(jnp.where(causal_p, gcol - grow, -jnp.inf))
        beta = beta_all[rs]
        egcs[c0] = jnp.exp(gc_all[rs])
        k = k_s[rs, :]
        kbs[c0] = jnp.concatenate([k[:, hsl(h)] * beta[:, h:h + 1] for h in range(N_HEADS)], axis=1)
        kq = lax.dot_general(_stack(kbs[c0], q_s[rs, :]), _head_blocks(k), nt_dims,
                             preferred_element_type=F32)
        a_mat[c0] = jnp.where(strict_p, kq[:C] * decays[c0], 0.0)
        atts[c0] = kq[C:] * decays[c0]

    eye = eye_p.astype(F32)
    t_inv = {c0: eye - a_mat[c0] for c0 in chunks}
    if C > 2:
        xs = {c0: _sdot(a_mat[c0], _head_blocks(a_mat[c0])) for c0 in chunks}
        m = 2
        while 2 * m < C:
            prod = {c0: _sdot(_stack(xs[c0], t_inv[c0]), _head_blocks(xs[c0])) for c0 in chunks}
            t_inv = {c0: t_inv[c0] + prod[c0][C:] for c0 in chunks}
            xs = {c0: prod[c0][:C] for c0 in chunks}
            m *= 2
        t_inv = {c0: t_inv[c0] + _sdot(t_inv[c0], _head_blocks(xs[c0])) for c0 in chunks}

    o_loc, q_til, n_loc, kws, e_last = {}, {}, {}, {}, {}
    for c0 in chunks:
        rs = slice(c0, c0 + C)
        beta, egc = beta_all[rs], egcs[c0]
        per_head = lambda x, cols, off: jnp.concatenate(
            [x[:, hsl(h)] * cols[:, off + h:off + h + 1] for h in range(N_HEADS)], axis=1)
        vb = per_head(v_s[rs, :], beta, 0)
        kbg = per_head(kbs[c0], egc, N_HEADS)
        uw = _sdot(t_inv[c0], jnp.concatenate([_head_blocks(vb), _head_blocks(kbg)], axis=1))
        u, w = uw[:, :DV_TOT], uw[:, DV_TOT:]
        au = _sdot(atts[c0], jnp.concatenate([_head_blocks(u), _head_blocks(w)], axis=1))
        o_loc[c0] = au[:, :DV_TOT]
        q_til[c0] = per_head(q_s[rs, :], egc, N_HEADS) - au[:, DV_TOT:]
        for h in range(N_HEADS):
            g_last = gc_all[c0 + C - 1:c0 + C, N_HEADS + h:N_HEADS + h + 1]
            k_til_t = k_t_all[h][:, c0:c0 + C] * jnp.exp(g_last - gc_all_t[N_HEADS + h:N_HEADS + h + 1, c0:c0 + C])
            kuw = _sdot(k_til_t, jnp.concatenate([u[:, hsl(h)], w[:, hsl(h)]], axis=1))
            n_loc[c0, h], kws[c0, h], e_last[c0, h] = kuw[:, :DV], kuw[:, DV:], jnp.exp(g_last)

    zero = jnp.zeros((DK, DV), F32)
    for c0 in chunks:
        rs = slice(c0, c0 + C)
        for h0 in range(0, N_HEADS, 2):
            h1 = h0 + 1
            s0, s1 = sd_ref[0, h0], sd_ref[0, h1]
            s_pair = jnp.concatenate([jnp.concatenate([s0, zero], axis=1), jnp.concatenate([zero, s1], axis=1)], axis=0)
            lhs = _stack(jnp.concatenate([kws[c0, h0], kws[c0, h1]], axis=1), q_til[c0][:, h0 * DK:(h1 + 1) * DK])
            prod = _sdot(lhs, s_pair)
            for i, (h, s_old) in enumerate(((h0, s0), (h1, s1))):
                o = prod[DK:, i * DV:(i + 1) * DV] + o_loc[c0][:, hsl(h)]
                sd_ref[0, h] = s_old * e_last[c0, h] - prod[:DK, i * DV:(i + 1) * DV] + n_loc[c0, h]
                mix_s[rs, hsl(h)] = _rms(o, onorm_ref[...]) * _silu(z_ref[rs, hsl(h)])


def _delta_blocks(q_s, k_s, v_s, z_ref, sd_ref, mix_s, onorm_ref, beta_all, gc_all, gc_all_t, gtail_all, *, R, C):
    rows2d = lambda ref, cols: ref[:, :, cols].reshape(R, cols.stop - cols.start)
    row_i = lax.broadcasted_iota(jnp.int32, (R, R), 0)
    col_i = lax.broadcasted_iota(jnp.int32, (R, R), 1)
    same = (row_i // C) == (col_i // C)
    causal = same & (row_i >= col_i)
    strict = same & (row_i > col_i)
    eye = (row_i == col_i).astype(F32)
    hsl = lambda h: slice(h * DK, (h + 1) * DK)
    heads = range(N_HEADS)

    gtail_t = _transpose_rows(gtail_all)
    gc_cs, kbs, decays, qs, kq, k_ts = {}, {}, {}, {}, {}, {}
    for h in heads:
        gc_cs[h] = gc_all[:, N_HEADS + h:N_HEADS + h + 1]
        gc_r = gc_all_t[N_HEADS + h:N_HEADS + h + 1, :]
        decays[h] = jnp.exp(jnp.where(causal, gc_cs[h] - gc_r, -jnp.inf))
        k = rows2d(k_s, hsl(h))
        kbs[h] = k * beta_all[:, h:h + 1]
        qs[h] = rows2d(q_s, hsl(h))
        k_ts[h] = _transpose_rows(k)
        kq[h] = _sdot(_stack(kbs[h], qs[h]), k_ts[h])

    a_mat = {h: jnp.where(strict, kq[h][:R] * decays[h], 0.0) for h in heads}
    t_inv = {h: eye - a_mat[h] for h in heads}
    if C > 2:
        xs = {h: _sdot(a_mat[h], a_mat[h]) for h in heads}
        m = 2
        while 2 * m < C:
            prod = {h: _sdot(_stack(xs[h], t_inv[h]), xs[h]) for h in heads}
            t_inv = {h: t_inv[h] + prod[h][R:] for h in heads}
            xs = {h: prod[h][:R] for h in heads}
            m *= 2
        t_inv = {h: t_inv[h] + _sdot(t_inv[h], xs[h]) for h in heads}

    for h in heads:
        egc = jnp.exp(gc_cs[h])
        vb = rows2d(v_s, hsl(h)) * beta_all[:, h:h + 1]
        uw = _sdot(t_inv[h], jnp.concatenate([vb, kbs[h] * egc], axis=1))
        u, w, att, qg = uw[:, :DV], uw[:, DV:], kq[h][R:] * decays[h], qs[h] * egc
        k_til_t = k_ts[h] * jnp.exp(gtail_t[N_HEADS + h:N_HEADS + h + 1, :])
        wss, qss = [], []
        for b in range(R // C):
            bs = slice(b * C, (b + 1) * C)
            wq = _sdot(_stack(w[bs], qg[bs]), sd_ref[b, h])
            wss.append(wq[:C])
            qss.append(wq[C:])
        v_new = u - jnp.concatenate(wss, axis=0)
        o = jnp.concatenate(qss, axis=0) + _sdot(att, v_new)
        for b in range(R // C):
            bs = slice(b * C, (b + 1) * C)
            e_b = jnp.exp(gc_cs[h][(b + 1) * C - 1:(b + 1) * C, :])
            sd_ref[b, h] = sd_ref[b, h] * e_b + _sdot(k_til_t[:, bs], v_new[bs])
        mix_s[:, :, hsl(h)] = (_rms(o, onorm_ref[...]) * _silu(rows2d(z_ref, hsl(h)))).reshape(R // C, C, DV)


def _mixer_kernel(h1_ref, qkv_ref, z_ref, cin_ref, ba_ref, sd0_ref, sq0_ref, sc0_ref,
                  qw_ref, alog_ref, dtb_ref, onorm_ref, cw_ref, cb_ref, cnorm_ref, wout_ref,
                  h2_ref, sd_ref, sq_ref, sc_ref,
                  extq_ref, extc_ref, shc_ref, q_s, k_s, v_s, mix_s,
                  *, G, T, C, NT):
    R = G * T
    t = pl.program_id(1)

    @pl.when(t == 0)
    def _():
        extq_ref[:, QKV_PAD - (SHORT_CONV - 1):QKV_PAD, :] = sq0_ref[...]
        extc_ref[:, CONV_PAD - (CONV_WIDTH - 1):CONV_PAD, :] = sc0_ref[...]
        sd_ref[...] = sd0_ref[...]

    extq_ref[0, QKV_PAD:QKV_PAD + T, :] = qkv_ref[...]
    extc_ref[0, CONV_PAD:CONV_PAD + T, :] = cin_ref[...]

    qk_scale = DK ** -0.5
    rbq = min(2 * SUBLANES, T)
    for g in range(G):
        for r0 in range(0, T, rbq):
            base = QKV_PAD - (SHORT_CONV - 1) + r0
            read = lambda j, i, g=g, base=base: extq_ref[g, pl.ds(base + j + SUBLANES * i, SUBLANES), :]
            s = _silu(_tap_sum(read, qw_ref, SHORT_CONV, rbq))
            rows = slice(g * T + r0, g * T + r0 + rbq)
            for h in range(N_HEADS):
                qh = s[:, h * DK:(h + 1) * DK]
                kh = s[:, DK_TOT + h * DK:DK_TOT + (h + 1) * DK]
                q_s[rows, h * DK:(h + 1) * DK] = qh * (lax.rsqrt(jnp.sum(qh * qh, -1, keepdims=True) + EPS) * qk_scale)
                k_s[rows, h * DK:(h + 1) * DK] = kh * lax.rsqrt(jnp.sum(kh * kh, -1, keepdims=True) + EPS)
            v_s[rows, :] = s[:, 2 * DK_TOT:]
    sq_ref[...] = extq_ref[:, T + QKV_PAD - (SHORT_CONV - 1):T + QKV_PAD, :]

    first = CONV_PAD - (CONV_WIDTH - 1)
    sh_rows = T + CONV_PAD - SUBLANES
    for g in range(G):
        for b in range(1, SUBLANES):
            for r0 in range(0, sh_rows, 32):
                rb = min(32, sh_rows - r0)
                shc_ref[g, b - 1, r0:r0 + rb, :] = extc_ref[g, r0 + b:r0 + b + rb, :]
    rbc = min(8 * SUBLANES, T)
    for g in range(G):
        for r0 in range(0, T, rbc):
            def read(j, i, g=g, r0=r0):
                a, b = divmod(j + first, SUBLANES)
                rows = pl.ds(r0 + SUBLANES * (a + i), SUBLANES)
                return extc_ref[g, rows, :] if b == 0 else shc_ref[g, b - 1, rows, :]
            acc = _tap_sum(read, cw_ref, CONV_WIDTH, rbc)
            mix_s[g * T + r0:g * T + r0 + rbc, DV_TOT:] = _silu(_rms(acc + cb_ref[...], cnorm_ref[...]))
    sc_ref[...] = extc_ref[:, T + CONV_PAD - (CONV_WIDTH - 1):T + CONV_PAD, :]

    if NT > 1:
        extq_ref[:, 0:QKV_PAD, :] = extq_ref[:, T:T + QKV_PAD, :]
        extc_ref[:, 0:CONV_PAD, :] = extc_ref[:, T:T + CONV_PAD, :]

    beta_all, g_all, gc_all = _gates(ba_ref[...], alog_ref, dtb_ref, C)
    _delta_chain(q_s, k_s, v_s, z_ref, sd_ref, mix_s, onorm_ref, beta_all, gc_all, _transpose_rows(gc_all), R=R, C=C)

    h2_ref[...] = h1_ref[...] + _wdot(mix_s[...], wout_ref[...])


def _sample_mixer_kernel(h1_ref, qkv_ref, z_ref, cin_ref, ba_ref, sd0_ref, sq0_ref, sc0_ref,
                         qw_ref, alog_ref, dtb_ref, onorm_ref, cw_ref, cb_ref, cnorm_ref, wout_ref,
                         h2_ref, sd_ref, sq_ref, sc_ref, q_s, k_s, v_s, mix_s, *, G, T):
    R = G * T
    n_q, n_c = SHORT_CONV - 1, CONV_WIDTH - 1
    sd_ref[...] = sd0_ref[...]
    hsl = lambda h: slice(h * DK, (h + 1) * DK)
    qk_scale = DK ** -0.5
    for gi in range(G // SUBLANES):
        gs = slice(gi * SUBLANES, (gi + 1) * SUBLANES)
        qkv_row = lambda r: sq0_ref[r, gs, :] if r < n_q else qkv_ref[gs, r - n_q, :]
        cin_row = lambda r: sc0_ref[r, gs, :] if r < n_c else cin_ref[gs, r - n_c, :]

        for t in range(T):
            acc = None
            for j in range(SHORT_CONV):
                tap = qkv_row(t + j) * qw_ref[j]
                acc = tap if acc is None else acc + tap
            s = _silu(acc)
            for h in range(N_HEADS):
                qh = s[:, hsl(h)]
                kh = s[:, DK_TOT + h * DK:DK_TOT + (h + 1) * DK]
                q_s[gs, t, hsl(h)] = qh * (lax.rsqrt(jnp.sum(qh * qh, -1, keepdims=True) + EPS) * qk_scale)
                k_s[gs, t, hsl(h)] = kh * lax.rsqrt(jnp.sum(kh * kh, -1, keepdims=True) + EPS)
            v_s[gs, t, :] = s[:, 2 * DK_TOT:]

        accs = [None] * T
        for r in range(n_c + T):
            row = cin_row(r)
            for t in range(max(0, r - n_c), min(T, r + 1)):
                tap = row * cw_ref[r - t]
                accs[t] = tap if accs[t] is None else accs[t] + tap
        for t in range(T):
            mix_s[gs, t, DV_TOT:] = _silu(_rms(accs[t] + cb_ref[...], cnorm_ref[...]))

    for r in range(n_q):
        sq_ref[r] = sq0_ref[r + T] if r + T < n_q else qkv_ref[:, r + T - n_q, :]
    for r in range(n_c):
        sc_ref[r] = sc0_ref[r + T] if r + T < n_c else cin_ref[:, r + T - n_c, :]

    beta_all, g_all, gc_all = _gates(ba_ref[...].reshape(R, LANES), alog_ref, dtb_ref, T)
    gtail_all = _block_cumsum(g_all, T, reverse=True) - g_all
    _delta_blocks(q_s, k_s, v_s, z_ref, sd_ref, mix_s, onorm_ref, beta_all, gc_all, _transpose_rows(gc_all), gtail_all,
                  R=R, C=T)

    mix = _wdot(mix_s[...].reshape(R, DV_TOT + C_CONV), wout_ref[...])
    h2_ref[...] = h1_ref[...].reshape(R, D_MODEL) + mix


def _sample_mixer(rows, sd0, sq0, sc0, prm, *, first_row, T, G):
    B = sd0.shape[1]
    R = G * T
    rows = [a.reshape(a.shape[0] // T, T, a.shape[1]) for a in rows]
    seq = lambda a: pl.BlockSpec((G, T, a.shape[2]), lambda b: (first_row // R + b, 0, 0))
    tm = lambda a: pl.BlockSpec((a.shape[0], G, a.shape[2]), lambda b: (0, b, 0))
    sd_spec = pl.BlockSpec((None, G) + sd0.shape[2:], lambda b: (0, b, 0, 0, 0))
    cst = lambda a: pl.BlockSpec(a.shape, lambda b: (0,) * a.ndim)
    return pl.pallas_call(
        functools.partial(_sample_mixer_kernel, G=G, T=T),
        grid=(B // G,),
        in_specs=[seq(a) for a in rows] + [sd_spec, tm(sq0), tm(sc0)] + [cst(a) for a in prm],
        out_specs=[pl.BlockSpec((R, D_MODEL), lambda b: (b, 0)), sd_spec, tm(sq0), tm(sc0)],
        out_shape=[jax.ShapeDtypeStruct((B * T, D_MODEL), F32), jax.ShapeDtypeStruct(sd0.shape, F32),
                   jax.ShapeDtypeStruct(sq0.shape, F32), jax.ShapeDtypeStruct(sc0.shape, F32)],
        scratch_shapes=[pltpu.VMEM((G, T, DK_TOT), F32), pltpu.VMEM((G, T, DK_TOT), F32),
                        pltpu.VMEM((G, T, DV_TOT), F32), pltpu.VMEM((G, T, DV_TOT + C_CONV), F32)],
        compiler_params=pltpu.CompilerParams(dimension_semantics=("arbitrary",), vmem_limit_bytes=VMEM_LIMIT),
        name="sample_mixer",
    )(*rows, sd0, sq0, sc0, *prm)


def _mixer(rows, sd0, sq0, sc0, prm, *, first_row, n_seq, seq_len, T, C):
    B, L = n_seq, seq_len
    G, NB, NT = 1, B, L // T
    R = G * T
    seq = lambda a: pl.BlockSpec((T, a.shape[1]), lambda b, t: (first_row // T + b * NT + t, 0))
    st = lambda shape: pl.BlockSpec((None, G) + shape, lambda b, t: (0,) * (2 + len(shape)))
    out_st = lambda shape: pl.BlockSpec((None, G) + shape, lambda b, t: (0, b) + (0,) * len(shape))
    cst = lambda a: pl.BlockSpec(a.shape, lambda b, t: (0,) * a.ndim)
    sd_shape, sq_shape, sc_shape = (N_HEADS, DK, DV), (SHORT_CONV - 1, QKV_DIM), (CONV_WIDTH - 1, C_CONV)
    kern = functools.partial(_mixer_kernel, G=G, T=T, C=C, NT=NT)
    return pl.pallas_call(
        kern,
        grid=(NB, NT),
        in_specs=[seq(a) for a in rows] + [st(sd_shape), st(sq_shape), st(sc_shape)] + [cst(a) for a in prm],
        out_specs=[pl.BlockSpec((T, D_MODEL), lambda b, t: (b * NT + t, 0)),
                   out_st(sd_shape), out_st(sq_shape), out_st(sc_shape)],
        out_shape=[jax.ShapeDtypeStruct((B * L, D_MODEL), F32), jax.ShapeDtypeStruct((1, B) + sd_shape, F32),
                   jax.ShapeDtypeStruct((1, B) + sq_shape, F32), jax.ShapeDtypeStruct((1, B) + sc_shape, F32)],
        scratch_shapes=[pltpu.VMEM((G, QKV_PAD + T, QKV_DIM), F32), pltpu.VMEM((G, CONV_PAD + T, C_CONV), F32),
                        pltpu.VMEM((G, SUBLANES - 1, T + CONV_PAD - SUBLANES, C_CONV), F32),
                        pltpu.VMEM((R, DK_TOT), F32), pltpu.VMEM((R, DK_TOT), F32), pltpu.VMEM((R, DV_TOT), F32),
                        pltpu.VMEM((R, DV_TOT + C_CONV), F32)],
        compiler_params=pltpu.CompilerParams(dimension_semantics=("arbitrary", "arbitrary"),
                                             vmem_limit_bytes=VMEM_LIMIT),
        name="mixer",
    )(*rows, sd0, sq0, sc0, *prm)


def kernel(x_prompt, x_sample, state_delta, state_qkv_conv, state_conv, meta_tokens, ffn1_norm, ffn1_w1, ffn1_w3,
           ffn1_w2, mix_norm, w_in, qkv_conv_w, a_log, dt_bias, o_norm, conv_w, conv_b, conv_norm, w_out, ffn2_norm,
           ffn2_w1, ffn2_w3, ffn2_w2, final_norm):
    bp, seq_len, _ = x_prompt.shape
    bs, dec_len, _ = x_sample.shape
    assert state_delta.shape[0] == 1
    row = lambda v: v.reshape(1, -1).astype(F32)
    mat32 = lambda w: w.reshape(w.shape[1:]).astype(F32)

    w_qkvz, w_glu, w_ba = _split_w_in(w_in.reshape(w_in.shape[1:]), FRONT_ROWS)
    ffn1_w = (mat32(ffn1_w1), mat32(ffn1_w3), mat32(ffn1_w2))
    ffn2_w = (mat32(ffn2_w1), mat32(ffn2_w3), mat32(ffn2_w2))
    f1 = (row(ffn1_norm), row(mix_norm), w_qkvz, w_glu, w_ba)
    f2 = (row(ffn2_norm), row(final_norm))
    lane_row = lambda v: jnp.zeros((1, LANES), F32).at[0, N_HEADS:2 * N_HEADS].set(v.reshape(-1).astype(F32))
    taps8 = lambda w: jnp.broadcast_to(w.astype(F32).reshape(w.shape[1], 1, w.shape[2]),
                                       (w.shape[1], SUBLANES, w.shape[2]))
    prm = (taps8(qkv_conv_w), lane_row(a_log), lane_row(dt_bias), row(o_norm),
           taps8(conv_w), row(conv_b), row(conv_norm), w_out.reshape(w_out.shape[1:]).astype(BF16))

    n_prompt, n_sample = bp * seq_len, bs * dec_len
    pad = (-(n_sample + N_META)) % FRONT_ROWS
    tail = jnp.concatenate([x_sample.reshape(n_sample, D_MODEL), meta_tokens.astype(F32),
                            jnp.zeros((pad, D_MODEL), F32)], axis=0)
    rows = _front(x_prompt.reshape(n_prompt, D_MODEL), tail, ffn1_w, f1, FRONT_ROWS)

    zeros = lambda *s: jnp.zeros(s, F32)
    _, sd_m, sq_m, sc_m = _mixer(rows, zeros(1, 1, N_HEADS, DK, DV), zeros(1, 1, SHORT_CONV - 1, QKV_DIM),
                                 zeros(1, 1, CONV_WIDTH - 1, C_CONV), prm, first_row=n_prompt + n_sample,
                                 n_seq=1, seq_len=N_META, T=N_META, C=N_META)

    h2p, sd_p, sq_p, sc_p = _mixer(rows, sd_m, sq_m, sc_m, prm, first_row=0, n_seq=bp, seq_len=seq_len,
                                   T=MIX_ROWS, C=CHUNK)

    time_major = lambda a: jnp.transpose(a.reshape(a.shape[1:]), (1, 0, 2))
    seq_major = lambda a: jnp.transpose(a, (1, 0, 2))[None]
    h2s, sd_s, sq_s, sc_s = _sample_mixer(rows, state_delta, time_major(state_qkv_conv), time_major(state_conv), prm,
                                          first_row=n_prompt, T=dec_len, G=SAMPLE_GROUP)

    y_prompt, y_sample = _back(h2p, h2s, ffn2_w, f2, BACK_ROWS)
    return (y_prompt.reshape(bp, seq_len, D_MODEL), y_sample.reshape(bs, dec_len, D_MODEL),
            sd_p, sq_p, sc_p, sd_s, seq_major(sq_s), seq_major(sc_s))
```

```python
import functools

import jax
import jax.numpy as jnp
from jax import lax
from jax.experimental import pallas as pl
from jax.experimental.pallas import tpu as pltpu

D_MODEL = 1024
D_FF = 2816
N_HEADS = 4
DK = 128
DV = 128
DK_TOT = N_HEADS * DK
DV_TOT = N_HEADS * DV
QKV_DIM = 2 * DK_TOT + DV_TOT
C_CONV = 512
SHORT_CONV = 4
CONV_WIDTH = 31
CHUNK = 64
N_META = 16
EPS = 1e-6

LANES = 128
SUBLANES = 8
QKV_PAD = SUBLANES
CONV_PAD = 4 * SUBLANES
FRONT_ROWS = 256
BACK_ROWS = 512
MIX_ROWS = 512
SAMPLE_GROUP = 16
N_CAST = 8
VMEM_LIMIT = 56 * 1024 * 1024

F32 = jnp.float32
BF16 = jnp.bfloat16


def _rms(x, g):
    return x * lax.rsqrt(jnp.mean(x * x, axis=-1, keepdims=True) + EPS) * g


def _silu(x):
    return x * jax.nn.sigmoid(x)


def _wdot(x, w):
    return jnp.dot(x.astype(BF16), w, preferred_element_type=F32)


def _sdot(a, b):
    return jnp.dot(a, b, preferred_element_type=F32)


def _stack(*xs):
    return jnp.concatenate(xs, axis=0)


def _ffn_half(h, g_ref, w1_ref, w3_ref, w2_ref):
    u = _rms(h, g_ref[...]).astype(BF16)
    a = jnp.dot(u, w1_ref[...], preferred_element_type=F32)
    b = jnp.dot(u, w3_ref[...], preferred_element_type=F32)
    hid = (_silu(a) * b).astype(BF16)
    return h + 0.5 * jnp.dot(hid, w2_ref[...], preferred_element_type=F32)


def _cast_ffn_chunk(s, srcs, dsts):
    for src, dst in zip(srcs, dsts):
        rc = src.shape[0]
        dst[pl.ds(pl.multiple_of(s * rc, rc), rc), :] = src[...].astype(BF16)


def _front_kernel(xa_ref, xb_ref, w1f_ref, w3f_ref, w2f_ref, g1_ref, gm_ref, wqkvz_ref, wglu_ref, wba_ref,
                  h1_ref, qkv_ref, z_ref, cin_ref, ba_ref, w1_ref, w3_ref, w2_ref, *, na):
    s = pl.program_id(0)

    @pl.when(s < N_CAST)
    def _():
        _cast_ffn_chunk(s, (w1f_ref, w3f_ref, w2f_ref), (w1_ref, w3_ref, w2_ref))

    @pl.when(s >= N_CAST)
    def _():
        x = jnp.where(s - N_CAST < na, xa_ref[...], xb_ref[...])
        h1 = _ffn_half(x, g1_ref, w1_ref, w3_ref, w2_ref)
        h1_ref[...] = h1
        u = _rms(h1, gm_ref[...]).astype(BF16)
        p = jnp.dot(u, wqkvz_ref[...], preferred_element_type=F32)
        qkv_ref[...] = p[:, :QKV_DIM]
        z_ref[...] = p[:, QKV_DIM:]
        glu = jnp.dot(u, wglu_ref[...], preferred_element_type=F32)
        cin_ref[...] = glu[:, :C_CONV] * jax.nn.sigmoid(glu[:, C_CONV:])
        ba_ref[...] = jnp.dot(u, wba_ref[...], preferred_element_type=F32)


def _back_kernel(ha_ref, hb_ref, w1f_ref, w3f_ref, w2f_ref, g2_ref, gf_ref, ya_ref, yb_ref,
                 w1_ref, w3_ref, w2_ref, *, na):
    s = pl.program_id(0)

    @pl.when(s < N_CAST)
    def _():
        _cast_ffn_chunk(s, (w1f_ref, w3f_ref, w2f_ref), (w1_ref, w3_ref, w2_ref))

    def run(h_ref, y_ref):
        y_ref[...] = _rms(_ffn_half(h_ref[...], g2_ref, w1_ref, w3_ref, w2_ref), gf_ref[...])

    pl.when((s >= N_CAST) & (s - N_CAST < na))(functools.partial(run, ha_ref, ya_ref))
    pl.when(s - N_CAST >= na)(functools.partial(run, hb_ref, yb_ref))


def _const_spec(shape):
    nd = len(shape)
    return pl.BlockSpec(shape, lambda *_: (0,) * nd, pipeline_mode=pl.Buffered(1))


def _two_source_specs(rows, na):
    first = pl.BlockSpec((rows, D_MODEL), lambda s: (jnp.clip(s - N_CAST, 0, na - 1), 0))
    second = pl.BlockSpec((rows, D_MODEL), lambda s: (jnp.maximum(s - N_CAST - na, 0), 0))
    return first, second


def _ffn_weight_specs(ws):
    chunk = lambda w: pl.BlockSpec((w.shape[0] // N_CAST, w.shape[1]), lambda s: (jnp.minimum(s, N_CAST - 1), 0))
    return [chunk(w) for w in ws], [pltpu.VMEM(w.shape, BF16) for w in ws]


def _front(xa, xb, ffn_w, prm, rows):
    na, nb = xa.shape[0] // rows, xb.shape[0] // rows
    n = xa.shape[0] + xb.shape[0]
    widths = (D_MODEL, QKV_DIM, DV_TOT, C_CONV, LANES)
    w_specs, w_scratch = _ffn_weight_specs(ffn_w)
    return pl.pallas_call(
        functools.partial(_front_kernel, na=na),
        grid=(N_CAST + na + nb,),
        in_specs=list(_two_source_specs(rows, na)) + w_specs + [_const_spec(a.shape) for a in prm],
        out_specs=[pl.BlockSpec((rows, w), lambda s: (jnp.maximum(s - N_CAST, 0), 0)) for w in widths],
        out_shape=[jax.ShapeDtypeStruct((n, w), F32) for w in widths],
        scratch_shapes=w_scratch,
        compiler_params=pltpu.CompilerParams(dimension_semantics=("arbitrary",), vmem_limit_bytes=VMEM_LIMIT),
        name="front",
    )(xa, xb, *ffn_w, *prm)


def _back(ha, hb, ffn_w, prm, rows):
    na, nb = ha.shape[0] // rows, hb.shape[0] // rows
    w_specs, w_scratch = _ffn_weight_specs(ffn_w)
    return pl.pallas_call(
        functools.partial(_back_kernel, na=na),
        grid=(N_CAST + na + nb,),
        in_specs=list(_two_source_specs(rows, na)) + w_specs + [_const_spec(a.shape) for a in prm],
        out_specs=list(_two_source_specs(rows, na)),
        out_shape=[jax.ShapeDtypeStruct(ha.shape, F32), jax.ShapeDtypeStruct(hb.shape, F32)],
        scratch_shapes=w_scratch,
        compiler_params=pltpu.CompilerParams(dimension_semantics=("arbitrary",), vmem_limit_bytes=VMEM_LIMIT),
        name="back",
    )(ha, hb, *ffn_w, *prm)


def _split_w_in_kernel(wt_ref, qkvz_ref, glu_ref, ba_ref):
    n_gate = 2 * N_HEADS
    glu0 = QKV_DIM + DV_TOT + n_gate
    step = 2 * LANES
    for c in range(0, QKV_DIM + DV_TOT, step):
        qkvz_ref[:, c:c + step] = wt_ref[c:c + step, :].T.astype(BF16)
    for c in range(0, 2 * C_CONV, step):
        glu_ref[:, c:c + step] = wt_ref[glu0 + c:glu0 + c + step, :].T.astype(BF16)
    ba = jnp.concatenate([wt_ref[QKV_DIM + DV_TOT:glu0, :], jnp.zeros((LANES - n_gate, D_MODEL), F32)], axis=0)
    ba_ref[...] = ba.T.astype(BF16)


def _split_w_in(wt):
    n, d = wt.shape
    widths = (QKV_DIM + DV_TOT, 2 * C_CONV, LANES)
    return pl.pallas_call(
        _split_w_in_kernel,
        grid=(1,),
        in_specs=[_const_spec(wt.shape)],
        out_specs=[pl.BlockSpec((d, wd), lambda i: (0, 0)) for wd in widths],
        out_shape=[jax.ShapeDtypeStruct((d, wd), BF16) for wd in widths],
        compiler_params=pltpu.CompilerParams(vmem_limit_bytes=VMEM_LIMIT),
        name="split_w_in",
    )(wt)


def _block_cumsum(x, block, reverse=False):
    rows = x.shape[0]
    r = lax.broadcasted_iota(jnp.int32, x.shape, 0) % block
    s = 1
    while s < block:
        if reverse:
            x = x + jnp.where(r < block - s, pltpu.roll(x, rows - s, axis=0), 0.0)
        else:
            x = x + jnp.where(r >= s, pltpu.roll(x, s, axis=0), 0.0)
        s *= 2
    return x


def _tap_sum(read, w_ref, n_taps, rows):
    accs = [None] * (rows // SUBLANES)
    for j in range(n_taps):
        w8 = w_ref[j]
        for i in range(len(accs)):
            tap = read(j, i) * w8
            accs[i] = tap if accs[i] is None else accs[i] + tap
    return accs[0] if len(accs) == 1 else jnp.concatenate(accs, axis=0)


def _transpose_rows(x):
    rows = x.shape[0]
    pad = (-rows) % LANES
    if pad:
        x = jnp.concatenate([x, jnp.zeros((pad, x.shape[1]), x.dtype)], axis=0)
    return x.T[:, :rows]


def _gates(ba, alog_ref, dtb_ref, block):
    beta_all = jax.nn.sigmoid(ba)
    xg = ba + dtb_ref[...]
    g_all = -jnp.exp(alog_ref[...]) * (jnp.maximum(xg, 0.0) + jnp.log1p(jnp.exp(-jnp.abs(xg))))
    return beta_all, g_all, _block_cumsum(g_all, block)


def _lane_heads(cols, offset, n_lanes):
    rows = cols.shape[0]
    lane_head = lax.broadcasted_iota(jnp.int32, (rows, n_lanes), 1) // (n_lanes // N_HEADS)
    out = jnp.broadcast_to(cols[:, offset + N_HEADS - 1:offset + N_HEADS], (rows, n_lanes))
    for h in range(N_HEADS - 2, -1, -1):
        out = jnp.where(lane_head == h, cols[:, offset + h:offset + h + 1], out)
    return out


def _head_blocks(x):
    rows, n = x.shape
    lane_head = lax.broadcasted_iota(jnp.int32, (rows, n), 1) // (n // N_HEADS)
    return jnp.concatenate([jnp.where(lane_head == h, x, 0.0) for h in range(N_HEADS)], axis=0)


def _delta_chain(q_s, k_s, v_s, z_ref, sd_ref, mix_s, onorm_ref, beta_all, gc_all, gc_all_t, *, R, C):
    HL = N_HEADS * C
    row_i = lax.broadcasted_iota(jnp.int32, (C, HL), 0)
    col_i = lax.broadcasted_iota(jnp.int32, (C, HL), 1) % C
    eye_p, causal_p, strict_p = row_i == col_i, row_i >= col_i, row_i > col_i
    hsl = lambda h: slice(h * DK, (h + 1) * DK)
    nt_dims = (((1,), (1,)), ((), ()))
    chunks = list(range(0, R, C))
    k_t_all = [_transpose_rows(k_s[:, hsl(h)]) for h in range(N_HEADS)]

    kbs, egcs, decays, a_mat, atts = {}, {}, {}, {}, {}
    for c0 in chunks:
        rs = slice(c0, c0 + C)
        gcol = _lane_heads(gc_all[rs], N_HEADS, HL)
        grow = jnp.sum(jnp.where(eye_p, gcol, 0.0), axis=0, keepdims=True)
        decays[c0] = jnp.exp(jnp.where(causal_p, gcol - grow, -jnp.inf))
        beta = beta_all[rs]
        egcs[c0] = jnp.exp(gc_all[rs])
        k = k_s[rs, :]
        kbs[c0] = jnp.concatenate([k[:, hsl(h)] * beta[:, h:h + 1] for h in range(N_HEADS)], axis=1)
        kq = lax.dot_general(_stack(kbs[c0], q_s[rs, :]), _head_blocks(k), nt_dims,
                             preferred_element_type=F32)
        a_mat[c0] = jnp.where(strict_p, kq[:C] * decays[c0], 0.0)
        atts[c0] = kq[C:] * decays[c0]

    eye = eye_p.astype(F32)
    t_inv = {c0: eye - a_mat[c0] for c0 in chunks}
    if C > 2:
        xs = {c0: _sdot(a_mat[c0], _head_blocks(a_mat[c0])) for c0 in chunks}
        m = 2
        while 2 * m < C:
            prod = {c0: _sdot(_stack(xs[c0], t_inv[c0]), _head_blocks(xs[c0])) for c0 in chunks}
            t_inv = {c0: t_inv[c0] + prod[c0][C:] for c0 in chunks}
            xs = {c0: prod[c0][:C] for c0 in chunks}
            m *= 2
        t_inv = {c0: t_inv[c0] + _sdot(t_inv[c0], _head_blocks(xs[c0])) for c0 in chunks}

    o_loc, q_til, n_loc, kws, e_last = {}, {}, {}, {}, {}
    for c0 in chunks:
        rs = slice(c0, c0 + C)
        beta, egc = beta_all[rs], egcs[c0]
        per_head = lambda x, cols, off: jnp.concatenate(
            [x[:, hsl(h)] * cols[:, off + h:off + h + 1] for h in range(N_HEADS)], axis=1)
        vb = per_head(v_s[rs, :], beta, 0)
        kbg = per_head(kbs[c0], egc, N_HEADS)
        uw = _sdot(t_inv[c0], jnp.concatenate([_head_blocks(vb), _head_blocks(kbg)], axis=1))
        u, w = uw[:, :DV_TOT], uw[:, DV_TOT:]
        au = _sdot(atts[c0], jnp.concatenate([_head_blocks(u), _head_blocks(w)], axis=1))
        o_loc[c0] = au[:, :DV_TOT]
        q_til[c0] = per_head(q_s[rs, :], egc, N_HEADS) - au[:, DV_TOT:]
        for h in range(N_HEADS):
            g_last = gc_all[c0 + C - 1:c0 + C, N_HEADS + h:N_HEADS + h + 1]
            k_til_t = k_t_all[h][:, c0:c0 + C] * jnp.exp(g_last - gc_all_t[N_HEADS + h:N_HEADS + h + 1, c0:c0 + C])
            kuw = _sdot(k_til_t, jnp.concatenate([u[:, hsl(h)], w[:, hsl(h)]], axis=1))
            n_loc[c0, h], kws[c0, h], e_last[c0, h] = kuw[:, :DV], kuw[:, DV:], jnp.exp(g_last)

    zero = jnp.zeros((DK, DV), F32)
    for c0 in chunks:
        rs = slice(c0, c0 + C)
        for h0 in range(0, N_HEADS, 2):
            h1 = h0 + 1
            s0, s1 = sd_ref[0, h0], sd_ref[0, h1]
            s_pair = jnp.concatenate([jnp.concatenate([s0, zero], axis=1), jnp.concatenate([zero, s1], axis=1)], axis=0)
            lhs = _stack(jnp.concatenate([kws[c0, h0], kws[c0, h1]], axis=1), q_til[c0][:, h0 * DK:(h1 + 1) * DK])
            prod = _sdot(lhs, s_pair)
            for i, (h, s_old) in enumerate(((h0, s0), (h1, s1))):
                o = prod[DK:, i * DV:(i + 1) * DV] + o_loc[c0][:, hsl(h)]
                sd_ref[0, h] = s_old * e_last[c0, h] - prod[:DK, i * DV:(i + 1) * DV] + n_loc[c0, h]
                mix_s[rs, hsl(h)] = _rms(o, onorm_ref[...]) * _silu(z_ref[rs, hsl(h)])


def _delta_blocks(q_s, k_s, v_s, z_ref, sd_ref, mix_s, onorm_ref, beta_all, gc_all, gc_all_t, gtail_all, *, R, C):
    rows2d = lambda ref, cols: ref[:, :, cols].reshape(R, cols.stop - cols.start)
    row_i = lax.broadcasted_iota(jnp.int32, (R, R), 0)
    col_i = lax.broadcasted_iota(jnp.int32, (R, R), 1)
    same = (row_i // C) == (col_i // C)
    causal = same & (row_i >= col_i)
    strict = same & (row_i > col_i)
    eye = (row_i == col_i).astype(F32)
    hsl = lambda h: slice(h * DK, (h + 1) * DK)
    heads = range(N_HEADS)

    gtail_t = _transpose_rows(gtail_all)
    gc_cs, kbs, decays, qs, kq, k_ts = {}, {}, {}, {}, {}, {}
    for h in heads:
        gc_cs[h] = gc_all[:, N_HEADS + h:N_HEADS + h + 1]
        gc_r = gc_all_t[N_HEADS + h:N_HEADS + h + 1, :]
        decays[h] = jnp.exp(jnp.where(causal, gc_cs[h] - gc_r, -jnp.inf))
        k = rows2d(k_s, hsl(h))
        kbs[h] = k * beta_all[:, h:h + 1]
        qs[h] = rows2d(q_s, hsl(h))
        k_ts[h] = _transpose_rows(k)
        kq[h] = _sdot(_stack(kbs[h], qs[h]), k_ts[h])

    a_mat = {h: jnp.where(strict, kq[h][:R] * decays[h], 0.0) for h in heads}
    t_inv = {h: eye - a_mat[h] for h in heads}
    if C > 2:
        xs = {h: _sdot(a_mat[h], a_mat[h]) for h in heads}
        m = 2
        while 2 * m < C:
            prod = {h: _sdot(_stack(xs[h], t_inv[h]), xs[h]) for h in heads}
            t_inv = {h: t_inv[h] + prod[h][R:] for h in heads}
            xs = {h: prod[h][:R] for h in heads}
            m *= 2
        t_inv = {h: t_inv[h] + _sdot(t_inv[h], xs[h]) for h in heads}

    for h in heads:
        egc = jnp.exp(gc_cs[h])
        vb = rows2d(v_s, hsl(h)) * beta_all[:, h:h + 1]
        uw = _sdot(t_inv[h], jnp.concatenate([vb, kbs[h] * egc], axis=1))
        u, w, att, qg = uw[:, :DV], uw[:, DV:], kq[h][R:] * decays[h], qs[h] * egc
        k_til_t = k_ts[h] * jnp.exp(gtail_t[N_HEADS + h:N_HEADS + h + 1, :])
        wss, qss = [], []
        for b in range(R // C):
            bs = slice(b * C, (b + 1) * C)
            wq = _sdot(_stack(w[bs], qg[bs]), sd_ref[b, h])
            wss.append(wq[:C])
            qss.append(wq[C:])
        v_new = u - jnp.concatenate(wss, axis=0)
        o = jnp.concatenate(qss, axis=0) + _sdot(att, v_new)
        for b in range(R // C):
            bs = slice(b * C, (b + 1) * C)
            e_b = jnp.exp(gc_cs[h][(b + 1) * C - 1:(b + 1) * C, :])
            sd_ref[b, h] = sd_ref[b, h] * e_b + _sdot(k_til_t[:, bs], v_new[bs])
        mix_s[:, :, hsl(h)] = (_rms(o, onorm_ref[...]) * _silu(rows2d(z_ref, hsl(h)))).reshape(R // C, C, DV)


def _mixer_kernel(h1_ref, qkv_ref, z_ref, cin_ref, ba_ref, sd0_ref, sq0_ref, sc0_ref,
                  qw_ref, alog_ref, dtb_ref, onorm_ref, cw_ref, cb_ref, cnorm_ref, wout_ref,
                  h2_ref, sd_ref, sq_ref, sc_ref,
                  extq_ref, extc_ref, shc_ref, q_s, k_s, v_s, mix_s,
                  *, G, T, C, NT):
    R = G * T
    t = pl.program_id(1)

    @pl.when(t == 0)
    def _():
        extq_ref[:, QKV_PAD - (SHORT_CONV - 1):QKV_PAD, :] = sq0_ref[...]
        extc_ref[:, CONV_PAD - (CONV_WIDTH - 1):CONV_PAD, :] = sc0_ref[...]
        sd_ref[...] = sd0_ref[...]

    extq_ref[0, QKV_PAD:QKV_PAD + T, :] = qkv_ref[...]
    extc_ref[0, CONV_PAD:CONV_PAD + T, :] = cin_ref[...]

    qk_scale = DK ** -0.5
    rbq = min(2 * SUBLANES, T)
    for g in range(G):
        for r0 in range(0, T, rbq):
            base = QKV_PAD - (SHORT_CONV - 1) + r0
            read = lambda j, i, g=g, base=base: extq_ref[g, pl.ds(base + j + SUBLANES * i, SUBLANES), :]
            s = _silu(_tap_sum(read, qw_ref, SHORT_CONV, rbq))
            rows = slice(g * T + r0, g * T + r0 + rbq)
            for h in range(N_HEADS):
                qh = s[:, h * DK:(h + 1) * DK]
                kh = s[:, DK_TOT + h * DK:DK_TOT + (h + 1) * DK]
                q_s[rows, h * DK:(h + 1) * DK] = qh * (lax.rsqrt(jnp.sum(qh * qh, -1, keepdims=True) + EPS) * qk_scale)
                k_s[rows, h * DK:(h + 1) * DK] = kh * lax.rsqrt(jnp.sum(kh * kh, -1, keepdims=True) + EPS)
            v_s[rows, :] = s[:, 2 * DK_TOT:]
    sq_ref[...] = extq_ref[:, T + QKV_PAD - (SHORT_CONV - 1):T + QKV_PAD, :]

    first = CONV_PAD - (CONV_WIDTH - 1)
    sh_rows = T + CONV_PAD - SUBLANES
    for g in range(G):
        for b in range(1, SUBLANES):
            for r0 in range(0, sh_rows, 32):
                rb = min(32, sh_rows - r0)
                shc_ref[g, b - 1, r0:r0 + rb, :] = extc_ref[g, r0 + b:r0 + b + rb, :]
    rbc = min(8 * SUBLANES, T)
    for g in range(G):
        for r0 in range(0, T, rbc):
            def read(j, i, g=g, r0=r0):
                a, b = divmod(j + first, SUBLANES)
                rows = pl.ds(r0 + SUBLANES * (a + i), SUBLANES)
                return extc_ref[g, rows, :] if b == 0 else shc_ref[g, b - 1, rows, :]
            acc = _tap_sum(read, cw_ref, CONV_WIDTH, rbc)
            mix_s[g * T + r0:g * T + r0 + rbc, DV_TOT:] = _silu(_rms(acc + cb_ref[...], cnorm_ref[...]))
    sc_ref[...] = extc_ref[:, T + CONV_PAD - (CONV_WIDTH - 1):T + CONV_PAD, :]

    if NT > 1:
        extq_ref[:, 0:QKV_PAD, :] = extq_ref[:, T:T + QKV_PAD, :]
        extc_ref[:, 0:CONV_PAD, :] = extc_ref[:, T:T + CONV_PAD, :]

    beta_all, g_all, gc_all = _gates(ba_ref[...], alog_ref, dtb_ref, C)
    _delta_chain(q_s, k_s, v_s, z_ref, sd_ref, mix_s, onorm_ref, beta_all, gc_all, _transpose_rows(gc_all), R=R, C=C)

    h2_ref[...] = h1_ref[...] + _wdot(mix_s[...], wout_ref[...])


def _sample_mixer_kernel(h1_ref, qkv_ref, z_ref, cin_ref, ba_ref, sd0_ref, sq0_ref, sc0_ref,
                         qw_ref, alog_ref, dtb_ref, onorm_ref, cw_ref, cb_ref, cnorm_ref, wout_ref,
                         h2_ref, sd_ref, sq_ref, sc_ref, q_s, k_s, v_s, mix_s, *, G, T):
    R = G * T
    n_q, n_c = SHORT_CONV - 1, CONV_WIDTH - 1
    sd_ref[...] = sd0_ref[...]
    hsl = lambda h: slice(h * DK, (h + 1) * DK)
    qk_scale = DK ** -0.5
    for gi in range(G // SUBLANES):
        gs = slice(gi * SUBLANES, (gi + 1) * SUBLANES)
        qkv_row = lambda r: sq0_ref[r, gs, :] if r < n_q else qkv_ref[gs, r - n_q, :]
        cin_row = lambda r: sc0_ref[r, gs, :] if r < n_c else cin_ref[gs, r - n_c, :]

        for t in range(T):
            acc = None
            for j in range(SHORT_CONV):
                tap = qkv_row(t + j) * qw_ref[j]
                acc = tap if acc is None else acc + tap
            s = _silu(acc)
            for h in range(N_HEADS):
                qh = s[:, hsl(h)]
                kh = s[:, DK_TOT + h * DK:DK_TOT + (h + 1) * DK]
                q_s[gs, t, hsl(h)] = qh * (lax.rsqrt(jnp.sum(qh * qh, -1, keepdims=True) + EPS) * qk_scale)
                k_s[gs, t, hsl(h)] = kh * lax.rsqrt(jnp.sum(kh * kh, -1, keepdims=True) + EPS)
            v_s[gs, t, :] = s[:, 2 * DK_TOT:]

        accs = [None] * T
        for r in range(n_c + T):
            row = cin_row(r)
            for t in range(max(0, r - n_c), min(T, r + 1)):
                tap = row * cw_ref[r - t]
                accs[t] = tap if accs[t] is None else accs[t] + tap
        for t in range(T):
            mix_s[gs, t, DV_TOT:] = _silu(_rms(accs[t] + cb_ref[...], cnorm_ref[...]))

    for r in range(n_q):
        sq_ref[r] = sq0_ref[r + T] if r + T < n_q else qkv_ref[:, r + T - n_q, :]
    for r in range(n_c):
        sc_ref[r] = sc0_ref[r + T] if r + T < n_c else cin_ref[:, r + T - n_c, :]

    beta_all, g_all, gc_all = _gates(ba_ref[...].reshape(R, LANES), alog_ref, dtb_ref, T)
    gtail_all = _block_cumsum(g_all, T, reverse=True) - g_all
    _delta_blocks(q_s, k_s, v_s, z_ref, sd_ref, mix_s, onorm_ref, beta_all, gc_all, _transpose_rows(gc_all), gtail_all,
                  R=R, C=T)

    mix = _wdot(mix_s[...].reshape(R, DV_TOT + C_CONV), wout_ref[...])
    h2_ref[...] = h1_ref[...].reshape(R, D_MODEL) + mix


def _sample_mixer(rows, sd0, sq0, sc0, prm, *, first_row, T, G):
    B = sd0.shape[1]
    R = G * T
    rows = [a.reshape(a.shape[0] // T, T, a.shape[1]) for a in rows]
    seq = lambda a: pl.BlockSpec((G, T, a.shape[2]), lambda b: (first_row // R + b, 0, 0))
    tm = lambda a: pl.BlockSpec((a.shape[0], G, a.shape[2]), lambda b: (0, b, 0))
    sd_spec = pl.BlockSpec((None, G) + sd0.shape[2:], lambda b: (0, b, 0, 0, 0))
    cst = lambda a: pl.BlockSpec(a.shape, lambda b: (0,) * a.ndim)
    return pl.pallas_call(
        functools.partial(_sample_mixer_kernel, G=G, T=T),
        grid=(B // G,),
        in_specs=[seq(a) for a in rows] + [sd_spec, tm(sq0), tm(sc0)] + [cst(a) for a in prm],
        out_specs=[pl.BlockSpec((R, D_MODEL), lambda b: (b, 0)), sd_spec, tm(sq0), tm(sc0)],
        out_shape=[jax.ShapeDtypeStruct((B * T, D_MODEL), F32), jax.ShapeDtypeStruct(sd0.shape, F32),
                   jax.ShapeDtypeStruct(sq0.shape, F32), jax.ShapeDtypeStruct(sc0.shape, F32)],
        scratch_shapes=[pltpu.VMEM((G, T, DK_TOT), F32), pltpu.VMEM((G, T, DK_TOT), F32),
                        pltpu.VMEM((G, T, DV_TOT), F32), pltpu.VMEM((G, T, DV_TOT + C_CONV), F32)],
        compiler_params=pltpu.CompilerParams(dimension_semantics=("arbitrary",), vmem_limit_bytes=VMEM_LIMIT),
        name="sample_mixer",
    )(*rows, sd0, sq0, sc0, *prm)


def _mixer(rows, sd0, sq0, sc0, prm, *, first_row, n_seq, seq_len, T, C):
    B, L = n_seq, seq_len
    G, NB, NT = 1, B, L // T
    R = G * T
    seq = lambda a: pl.BlockSpec((T, a.shape[1]), lambda b, t: (first_row // T + b * NT + t, 0))
    st = lambda shape: pl.BlockSpec((None, G) + shape, lambda b, t: (0,) * (2 + len(shape)))
    out_st = lambda shape: pl.BlockSpec((None, G) + shape, lambda b, t: (0, b) + (0,) * len(shape))
    cst = lambda a: pl.BlockSpec(a.shape, lambda b, t: (0,) * a.ndim)
    sd_shape, sq_shape, sc_shape = (N_HEADS, DK, DV), (SHORT_CONV - 1, QKV_DIM), (CONV_WIDTH - 1, C_CONV)
    kern = functools.partial(_mixer_kernel, G=G, T=T, C=C, NT=NT)
    return pl.pallas_call(
        kern,
        grid=(NB, NT),
        in_specs=[seq(a) for a in rows] + [st(sd_shape), st(sq_shape), st(sc_shape)] + [cst(a) for a in prm],
        out_specs=[pl.BlockSpec((T, D_MODEL), lambda b, t: (b * NT + t, 0)),
                   out_st(sd_shape), out_st(sq_shape), out_st(sc_shape)],
        out_shape=[jax.ShapeDtypeStruct((B * L, D_MODEL), F32), jax.ShapeDtypeStruct((1, B) + sd_shape, F32),
                   jax.ShapeDtypeStruct((1, B) + sq_shape, F32), jax.ShapeDtypeStruct((1, B) + sc_shape, F32)],
        scratch_shapes=[pltpu.VMEM((G, QKV_PAD + T, QKV_DIM), F32), pltpu.VMEM((G, CONV_PAD + T, C_CONV), F32),
                        pltpu.VMEM((G, SUBLANES - 1, T + CONV_PAD - SUBLANES, C_CONV), F32),
                        pltpu.VMEM((R, DK_TOT), F32), pltpu.VMEM((R, DK_TOT), F32), pltpu.VMEM((R, DV_TOT), F32),
                        pltpu.VMEM((R, DV_TOT + C_CONV), F32)],
        compiler_params=pltpu.CompilerParams(dimension_semantics=("arbitrary", "arbitrary"),
                                             vmem_limit_bytes=VMEM_LIMIT),
        name="mixer",
    )(*rows, sd0, sq0, sc0, *prm)


def kernel(x_prompt, x_sample, state_delta, state_qkv_conv, state_conv, meta_tokens, ffn1_norm, ffn1_w1, ffn1_w3,
           ffn1_w2, mix_norm, w_in, qkv_conv_w, a_log, dt_bias, o_norm, conv_w, conv_b, conv_norm, w_out, ffn2_norm,
           ffn2_w1, ffn2_w3, ffn2_w2, final_norm):
    bp, seq_len, _ = x_prompt.shape
    bs, dec_len, _ = x_sample.shape
    assert state_delta.shape[0] == 1
    row = lambda v: v.reshape(1, -1).astype(F32)
    mat32 = lambda w: w.reshape(w.shape[1:]).astype(F32)

    w_qkvz, w_glu, w_ba = _split_w_in(jnp.transpose(w_in.reshape(w_in.shape[1:])))
    ffn1_w = (mat32(ffn1_w1), mat32(ffn1_w3), mat32(ffn1_w2))
    ffn2_w = (mat32(ffn2_w1), mat32(ffn2_w3), mat32(ffn2_w2))
    f1 = (row(ffn1_norm), row(mix_norm), w_qkvz, w_glu, w_ba)
    f2 = (row(ffn2_norm), row(final_norm))
    lane_row = lambda v: jnp.zeros((1, LANES), F32).at[0, N_HEADS:2 * N_HEADS].set(v.reshape(-1).astype(F32))
    taps8 = lambda w: jnp.broadcast_to(w.astype(F32).reshape(w.shape[1], 1, w.shape[2]),
                                       (w.shape[1], SUBLANES, w.shape[2]))
    prm = (taps8(qkv_conv_w), lane_row(a_log), lane_row(dt_bias), row(o_norm),
           taps8(conv_w), row(conv_b), row(conv_norm), w_out.reshape(w_out.shape[1:]).astype(BF16))

    n_prompt, n_sample = bp * seq_len, bs * dec_len
    pad = (-(n_sample + N_META)) % FRONT_ROWS
    tail = jnp.concatenate([x_sample.reshape(n_sample, D_MODEL), meta_tokens.astype(F32),
                            jnp.zeros((pad, D_MODEL), F32)], axis=0)
    rows = _front(x_prompt.reshape(n_prompt, D_MODEL), tail, ffn1_w, f1, FRONT_ROWS)

    zeros = lambda *s: jnp.zeros(s, F32)
    _, sd_m, sq_m, sc_m = _mixer(rows, zeros(1, 1, N_HEADS, DK, DV), zeros(1, 1, SHORT_CONV - 1, QKV_DIM),
                                 zeros(1, 1, CONV_WIDTH - 1, C_CONV), prm, first_row=n_prompt + n_sample,
                                 n_seq=1, seq_len=N_META, T=N_META, C=N_META)

    h2p, sd_p, sq_p, sc_p = _mixer(rows, sd_m, sq_m, sc_m, prm, first_row=0, n_seq=bp, seq_len=seq_len,
                                   T=MIX_ROWS, C=CHUNK)

    time_major = lambda a: jnp.transpose(a.reshape(a.shape[1:]), (1, 0, 2))
    seq_major = lambda a: jnp.transpose(a, (1, 0, 2))[None]
    h2s, sd_s, sq_s, sc_s = _sample_mixer(rows, state_delta, time_major(state_qkv_conv), time_major(state_conv), prm,
                                          first_row=n_prompt, T=dec_len, G=SAMPLE_GROUP)

    y_prompt, y_sample = _back(h2p, h2s, ffn2_w, f2, BACK_ROWS)
    return (y_prompt.reshape(bp, seq_len, D_MODEL), y_sample.reshape(bs, dec_len, D_MODEL),
            sd_p, sq_p, sc_p, sd_s, seq_major(sq_s), seq_major(sc_s))
```

```python
import functools

import jax
import jax.numpy as jnp
from jax import lax
from jax.experimental import pallas as pl
from jax.experimental.pallas import tpu as pltpu

D_MODEL = 1024
D_FF = 2816
N_HEADS = 4
DK = 128
DV = 128
DK_TOT = N_HEADS * DK
DV_TOT = N_HEADS * DV
QKV_DIM = 2 * DK_TOT + DV_TOT
C_CONV = 512
SHORT_CONV = 4
CONV_WIDTH = 31
CHUNK = 64
N_META = 16
EPS = 1e-6

LANES = 128
SUBLANES = 8
QKV_PAD = SUBLANES
CONV_PAD = 4 * SUBLANES
FRONT_ROWS = 256
BACK_ROWS = 512
MIX_ROWS = 512
SAMPLE_GROUP = 16
N_CAST = 8
VMEM_LIMIT = 56 * 1024 * 1024

F32 = jnp.float32
BF16 = jnp.bfloat16


def _rms(x, g):
    return x * lax.rsqrt(jnp.mean(x * x, axis=-1, keepdims=True) + EPS) * g


def _silu(x):
    return x * jax.nn.sigmoid(x)


def _wdot(x, w):
    return jnp.dot(x.astype(BF16), w, preferred_element_type=F32)


def _sdot(a, b):
    return jnp.dot(a, b, preferred_element_type=F32)


def _stack(*xs):
    return jnp.concatenate(xs, axis=0)


def _ffn_half(h, g_ref, w1_ref, w3_ref, w2_ref):
    u = _rms(h, g_ref[...]).astype(BF16)
    a = jnp.dot(u, w1_ref[...], preferred_element_type=F32)
    b = jnp.dot(u, w3_ref[...], preferred_element_type=F32)
    hid = (_silu(a) * b).astype(BF16)
    return h + 0.5 * jnp.dot(hid, w2_ref[...], preferred_element_type=F32)


def _cast_ffn_chunk(s, srcs, dsts):
    for src, dst in zip(srcs, dsts):
        rc = src.shape[0]
        dst[pl.ds(pl.multiple_of(s * rc, rc), rc), :] = src[...].astype(BF16)


def _front_kernel(xa_ref, xb_ref, w1f_ref, w3f_ref, w2f_ref, g1_ref, gm_ref, wqkvz_ref, wglu_ref, wba_ref,
                  h1_ref, qkv_ref, z_ref, cin_ref, ba_ref, w1_ref, w3_ref, w2_ref, *, na):
    s = pl.program_id(0)

    @pl.when(s < N_CAST)
    def _():
        _cast_ffn_chunk(s, (w1f_ref, w3f_ref, w2f_ref), (w1_ref, w3_ref, w2_ref))

    @pl.when(s >= N_CAST)
    def _():
        x = jnp.where(s - N_CAST < na, xa_ref[...], xb_ref[...])
        h1 = _ffn_half(x, g1_ref, w1_ref, w3_ref, w2_ref)
        h1_ref[...] = h1
        u = _rms(h1, gm_ref[...]).astype(BF16)
        p = jnp.dot(u, wqkvz_ref[...], preferred_element_type=F32)
        qkv_ref[...] = p[:, :QKV_DIM]
        z_ref[...] = p[:, QKV_DIM:]
        glu = jnp.dot(u, wglu_ref[...], preferred_element_type=F32)
        cin_ref[...] = glu[:, :C_CONV] * jax.nn.sigmoid(glu[:, C_CONV:])
        ba_ref[...] = jnp.dot(u, wba_ref[...], preferred_element_type=F32)


def _back_kernel(ha_ref, hb_ref, w1f_ref, w3f_ref, w2f_ref, g2_ref, gf_ref, ya_ref, yb_ref,
                 w1_ref, w3_ref, w2_ref, *, na):
    s = pl.program_id(0)

    @pl.when(s < N_CAST)
    def _():
        _cast_ffn_chunk(s, (w1f_ref, w3f_ref, w2f_ref), (w1_ref, w3_ref, w2_ref))

    def run(h_ref, y_ref):
        y_ref[...] = _rms(_ffn_half(h_ref[...], g2_ref, w1_ref, w3_ref, w2_ref), gf_ref[...])

    pl.when((s >= N_CAST) & (s - N_CAST < na))(functools.partial(run, ha_ref, ya_ref))
    pl.when(s - N_CAST >= na)(functools.partial(run, hb_ref, yb_ref))


def _const_spec(shape):
    nd = len(shape)
    return pl.BlockSpec(shape, lambda *_: (0,) * nd, pipeline_mode=pl.Buffered(1))


def _two_source_specs(rows, na):
    first = pl.BlockSpec((rows, D_MODEL), lambda s: (jnp.clip(s - N_CAST, 0, na - 1), 0))
    second = pl.BlockSpec((rows, D_MODEL), lambda s: (jnp.maximum(s - N_CAST - na, 0), 0))
    return first, second


def _ffn_weight_specs(ws):
    chunk = lambda w: pl.BlockSpec((w.shape[0] // N_CAST, w.shape[1]), lambda s: (jnp.minimum(s, N_CAST - 1), 0))
    return [chunk(w) for w in ws], [pltpu.VMEM(w.shape, BF16) for w in ws]


def _front(xa, xb, ffn_w, prm, rows):
    na, nb = xa.shape[0] // rows, xb.shape[0] // rows
    n = xa.shape[0] + xb.shape[0]
    widths = (D_MODEL, QKV_DIM, DV_TOT, C_CONV, LANES)
    w_specs, w_scratch = _ffn_weight_specs(ffn_w)
    return pl.pallas_call(
        functools.partial(_front_kernel, na=na),
        grid=(N_CAST + na + nb,),
        in_specs=list(_two_source_specs(rows, na)) + w_specs + [_const_spec(a.shape) for a in prm],
        out_specs=[pl.BlockSpec((rows, w), lambda s: (jnp.maximum(s - N_CAST, 0), 0)) for w in widths],
        out_shape=[jax.ShapeDtypeStruct((n, w), F32) for w in widths],
        scratch_shapes=w_scratch,
        compiler_params=pltpu.CompilerParams(dimension_semantics=("arbitrary",), vmem_limit_bytes=VMEM_LIMIT),
        name="front",
    )(xa, xb, *ffn_w, *prm)


def _back(ha, hb, ffn_w, prm, rows):
    na, nb = ha.shape[0] // rows, hb.shape[0] // rows
    w_specs, w_scratch = _ffn_weight_specs(ffn_w)
    return pl.pallas_call(
        functools.partial(_back_kernel, na=na),
        grid=(N_CAST + na + nb,),
        in_specs=list(_two_source_specs(rows, na)) + w_specs + [_const_spec(a.shape) for a in prm],
        out_specs=list(_two_source_specs(rows, na)),
        out_shape=[jax.ShapeDtypeStruct(ha.shape, F32), jax.ShapeDtypeStruct(hb.shape, F32)],
        scratch_shapes=w_scratch,
        compiler_params=pltpu.CompilerParams(dimension_semantics=("arbitrary",), vmem_limit_bytes=VMEM_LIMIT),
        name="back",
    )(ha, hb, *ffn_w, *prm)


def _split_w_in_kernel(wt_ref, qkvz_ref, glu_ref, ba_ref):
    n_gate = 2 * N_HEADS
    glu0 = QKV_DIM + DV_TOT + n_gate
    step = 2 * LANES
    for c in range(0, QKV_DIM + DV_TOT, step):
        qkvz_ref[:, c:c + step] = wt_ref[c:c + step, :].T.astype(BF16)
    for c in range(0, 2 * C_CONV, step):
        glu_ref[:, c:c + step] = wt_ref[glu0 + c:glu0 + c + step, :].T.astype(BF16)
    ba = jnp.concatenate([wt_ref[QKV_DIM + DV_TOT:glu0, :], jnp.zeros((LANES - n_gate, D_MODEL), F32)], axis=0)
    ba_ref[...] = ba.T.astype(BF16)


def _split_w_in(wt):
    n, d = wt.shape
    widths = (QKV_DIM + DV_TOT, 2 * C_CONV, LANES)
    return pl.pallas_call(
        _split_w_in_kernel,
        grid=(1,),
        in_specs=[_const_spec(wt.shape)],
        out_specs=[pl.BlockSpec((d, wd), lambda i: (0, 0)) for wd in widths],
        out_shape=[jax.ShapeDtypeStruct((d, wd), BF16) for wd in widths],
        compiler_params=pltpu.CompilerParams(vmem_limit_bytes=VMEM_LIMIT),
        name="split_w_in",
    )(wt)


def _block_cumsum(x, block, reverse=False):
    rows = x.shape[0]
    r = lax.broadcasted_iota(jnp.int32, x.shape, 0) % block
    s = 1
    while s < block:
        if reverse:
            x = x + jnp.where(r < block - s, pltpu.roll(x, rows - s, axis=0), 0.0)
        else:
            x = x + jnp.where(r >= s, pltpu.roll(x, s, axis=0), 0.0)
        s *= 2
    return x


def _tap_sum(read, w_ref, n_taps, rows):
    accs = [None] * (rows // SUBLANES)
    for j in range(n_taps):
        w8 = w_ref[j]
        for i in range(len(accs)):
            tap = read(j, i) * w8
            accs[i] = tap if accs[i] is None else accs[i] + tap
    return accs[0] if len(accs) == 1 else jnp.concatenate(accs, axis=0)


def _transpose_rows(x):
    rows = x.shape[0]
    pad = (-rows) % LANES
    if pad:
        x = jnp.concatenate([x, jnp.zeros((pad, x.shape[1]), x.dtype)], axis=0)
    return x.T[:, :rows]


def _gates(ba, alog_ref, dtb_ref, block):
    beta_all = jax.nn.sigmoid(ba)
    xg = ba + dtb_ref[...]
    g_all = -jnp.exp(alog_ref[...]) * (jnp.maximum(xg, 0.0) + jnp.log1p(jnp.exp(-jnp.abs(xg))))
    return beta_all, g_all, _block_cumsum(g_all, block)


def _lane_heads(cols, offset, n_lanes):
    rows = cols.shape[0]
    lane_head = lax.broadcasted_iota(jnp.int32, (rows, n_lanes), 1) // (n_lanes // N_HEADS)
    out = jnp.broadcast_to(cols[:, offset + N_HEADS - 1:offset + N_HEADS], (rows, n_lanes))
    for h in range(N_HEADS - 2, -1, -1):
        out = jnp.where(lane_head == h, cols[:, offset + h:offset + h + 1], out)
    return out


def _head_blocks(x):
    rows, n = x.shape
    lane_head = lax.broadcasted_iota(jnp.int32, (rows, n), 1) // (n // N_HEADS)
    return jnp.concatenate([jnp.where(lane_head == h, x, 0.0) for h in range(N_HEADS)], axis=0)


def _delta_chain(q_s, k_s, v_s, z_ref, sd_ref, mix_s, onorm_ref, beta_all, gc_all, gc_all_t, anchors, *, R, C):
    HL = N_HEADS * C
    row_i = lax.broadcasted_iota(jnp.int32, (C, HL), 0)
    col_i = lax.broadcasted_iota(jnp.int32, (C, HL), 1) % C
    eye_p, causal_p, strict_p = row_i == col_i, row_i >= col_i, row_i > col_i
    hsl = lambda h: slice(h * DK, (h + 1) * DK)
    nt_dims = (((1,), (1,)), ((), ()))
    chunks = list(range(0, R, C))
    k_t_all = [_transpose_rows(k_s[:, hsl(h)]) for h in range(N_HEADS)]

    kbs, egcs, decays, a_mat, atts = {}, {}, {}, {}, {}
    for c0 in chunks:
        rs = slice(c0, c0 + C)
        gcol = _lane_heads(gc_all[rs], N_HEADS, HL)
        grow = jnp.sum(jnp.where(eye_p, gcol, 0.0), axis=0, keepdims=True)
        decays[c0] = jnp.exp(jnp.where(causal_p, gcol - grow, -jnp.inf))
        beta = beta_all[rs]
        egcs[c0] = jnp.exp(gc_all[rs])
        k = k_s[rs, :]
        kbs[c0] = jnp.concatenate([k[:, hsl(h)] * beta[:, h:h + 1] for h in range(N_HEADS)], axis=1)
        kq = lax.dot_general(_stack(kbs[c0], q_s[rs, :]), _head_blocks(k), nt_dims,
                             preferred_element_type=F32)
        a_mat[c0] = jnp.where(strict_p, kq[:C] * decays[c0], 0.0)
        atts[c0] = kq[C:] * decays[c0]

    eye = eye_p.astype(F32)
    t_inv = {c0: eye - a_mat[c0] for c0 in chunks}
    if C > 2:
        xs = {c0: _sdot(a_mat[c0], _head_blocks(a_mat[c0])) for c0 in chunks}
        m = 2
        while 2 * m < C:
            prod = {c0: _sdot(_stack(xs[c0], t_inv[c0]), _head_blocks(xs[c0])) for c0 in chunks}
            t_inv = {c0: t_inv[c0] + prod[c0][C:] for c0 in chunks}
            xs = {c0: prod[c0][:C] for c0 in chunks}
            m *= 2
        t_inv = {c0: t_inv[c0] + _sdot(t_inv[c0], _head_blocks(xs[c0])) for c0 in chunks}

    o_loc, q_til, n_loc, kws, e_last = {}, {}, {}, {}, {}
    for c0 in chunks:
        rs = slice(c0, c0 + C)
        beta, egc = beta_all[rs], egcs[c0]
        per_head = lambda x, cols, off: jnp.concatenate(
            [x[:, hsl(h)] * cols[:, off + h:off + h + 1] for h in range(N_HEADS)], axis=1)
        vb = per_head(v_s[rs, :], beta, 0)
        kbg = per_head(kbs[c0], egc, N_HEADS)
        uw = _sdot(t_inv[c0], jnp.concatenate([_head_blocks(vb), _head_blocks(kbg)], axis=1))
        u, w = uw[:, :DV_TOT], uw[:, DV_TOT:]
        au = _sdot(atts[c0], jnp.concatenate([_head_blocks(u), _head_blocks(w)], axis=1))
        o_loc[c0] = au[:, :DV_TOT]
        q_til[c0] = per_head(q_s[rs, :], egc, N_HEADS) - au[:, DV_TOT:]
        for h in range(N_HEADS):
            g_last = gc_all[c0 + C - 1:c0 + C, N_HEADS + h:N_HEADS + h + 1]
            k_til_t = k_t_all[h][:, c0:c0 + C] * jnp.exp(g_last - gc_all_t[N_HEADS + h:N_HEADS + h + 1, c0:c0 + C])
            kuw = _sdot(k_til_t, jnp.concatenate([u[:, hsl(h)], w[:, hsl(h)]], axis=1))
            n_loc[c0, h], kws[c0, h], e_last[c0, h] = kuw[:, :DV], kuw[:, DV:], jnp.exp(g_last)

    zero = jnp.zeros((DK, DV), F32)
    for c0 in chunks:
        rs = slice(c0, c0 + C)
        for h0 in range(0, N_HEADS, 2):
            h1 = h0 + 1
            s0, s1 = sd_ref[0, h0], sd_ref[0, h1]
            s_pair = jnp.concatenate([jnp.concatenate([s0, zero], axis=1), jnp.concatenate([zero, s1], axis=1)], axis=0)
            q_pair = q_til[c0][:, h0 * DK:(h1 + 1) * DK] + anchors[c0]
            lhs = _stack(jnp.concatenate([kws[c0, h0], kws[c0, h1]], axis=1), q_pair)
            prod = _sdot(lhs, s_pair)
            for i, (h, s_old) in enumerate(((h0, s0), (h1, s1))):
                o = prod[DK:, i * DV:(i + 1) * DV] + o_loc[c0][:, hsl(h)]
                sd_ref[0, h] = s_old * e_last[c0, h] - prod[:DK, i * DV:(i + 1) * DV] + n_loc[c0, h]
                mix_s[rs, hsl(h)] = _rms(o, onorm_ref[...]) * _silu(z_ref[rs, hsl(h)])


def _delta_blocks(q_s, k_s, v_s, z_ref, sd_ref, mix_s, onorm_ref, beta_all, gc_all, gc_all_t, gtail_all, *, R, C):
    rows2d = lambda ref, cols: ref[:, :, cols].reshape(R, cols.stop - cols.start)
    row_i = lax.broadcasted_iota(jnp.int32, (R, R), 0)
    col_i = lax.broadcasted_iota(jnp.int32, (R, R), 1)
    same = (row_i // C) == (col_i // C)
    causal = same & (row_i >= col_i)
    strict = same & (row_i > col_i)
    eye = (row_i == col_i).astype(F32)
    hsl = lambda h: slice(h * DK, (h + 1) * DK)
    heads = range(N_HEADS)

    gtail_t = _transpose_rows(gtail_all)
    gc_cs, kbs, decays, qs, kq, k_ts = {}, {}, {}, {}, {}, {}
    for h in heads:
        gc_cs[h] = gc_all[:, N_HEADS + h:N_HEADS + h + 1]
        gc_r = gc_all_t[N_HEADS + h:N_HEADS + h + 1, :]
        decays[h] = jnp.exp(jnp.where(causal, gc_cs[h] - gc_r, -jnp.inf))
        k = rows2d(k_s, hsl(h))
        kbs[h] = k * beta_all[:, h:h + 1]
        qs[h] = rows2d(q_s, hsl(h))
        k_ts[h] = _transpose_rows(k)
        kq[h] = _sdot(_stack(kbs[h], qs[h]), k_ts[h])

    a_mat = {h: jnp.where(strict, kq[h][:R] * decays[h], 0.0) for h in heads}
    t_inv = {h: eye - a_mat[h] for h in heads}
    if C > 2:
        xs = {h: _sdot(a_mat[h], a_mat[h]) for h in heads}
        m = 2
        while 2 * m < C:
            prod = {h: _sdot(_stack(xs[h], t_inv[h]), xs[h]) for h in heads}
            t_inv = {h: t_inv[h] + prod[h][R:] for h in heads}
            xs = {h: prod[h][:R] for h in heads}
            m *= 2
        t_inv = {h: t_inv[h] + _sdot(t_inv[h], xs[h]) for h in heads}

    for h in heads:
        egc = jnp.exp(gc_cs[h])
        vb = rows2d(v_s, hsl(h)) * beta_all[:, h:h + 1]
        uw = _sdot(t_inv[h], jnp.concatenate([vb, kbs[h] * egc], axis=1))
        u, w, att, qg = uw[:, :DV], uw[:, DV:], kq[h][R:] * decays[h], qs[h] * egc
        k_til_t = k_ts[h] * jnp.exp(gtail_t[N_HEADS + h:N_HEADS + h + 1, :])
        wss, qss = [], []
        for b in range(R // C):
            bs = slice(b * C, (b + 1) * C)
            wq = _sdot(_stack(w[bs], qg[bs]), sd_ref[b, h])
            wss.append(wq[:C])
            qss.append(wq[C:])
        v_new = u - jnp.concatenate(wss, axis=0)
        o = jnp.concatenate(qss, axis=0) + _sdot(att, v_new)
        for b in range(R // C):
            bs = slice(b * C, (b + 1) * C)
            e_b = jnp.exp(gc_cs[h][(b + 1) * C - 1:(b + 1) * C, :])
            sd_ref[b, h] = sd_ref[b, h] * e_b + _sdot(k_til_t[:, bs], v_new[bs])
        mix_s[:, :, hsl(h)] = (_rms(o, onorm_ref[...]) * _silu(rows2d(z_ref, hsl(h)))).reshape(R // C, C, DV)


def _mixer_kernel(h1_ref, qkv_ref, z_ref, cin_ref, ba_ref, sd0_ref, sq0_ref, sc0_ref,
                  qw_ref, alog_ref, dtb_ref, onorm_ref, cw_ref, cb_ref, cnorm_ref, wout_ref,
                  h2_ref, sd_ref, sq_ref, sc_ref,
                  extq_ref, extc_ref, shc_ref, q_s, k_s, v_s, mix_s,
                  *, G, T, C, NT):
    R = G * T
    t = pl.program_id(1)

    @pl.when(t == 0)
    def _():
        extq_ref[:, QKV_PAD - (SHORT_CONV - 1):QKV_PAD, :] = sq0_ref[...]
        extc_ref[:, CONV_PAD - (CONV_WIDTH - 1):CONV_PAD, :] = sc0_ref[...]
        sd_ref[...] = sd0_ref[...]

    extq_ref[0, QKV_PAD:QKV_PAD + T, :] = qkv_ref[...]
    extc_ref[0, CONV_PAD:CONV_PAD + T, :] = cin_ref[...]

    qk_scale = DK ** -0.5
    rbq = min(2 * SUBLANES, T)
    for g in range(G):
        for r0 in range(0, T, rbq):
            base = QKV_PAD - (SHORT_CONV - 1) + r0
            read = lambda j, i, g=g, base=base: extq_ref[g, pl.ds(base + j + SUBLANES * i, SUBLANES), :]
            s = _silu(_tap_sum(read, qw_ref, SHORT_CONV, rbq))
            rows = slice(g * T + r0, g * T + r0 + rbq)
            for h in range(N_HEADS):
                qh = s[:, h * DK:(h + 1) * DK]
                kh = s[:, DK_TOT + h * DK:DK_TOT + (h + 1) * DK]
                q_s[rows, h * DK:(h + 1) * DK] = qh * (lax.rsqrt(jnp.sum(qh * qh, -1, keepdims=True) + EPS) * qk_scale)
                k_s[rows, h * DK:(h + 1) * DK] = kh * lax.rsqrt(jnp.sum(kh * kh, -1, keepdims=True) + EPS)
            v_s[rows, :] = s[:, 2 * DK_TOT:]
    sq_ref[...] = extq_ref[:, T + QKV_PAD - (SHORT_CONV - 1):T + QKV_PAD, :]

    first = CONV_PAD - (CONV_WIDTH - 1)
    sh_rows = T + CONV_PAD - SUBLANES
    for g in range(G):
        for b in range(1, SUBLANES):
            for r0 in range(0, sh_rows, 32):
                rb = min(32, sh_rows - r0)
                shc_ref[g, b - 1, r0:r0 + rb, :] = extc_ref[g, r0 + b:r0 + b + rb, :]
    rbc = min(C, T)
    anchors = {}
    for g in range(G):
        for r0 in range(0, T, rbc):
            def read(j, i, g=g, r0=r0):
                a, b = divmod(j + first, SUBLANES)
                rows = pl.ds(r0 + SUBLANES * (a + i), SUBLANES)
                return extc_ref[g, rows, :] if b == 0 else shc_ref[g, b - 1, rows, :]
            acc = _tap_sum(read, cw_ref, CONV_WIDTH, rbc)
            c_out = _silu(_rms(acc + cb_ref[...], cnorm_ref[...]))
            mix_s[g * T + r0:g * T + r0 + rbc, DV_TOT:] = c_out
            row0 = c_out[0:1, 0:2 * DK]
            anchors[g * T + r0] = row0 - row0
    sc_ref[...] = extc_ref[:, T + CONV_PAD - (CONV_WIDTH - 1):T + CONV_PAD, :]

    if NT > 1:
        extq_ref[:, 0:QKV_PAD, :] = extq_ref[:, T:T + QKV_PAD, :]
        extc_ref[:, 0:CONV_PAD, :] = extc_ref[:, T:T + CONV_PAD, :]

    beta_all, g_all, gc_all = _gates(ba_ref[...], alog_ref, dtb_ref, C)
    _delta_chain(q_s, k_s, v_s, z_ref, sd_ref, mix_s, onorm_ref, beta_all, gc_all, _transpose_rows(gc_all), anchors,
                 R=R, C=C)

    h2_ref[...] = h1_ref[...] + _wdot(mix_s[...], wout_ref[...])


def _sample_mixer_kernel(h1_ref, qkv_ref, z_ref, cin_ref, ba_ref, sd0_ref, sq0_ref, sc0_ref,
                         qw_ref, alog_ref, dtb_ref, onorm_ref, cw_ref, cb_ref, cnorm_ref, wout_ref,
                         h2_ref, sd_ref, sq_ref, sc_ref, q_s, k_s, v_s, mix_s, *, G, T):
    R = G * T
    n_q, n_c = SHORT_CONV - 1, CONV_WIDTH - 1
    sd_ref[...] = sd0_ref[...]
    hsl = lambda h: slice(h * DK, (h + 1) * DK)
    qk_scale = DK ** -0.5
    for gi in range(G // SUBLANES):
        gs = slice(gi * SUBLANES, (gi + 1) * SUBLANES)
        qkv_row = lambda r: sq0_ref[r, gs, :] if r < n_q else qkv_ref[gs, r - n_q, :]
        cin_row = lambda r: sc0_ref[r, gs, :] if r < n_c else cin_ref[gs, r - n_c, :]

        for t in range(T):
            acc = None
            for j in range(SHORT_CONV):
                tap = qkv_row(t + j) * qw_ref[j]
                acc = tap if acc is None else acc + tap
            s = _silu(acc)
            for h in range(N_HEADS):
                qh = s[:, hsl(h)]
                kh = s[:, DK_TOT + h * DK:DK_TOT + (h + 1) * DK]
                q_s[gs, t, hsl(h)] = qh * (lax.rsqrt(jnp.sum(qh * qh, -1, keepdims=True) + EPS) * qk_scale)
                k_s[gs, t, hsl(h)] = kh * lax.rsqrt(jnp.sum(kh * kh, -1, keepdims=True) + EPS)
            v_s[gs, t, :] = s[:, 2 * DK_TOT:]

        accs = [None] * T
        for r in range(n_c + T):
            row = cin_row(r)
            for t in range(max(0, r - n_c), min(T, r + 1)):
                tap = row * cw_ref[r - t]
                accs[t] = tap if accs[t] is None else accs[t] + tap
        for t in range(T):
            mix_s[gs, t, DV_TOT:] = _silu(_rms(accs[t] + cb_ref[...], cnorm_ref[...]))

    for r in range(n_q):
        sq_ref[r] = sq0_ref[r + T] if r + T < n_q else qkv_ref[:, r + T - n_q, :]
    for r in range(n_c):
        sc_ref[r] = sc0_ref[r + T] if r + T < n_c else cin_ref[:, r + T - n_c, :]

    beta_all, g_all, gc_all = _gates(ba_ref[...].reshape(R, LANES), alog_ref, dtb_ref, T)
    gtail_all = _block_cumsum(g_all, T, reverse=True) - g_all
    _delta_blocks(q_s, k_s, v_s, z_ref, sd_ref, mix_s, onorm_ref, beta_all, gc_all, _transpose_rows(gc_all), gtail_all,
                  R=R, C=T)

    mix = _wdot(mix_s[...].reshape(R, DV_TOT + C_CONV), wout_ref[...])
    h2_ref[...] = h1_ref[...].reshape(R, D_MODEL) + mix


def _sample_mixer(rows, sd0, sq0, sc0, prm, *, first_row, T, G):
    B = sd0.shape[1]
    R = G * T
    rows = [a.reshape(a.shape[0] // T, T, a.shape[1]) for a in rows]
    seq = lambda a: pl.BlockSpec((G, T, a.shape[2]), lambda b: (first_row // R + b, 0, 0))
    tm = lambda a: pl.BlockSpec((a.shape[0], G, a.shape[2]), lambda b: (0, b, 0))
    sd_spec = pl.BlockSpec((None, G) + sd0.shape[2:], lambda b: (0, b, 0, 0, 0))
    cst = lambda a: pl.BlockSpec(a.shape, lambda b: (0,) * a.ndim)
    return pl.pallas_call(
        functools.partial(_sample_mixer_kernel, G=G, T=T),
        grid=(B // G,),
        in_specs=[seq(a) for a in rows] + [sd_spec, tm(sq0), tm(sc0)] + [cst(a) for a in prm],
        out_specs=[pl.BlockSpec((R, D_MODEL), lambda b: (b, 0)), sd_spec, tm(sq0), tm(sc0)],
        out_shape=[jax.ShapeDtypeStruct((B * T, D_MODEL), F32), jax.ShapeDtypeStruct(sd0.shape, F32),
                   jax.ShapeDtypeStruct(sq0.shape, F32), jax.ShapeDtypeStruct(sc0.shape, F32)],
        scratch_shapes=[pltpu.VMEM((G, T, DK_TOT), F32), pltpu.VMEM((G, T, DK_TOT), F32),
                        pltpu.VMEM((G, T, DV_TOT), F32), pltpu.VMEM((G, T, DV_TOT + C_CONV), F32)],
        compiler_params=pltpu.CompilerParams(dimension_semantics=("arbitrary",), vmem_limit_bytes=VMEM_LIMIT),
        name="sample_mixer",
    )(*rows, sd0, sq0, sc0, *prm)


def _mixer(rows, sd0, sq0, sc0, prm, *, first_row, n_seq, seq_len, T, C):
    B, L = n_seq, seq_len
    G, NB, NT = 1, B, L // T
    R = G * T
    seq = lambda a: pl.BlockSpec((T, a.shape[1]), lambda b, t: (first_row // T + b * NT + t, 0))
    st = lambda shape: pl.BlockSpec((None, G) + shape, lambda b, t: (0,) * (2 + len(shape)))
    out_st = lambda shape: pl.BlockSpec((None, G) + shape, lambda b, t: (0, b) + (0,) * len(shape))
    cst = lambda a: pl.BlockSpec(a.shape, lambda b, t: (0,) * a.ndim)
    sd_shape, sq_shape, sc_shape = (N_HEADS, DK, DV), (SHORT_CONV - 1, QKV_DIM), (CONV_WIDTH - 1, C_CONV)
    kern = functools.partial(_mixer_kernel, G=G, T=T, C=C, NT=NT)
    return pl.pallas_call(
        kern,
        grid=(NB, NT),
        in_specs=[seq(a) for a in rows] + [st(sd_shape), st(sq_shape), st(sc_shape)] + [cst(a) for a in prm],
        out_specs=[pl.BlockSpec((T, D_MODEL), lambda b, t: (b * NT + t, 0)),
                   out_st(sd_shape), out_st(sq_shape), out_st(sc_shape)],
        out_shape=[jax.ShapeDtypeStruct((B * L, D_MODEL), F32), jax.ShapeDtypeStruct((1, B) + sd_shape, F32),
                   jax.ShapeDtypeStruct((1, B) + sq_shape, F32), jax.ShapeDtypeStruct((1, B) + sc_shape, F32)],
        scratch_shapes=[pltpu.VMEM((G, QKV_PAD + T, QKV_DIM), F32), pltpu.VMEM((G, CONV_PAD + T, C_CONV), F32),
                        pltpu.VMEM((G, SUBLANES - 1, T + CONV_PAD - SUBLANES, C_CONV), F32),
                        pltpu.VMEM((R, DK_TOT), F32), pltpu.VMEM((R, DK_TOT), F32), pltpu.VMEM((R, DV_TOT), F32),
                        pltpu.VMEM((R, DV_TOT + C_CONV), F32)],
        compiler_params=pltpu.CompilerParams(dimension_semantics=("arbitrary", "arbitrary"),
                                             vmem_limit_bytes=VMEM_LIMIT),
        name="mixer",
    )(*rows, sd0, sq0, sc0, *prm)


def kernel(x_prompt, x_sample, state_delta, state_qkv_conv, state_conv, meta_tokens, ffn1_norm, ffn1_w1, ffn1_w3,
           ffn1_w2, mix_norm, w_in, qkv_conv_w, a_log, dt_bias, o_norm, conv_w, conv_b, conv_norm, w_out, ffn2_norm,
           ffn2_w1, ffn2_w3, ffn2_w2, final_norm):
    bp, seq_len, _ = x_prompt.shape
    bs, dec_len, _ = x_sample.shape
    assert state_delta.shape[0] == 1
    row = lambda v: v.reshape(1, -1).astype(F32)
    mat32 = lambda w: w.reshape(w.shape[1:]).astype(F32)

    w_qkvz, w_glu, w_ba = _split_w_in(jnp.transpose(w_in.reshape(w_in.shape[1:])))
    ffn1_w = (mat32(ffn1_w1), mat32(ffn1_w3), mat32(ffn1_w2))
    ffn2_w = (mat32(ffn2_w1), mat32(ffn2_w3), mat32(ffn2_w2))
    f1 = (row(ffn1_norm), row(mix_norm), w_qkvz, w_glu, w_ba)
    f2 = (row(ffn2_norm), row(final_norm))
    lane_row = lambda v: jnp.zeros((1, LANES), F32).at[0, N_HEADS:2 * N_HEADS].set(v.reshape(-1).astype(F32))
    taps8 = lambda w: jnp.broadcast_to(w.astype(F32).reshape(w.shape[1], 1, w.shape[2]),
                                       (w.shape[1], SUBLANES, w.shape[2]))
    prm = (taps8(qkv_conv_w), lane_row(a_log), lane_row(dt_bias), row(o_norm),
           taps8(conv_w), row(conv_b), row(conv_norm), w_out.reshape(w_out.shape[1:]).astype(BF16))

    n_prompt, n_sample = bp * seq_len, bs * dec_len
    pad = (-(n_sample + N_META)) % FRONT_ROWS
    tail = jnp.concatenate([x_sample.reshape(n_sample, D_MODEL), meta_tokens.astype(F32),
                            jnp.zeros((pad, D_MODEL), F32)], axis=0)
    rows = _front(x_prompt.reshape(n_prompt, D_MODEL), tail, ffn1_w, f1, FRONT_ROWS)

    zeros = lambda *s: jnp.zeros(s, F32)
    _, sd_m, sq_m, sc_m = _mixer(rows, zeros(1, 1, N_HEADS, DK, DV), zeros(1, 1, SHORT_CONV - 1, QKV_DIM),
                                 zeros(1, 1, CONV_WIDTH - 1, C_CONV), prm, first_row=n_prompt + n_sample,
                                 n_seq=1, seq_len=N_META, T=N_META, C=N_META)

    h2p, sd_p, sq_p, sc_p = _mixer(rows, sd_m, sq_m, sc_m, prm, first_row=0, n_seq=bp, seq_len=seq_len,
                                   T=MIX_ROWS, C=CHUNK)

    time_major = lambda a: jnp.transpose(a.reshape(a.shape[1:]), (1, 0, 2))
    seq_major = lambda a: jnp.transpose(a, (1, 0, 2))[None]
    h2s, sd_s, sq_s, sc_s = _sample_mixer(rows, state_delta, time_major(state_qkv_conv), time_major(state_conv), prm,
                                          first_row=n_prompt, T=dec_len, G=SAMPLE_GROUP)

    y_prompt, y_sample = _back(h2p, h2s, ffn2_w, f2, BACK_ROWS)
    return (y_prompt.reshape(bp, seq_len, D_MODEL), y_sample.reshape(bs, dec_len, D_MODEL),
            sd_p, sq_p, sc_p, sd_s, seq_major(sq_s), seq_major(sc_s))
```

```python
import functools

import jax
import jax.numpy as jnp
from jax import lax
from jax.experimental import pallas as pl
from jax.experimental.pallas import tpu as pltpu

D_MODEL = 1024
N_HEADS = 4
DK = 128
DV = 128
DK_TOT = N_HEADS * DK
DV_TOT = N_HEADS * DV
QKV_DIM = 2 * DK_TOT + DV_TOT
C_CONV = 512
SHORT_CONV = 4
CONV_WIDTH = 31
CHUNK = 64
N_META = 16
EPS = 1e-6

LANES = 128
SUBLANES = 8
QKV_PAD = SUBLANES
CONV_PAD = 4 * SUBLANES
FRONT_ROWS = 256
BACK_ROWS = 512
MIX_ROWS = 512
SAMPLE_GROUP = 16
N_CAST = 8
VMEM_LIMIT = 56 * 1024 * 1024

F32 = jnp.float32
BF16 = jnp.bfloat16


def _rms(x, g):
    return x * lax.rsqrt(jnp.mean(x * x, axis=-1, keepdims=True) + EPS) * g


def _silu(x):
    return x * jax.nn.sigmoid(x)


def _wdot(x, w):
    return jnp.dot(x.astype(BF16), w, preferred_element_type=F32)


def _sdot(a, b):
    return jnp.dot(a, b, preferred_element_type=F32)


def _stack(*xs):
    return jnp.concatenate(xs, axis=0)


def _ffn_half(h, g_ref, w1_ref, w3_ref, w2_ref):
    u = _rms(h, g_ref[...]).astype(BF16)
    a = jnp.dot(u, w1_ref[...], preferred_element_type=F32)
    b = jnp.dot(u, w3_ref[...], preferred_element_type=F32)
    hid = (_silu(a) * b).astype(BF16)
    return h + 0.5 * jnp.dot(hid, w2_ref[...], preferred_element_type=F32)


def _cast_ffn_chunk(s, srcs, dsts):
    for src, dst in zip(srcs, dsts):
        rc = src.shape[0]
        dst[pl.ds(pl.multiple_of(s * rc, rc), rc), :] = src[...].astype(BF16)


def _front_kernel(xa_ref, xb_ref, w1f_ref, w3f_ref, w2f_ref, g1_ref, gm_ref, wqkvz_ref, wglu_ref, wba_ref,
                  h1_ref, qkv_ref, z_ref, cin_ref, ba_ref, w1_ref, w3_ref, w2_ref, *, na):
    s = pl.program_id(0)

    @pl.when(s < N_CAST)
    def _():
        _cast_ffn_chunk(s, (w1f_ref, w3f_ref, w2f_ref), (w1_ref, w3_ref, w2_ref))

    def run(x_ref):
        h1 = _ffn_half(x_ref[...], g1_ref, w1_ref, w3_ref, w2_ref)
        h1_ref[...] = h1
        u = _rms(h1, gm_ref[...]).astype(BF16)
        p = jnp.dot(u, wqkvz_ref[...], preferred_element_type=F32)
        qkv_ref[...] = p[:, :QKV_DIM]
        z_ref[...] = p[:, QKV_DIM:]
        glu = jnp.dot(u, wglu_ref[...], preferred_element_type=F32)
        cin_ref[...] = glu[:, :C_CONV] * jax.nn.sigmoid(glu[:, C_CONV:])
        ba_ref[...] = jnp.dot(u, wba_ref[...], preferred_element_type=F32)

    pl.when((s >= N_CAST) & (s - N_CAST < na))(functools.partial(run, xa_ref))
    pl.when(s - N_CAST >= na)(functools.partial(run, xb_ref))


def _back_kernel(ha_ref, hb_ref, w1f_ref, w3f_ref, w2f_ref, g2_ref, gf_ref, ya_ref, yb_ref,
                 w1_ref, w3_ref, w2_ref, *, na):
    s = pl.program_id(0)

    @pl.when(s < N_CAST)
    def _():
        _cast_ffn_chunk(s, (w1f_ref, w3f_ref, w2f_ref), (w1_ref, w3_ref, w2_ref))

    def run(h_ref, y_ref):
        y_ref[...] = _rms(_ffn_half(h_ref[...], g2_ref, w1_ref, w3_ref, w2_ref), gf_ref[...])

    pl.when((s >= N_CAST) & (s - N_CAST < na))(functools.partial(run, ha_ref, ya_ref))
    pl.when(s - N_CAST >= na)(functools.partial(run, hb_ref, yb_ref))


def _const_spec(shape):
    nd = len(shape)
    return pl.BlockSpec(shape, lambda *_: (0,) * nd, pipeline_mode=pl.Buffered(1))


def _two_source_specs(rows, na):
    first = pl.BlockSpec((rows, D_MODEL), lambda s: (jnp.clip(s - N_CAST, 0, na - 1), 0))
    second = pl.BlockSpec((rows, D_MODEL), lambda s: (jnp.maximum(s - N_CAST - na, 0), 0))
    return first, second


def _ffn_weight_specs(ws):
    chunk = lambda w: pl.BlockSpec((w.shape[0] // N_CAST, w.shape[1]), lambda s: (jnp.minimum(s, N_CAST - 1), 0))
    return [chunk(w) for w in ws], [pltpu.VMEM(w.shape, BF16) for w in ws]


def _front(xa, xb, ffn_w, prm, rows):
    na, nb = xa.shape[0] // rows, xb.shape[0] // rows
    n = xa.shape[0] + xb.shape[0]
    widths = (D_MODEL, QKV_DIM, DV_TOT, C_CONV, LANES)
    w_specs, w_scratch = _ffn_weight_specs(ffn_w)
    return pl.pallas_call(
        functools.partial(_front_kernel, na=na),
        grid=(N_CAST + na + nb,),
        in_specs=list(_two_source_specs(rows, na)) + w_specs + [_const_spec(a.shape) for a in prm],
        out_specs=[pl.BlockSpec((rows, w), lambda s: (jnp.maximum(s - N_CAST, 0), 0)) for w in widths],
        out_shape=[jax.ShapeDtypeStruct((n, w), F32) for w in widths],
        scratch_shapes=w_scratch,
        compiler_params=pltpu.CompilerParams(dimension_semantics=("arbitrary",), vmem_limit_bytes=VMEM_LIMIT),
        name="front",
    )(xa, xb, *ffn_w, *prm)


def _back(ha, hb, ffn_w, prm, rows):
    na, nb = ha.shape[0] // rows, hb.shape[0] // rows
    w_specs, w_scratch = _ffn_weight_specs(ffn_w)
    return pl.pallas_call(
        functools.partial(_back_kernel, na=na),
        grid=(N_CAST + na + nb,),
        in_specs=list(_two_source_specs(rows, na)) + w_specs + [_const_spec(a.shape) for a in prm],
        out_specs=list(_two_source_specs(rows, na)),
        out_shape=[jax.ShapeDtypeStruct(ha.shape, F32), jax.ShapeDtypeStruct(hb.shape, F32)],
        scratch_shapes=w_scratch,
        compiler_params=pltpu.CompilerParams(dimension_semantics=("arbitrary",), vmem_limit_bytes=VMEM_LIMIT),
        name="back",
    )(ha, hb, *ffn_w, *prm)


def _split_w_in_kernel(wt_ref, qkvz_ref, glu_ref, ba_ref):
    n_gate = 2 * N_HEADS
    glu0 = QKV_DIM + DV_TOT + n_gate
    step = 2 * LANES
    for c in range(0, QKV_DIM + DV_TOT, step):
        qkvz_ref[:, c:c + step] = wt_ref[c:c + step, :].T.astype(BF16)
    for c in range(0, 2 * C_CONV, step):
        glu_ref[:, c:c + step] = wt_ref[glu0 + c:glu0 + c + step, :].T.astype(BF16)
    ba = jnp.concatenate([wt_ref[QKV_DIM + DV_TOT:glu0, :], jnp.zeros((LANES - n_gate, D_MODEL), F32)], axis=0)
    ba_ref[...] = ba.T.astype(BF16)


def _split_w_in(wt):
    n, d = wt.shape
    widths = (QKV_DIM + DV_TOT, 2 * C_CONV, LANES)
    return pl.pallas_call(
        _split_w_in_kernel,
        grid=(1,),
        in_specs=[_const_spec(wt.shape)],
        out_specs=[pl.BlockSpec((d, wd), lambda i: (0, 0)) for wd in widths],
        out_shape=[jax.ShapeDtypeStruct((d, wd), BF16) for wd in widths],
        compiler_params=pltpu.CompilerParams(vmem_limit_bytes=VMEM_LIMIT),
        name="split_w_in",
    )(wt)


def _block_cumsum(x, block, reverse=False):
    rows = x.shape[0]
    r = lax.broadcasted_iota(jnp.int32, x.shape, 0) % block
    s = 1
    while s < block:
        if reverse:
            x = x + jnp.where(r < block - s, pltpu.roll(x, rows - s, axis=0), 0.0)
        else:
            x = x + jnp.where(r >= s, pltpu.roll(x, s, axis=0), 0.0)
        s *= 2
    return x


def _tap_sum(read, w_ref, n_taps, rows):
    accs = [None] * (rows // SUBLANES)
    for j in range(n_taps):
        w8 = w_ref[j]
        for i in range(len(accs)):
            tap = read(j, i) * w8
            accs[i] = tap if accs[i] is None else accs[i] + tap
    return accs[0] if len(accs) == 1 else jnp.concatenate(accs, axis=0)


def _transpose_rows(x):
    rows = x.shape[0]
    pad = (-rows) % LANES
    if pad:
        x = jnp.concatenate([x, jnp.zeros((pad, x.shape[1]), x.dtype)], axis=0)
    return x.T[:, :rows]


def _gates(ba, alog_ref, dtb_ref, block):
    beta_all = jax.nn.sigmoid(ba)
    xg = ba + dtb_ref[...]
    g_all = -jnp.exp(alog_ref[...]) * (jnp.maximum(xg, 0.0) + jnp.log1p(jnp.exp(-jnp.abs(xg))))
    return beta_all, g_all, _block_cumsum(g_all, block)


def _lane_heads(cols, offset, n_lanes):
    rows = cols.shape[0]
    lane_head = lax.broadcasted_iota(jnp.int32, (rows, n_lanes), 1) // (n_lanes // N_HEADS)
    out = jnp.broadcast_to(cols[:, offset + N_HEADS - 1:offset + N_HEADS], (rows, n_lanes))
    for h in range(N_HEADS - 2, -1, -1):
        out = jnp.where(lane_head == h, cols[:, offset + h:offset + h + 1], out)
    return out


def _head_blocks(x):
    rows, n = x.shape
    lane_head = lax.broadcasted_iota(jnp.int32, (rows, n), 1) // (n // N_HEADS)
    return jnp.concatenate([jnp.where(lane_head == h, x, 0.0) for h in range(N_HEADS)], axis=0)


def _delta_chain(q_s, k_s, v_s, z_ref, sd_ref, mix_s, onorm_ref, beta_all, gc_all, gc_all_t, anchors, *, R, C):
    HL = N_HEADS * C
    row_i = lax.broadcasted_iota(jnp.int32, (C, HL), 0)
    col_i = lax.broadcasted_iota(jnp.int32, (C, HL), 1) % C
    eye_p, causal_p, strict_p = row_i == col_i, row_i >= col_i, row_i > col_i
    hsl = lambda h: slice(h * DK, (h + 1) * DK)
    nt_dims = (((1,), (1,)), ((), ()))
    chunks = list(range(0, R, C))
    k_t_all = [_transpose_rows(k_s[:, hsl(h)]) for h in range(N_HEADS)]

    kbs, egcs, decays, a_mat, atts = {}, {}, {}, {}, {}
    for c0 in chunks:
        rs = slice(c0, c0 + C)
        gcol = _lane_heads(gc_all[rs], N_HEADS, HL)
        grow = jnp.sum(jnp.where(eye_p, gcol, 0.0), axis=0, keepdims=True)
        decays[c0] = jnp.exp(jnp.where(causal_p, gcol - grow, -jnp.inf))
        beta = beta_all[rs]
        egcs[c0] = jnp.exp(gc_all[rs])
        k = k_s[rs, :]
        kbs[c0] = jnp.concatenate([k[:, hsl(h)] * beta[:, h:h + 1] for h in range(N_HEADS)], axis=1)
        kq = lax.dot_general(_stack(kbs[c0], q_s[rs, :]), _head_blocks(k), nt_dims,
                             preferred_element_type=F32)
        a_mat[c0] = jnp.where(strict_p, kq[:C] * decays[c0], 0.0)
        atts[c0] = kq[C:] * decays[c0]

    eye = eye_p.astype(F32)
    t_inv = {c0: eye - a_mat[c0] for c0 in chunks}
    if C > 2:
        xs = {c0: _sdot(a_mat[c0], _head_blocks(a_mat[c0])) for c0 in chunks}
        m = 2
        while 2 * m < C:
            prod = {c0: _sdot(_stack(xs[c0], t_inv[c0]), _head_blocks(xs[c0])) for c0 in chunks}
            t_inv = {c0: t_inv[c0] + prod[c0][C:] for c0 in chunks}
            xs = {c0: prod[c0][:C] for c0 in chunks}
            m *= 2
        t_inv = {c0: t_inv[c0] + _sdot(t_inv[c0], _head_blocks(xs[c0])) for c0 in chunks}

    o_loc, q_til, n_loc, kws, e_last = {}, {}, {}, {}, {}
    for c0 in chunks:
        rs = slice(c0, c0 + C)
        beta, egc = beta_all[rs], egcs[c0]
        per_head = lambda x, cols, off: jnp.concatenate(
            [x[:, hsl(h)] * cols[:, off + h:off + h + 1] for h in range(N_HEADS)], axis=1)
        vb = per_head(v_s[rs, :], beta, 0)
        kbg = per_head(kbs[c0], egc, N_HEADS)
        uw = _sdot(t_inv[c0], jnp.concatenate([_head_blocks(vb), _head_blocks(kbg)], axis=1))
        u, w = uw[:, :DV_TOT], uw[:, DV_TOT:]
        au = _sdot(atts[c0], jnp.concatenate([_head_blocks(u), _head_blocks(w)], axis=1))
        o_loc[c0] = au[:, :DV_TOT]
        q_til[c0] = per_head(q_s[rs, :], egc, N_HEADS) - au[:, DV_TOT:]
        for h in range(N_HEADS):
            g_last = gc_all[c0 + C - 1:c0 + C, N_HEADS + h:N_HEADS + h + 1]
            k_til_t = k_t_all[h][:, c0:c0 + C] * jnp.exp(g_last - gc_all_t[N_HEADS + h:N_HEADS + h + 1, c0:c0 + C])
            kuw = _sdot(k_til_t, jnp.concatenate([u[:, hsl(h)], w[:, hsl(h)]], axis=1))
            n_loc[c0, h], kws[c0, h], e_last[c0, h] = kuw[:, :DV], kuw[:, DV:], jnp.exp(g_last)

    zero = jnp.zeros((DK, DV), F32)
    for c0 in chunks:
        rs = slice(c0, c0 + C)
        for h0 in range(0, N_HEADS, 2):
            h1 = h0 + 1
            s0, s1 = sd_ref[0, h0], sd_ref[0, h1]
            s_pair = jnp.concatenate([jnp.concatenate([s0, zero], axis=1), jnp.concatenate([zero, s1], axis=1)], axis=0)
            q_pair = q_til[c0][:, h0 * DK:(h1 + 1) * DK] + anchors[c0]
            lhs = _stack(jnp.concatenate([kws[c0, h0], kws[c0, h1]], axis=1), q_pair)
            prod = _sdot(lhs, s_pair)
            for i, (h, s_old) in enumerate(((h0, s0), (h1, s1))):
                o = prod[DK:, i * DV:(i + 1) * DV] + o_loc[c0][:, hsl(h)]
                sd_ref[0, h] = s_old * e_last[c0, h] - prod[:DK, i * DV:(i + 1) * DV] + n_loc[c0, h]
                mix_s[rs, hsl(h)] = _rms(o, onorm_ref[...]) * _silu(z_ref[rs, hsl(h)])


def _delta_blocks(q_s, k_s, v_s, z_ref, sd_ref, mix_s, onorm_ref, beta_all, gc_all, gc_all_t, gtail_all, *, R, C):
    rows2d = lambda ref, cols: ref[:, :, cols].reshape(R, cols.stop - cols.start)
    row_i = lax.broadcasted_iota(jnp.int32, (R, R), 0)
    col_i = lax.broadcasted_iota(jnp.int32, (R, R), 1)
    same = (row_i // C) == (col_i // C)
    causal = same & (row_i >= col_i)
    strict = same & (row_i > col_i)
    eye = (row_i == col_i).astype(F32)
    hsl = lambda h: slice(h * DK, (h + 1) * DK)
    heads = range(N_HEADS)

    gtail_t = _transpose_rows(gtail_all)
    gc_cs, kbs, decays, qs, kq, k_ts = {}, {}, {}, {}, {}, {}
    for h in heads:
        gc_cs[h] = gc_all[:, N_HEADS + h:N_HEADS + h + 1]
        gc_r = gc_all_t[N_HEADS + h:N_HEADS + h + 1, :]
        decays[h] = jnp.exp(jnp.where(causal, gc_cs[h] - gc_r, -jnp.inf))
        k = rows2d(k_s, hsl(h))
        kbs[h] = k * beta_all[:, h:h + 1]
        qs[h] = rows2d(q_s, hsl(h))
        k_ts[h] = _transpose_rows(k)
        kq[h] = _sdot(_stack(kbs[h], qs[h]), k_ts[h])

    a_mat = {h: jnp.where(strict, kq[h][:R] * decays[h], 0.0) for h in heads}
    t_inv = {h: eye - a_mat[h] for h in heads}
    if C > 2:
        xs = {h: _sdot(a_mat[h], a_mat[h]) for h in heads}
        m = 2
        while 2 * m < C:
            prod = {h: _sdot(_stack(xs[h], t_inv[h]), xs[h]) for h in heads}
            t_inv = {h: t_inv[h] + prod[h][R:] for h in heads}
            xs = {h: prod[h][:R] for h in heads}
            m *= 2
        t_inv = {h: t_inv[h] + _sdot(t_inv[h], xs[h]) for h in heads}

    for h in heads:
        egc = jnp.exp(gc_cs[h])
        vb = rows2d(v_s, hsl(h)) * beta_all[:, h:h + 1]
        uw = _sdot(t_inv[h], jnp.concatenate([vb, kbs[h] * egc], axis=1))
        u, w, att, qg = uw[:, :DV], uw[:, DV:], kq[h][R:] * decays[h], qs[h] * egc
        k_til_t = k_ts[h] * jnp.exp(gtail_t[N_HEADS + h:N_HEADS + h + 1, :])
        wss, qss = [], []
        for b in range(R // C):
            bs = slice(b * C, (b + 1) * C)
            wq = _sdot(_stack(w[bs], qg[bs]), sd_ref[b, h])
            wss.append(wq[:C])
            qss.append(wq[C:])
        v_new = u - jnp.concatenate(wss, axis=0)
        o = jnp.concatenate(qss, axis=0) + _sdot(att, v_new)
        for b in range(R // C):
            bs = slice(b * C, (b + 1) * C)
            e_b = jnp.exp(gc_cs[h][(b + 1) * C - 1:(b + 1) * C, :])
            sd_ref[b, h] = sd_ref[b, h] * e_b + _sdot(k_til_t[:, bs], v_new[bs])
        mix_s[:, :, hsl(h)] = (_rms(o, onorm_ref[...]) * _silu(rows2d(z_ref, hsl(h)))).reshape(R // C, C, DV)


def _mixer_kernel(h1_ref, qkv_ref, z_ref, cin_ref, ba_ref, sd0_ref, sq0_ref, sc0_ref,
                  qw_ref, alog_ref, dtb_ref, onorm_ref, cw_ref, cb_ref, cnorm_ref, wout_ref,
                  h2_ref, sd_ref, sq_ref, sc_ref,
                  extq_ref, extc_ref, shc_ref, q_s, k_s, v_s, mix_s,
                  *, G, T, C, NT):
    R = G * T
    t = pl.program_id(1)

    @pl.when(t == 0)
    def _():
        extq_ref[:, QKV_PAD - (SHORT_CONV - 1):QKV_PAD, :] = sq0_ref[...]
        extc_ref[:, CONV_PAD - (CONV_WIDTH - 1):CONV_PAD, :] = sc0_ref[...]
        sd_ref[...] = sd0_ref[...]

    extq_ref[0, QKV_PAD:QKV_PAD + T, :] = qkv_ref[...]
    extc_ref[0, CONV_PAD:CONV_PAD + T, :] = cin_ref[...]

    qk_scale = DK ** -0.5
    rbq = min(2 * SUBLANES, T)
    for g in range(G):
        for r0 in range(0, T, rbq):
            base = QKV_PAD - (SHORT_CONV - 1) + r0
            read = lambda j, i, g=g, base=base: extq_ref[g, pl.ds(base + j + SUBLANES * i, SUBLANES), :]
            s = _silu(_tap_sum(read, qw_ref, SHORT_CONV, rbq))
            rows = slice(g * T + r0, g * T + r0 + rbq)
            for h in range(N_HEADS):
                qh = s[:, h * DK:(h + 1) * DK]
                kh = s[:, DK_TOT + h * DK:DK_TOT + (h + 1) * DK]
                q_s[rows, h * DK:(h + 1) * DK] = qh * (lax.rsqrt(jnp.sum(qh * qh, -1, keepdims=True) + EPS) * qk_scale)
                k_s[rows, h * DK:(h + 1) * DK] = kh * lax.rsqrt(jnp.sum(kh * kh, -1, keepdims=True) + EPS)
            v_s[rows, :] = s[:, 2 * DK_TOT:]
    sq_ref[...] = extq_ref[:, T + QKV_PAD - (SHORT_CONV - 1):T + QKV_PAD, :]

    first = CONV_PAD - (CONV_WIDTH - 1)
    sh_rows = T + CONV_PAD - SUBLANES
    for g in range(G):
        for b in range(1, SUBLANES):
            for r0 in range(0, sh_rows, CONV_PAD):
                rb = min(CONV_PAD, sh_rows - r0)
                shc_ref[g, b - 1, r0:r0 + rb, :] = extc_ref[g, r0 + b:r0 + b + rb, :]
    rbc = min(C, T)
    anchors = {}
    for g in range(G):
        for r0 in range(0, T, rbc):
            def read(j, i, g=g, r0=r0):
                a, b = divmod(j + first, SUBLANES)
                rows = pl.ds(r0 + SUBLANES * (a + i), SUBLANES)
                return extc_ref[g, rows, :] if b == 0 else shc_ref[g, b - 1, rows, :]
            acc = _tap_sum(read, cw_ref, CONV_WIDTH, rbc)
            c_out = _silu(_rms(acc + cb_ref[...], cnorm_ref[...]))
            mix_s[g * T + r0:g * T + r0 + rbc, DV_TOT:] = c_out
            row0 = c_out[0:1, 0:2 * DK]
            anchors[g * T + r0] = row0 - row0
    sc_ref[...] = extc_ref[:, T + CONV_PAD - (CONV_WIDTH - 1):T + CONV_PAD, :]

    if NT > 1:
        extq_ref[:, 0:QKV_PAD, :] = extq_ref[:, T:T + QKV_PAD, :]
        extc_ref[:, 0:CONV_PAD, :] = extc_ref[:, T:T + CONV_PAD, :]

    beta_all, g_all, gc_all = _gates(ba_ref[...], alog_ref, dtb_ref, C)
    _delta_chain(q_s, k_s, v_s, z_ref, sd_ref, mix_s, onorm_ref, beta_all, gc_all, _transpose_rows(gc_all), anchors,
                 R=R, C=C)

    h2_ref[...] = h1_ref[...] + _wdot(mix_s[...], wout_ref[...])


def _sample_mixer_kernel(h1_ref, qkv_ref, z_ref, cin_ref, ba_ref, sd0_ref, sq0_ref, sc0_ref,
                         qw_ref, alog_ref, dtb_ref, onorm_ref, cw_ref, cb_ref, cnorm_ref, wout_ref,
                         h2_ref, sd_ref, sq_ref, sc_ref, q_s, k_s, v_s, mix_s, *, G, T):
    R = G * T
    n_q, n_c = SHORT_CONV - 1, CONV_WIDTH - 1
    sd_ref[...] = sd0_ref[...]
    hsl = lambda h: slice(h * DK, (h + 1) * DK)
    qk_scale = DK ** -0.5
    for gi in range(G // SUBLANES):
        gs = slice(gi * SUBLANES, (gi + 1) * SUBLANES)
        qkv_row = lambda r: sq0_ref[r, gs, :] if r < n_q else qkv_ref[gs, r - n_q, :]
        cin_row = lambda r: sc0_ref[r, gs, :] if r < n_c else cin_ref[gs, r - n_c, :]

        for t in range(T):
            acc = None
            for j in range(SHORT_CONV):
                tap = qkv_row(t + j) * qw_ref[j]
                acc = tap if acc is None else acc + tap
            s = _silu(acc)
            for h in range(N_HEADS):
                qh = s[:, hsl(h)]
                kh = s[:, DK_TOT + h * DK:DK_TOT + (h + 1) * DK]
                q_s[gs, t, hsl(h)] = qh * (lax.rsqrt(jnp.sum(qh * qh, -1, keepdims=True) + EPS) * qk_scale)
                k_s[gs, t, hsl(h)] = kh * lax.rsqrt(jnp.sum(kh * kh, -1, keepdims=True) + EPS)
            v_s[gs, t, :] = s[:, 2 * DK_TOT:]

        accs = [None] * T
        for r in range(n_c + T):
            row = cin_row(r)
            for t in range(max(0, r - n_c), min(T, r + 1)):
                tap = row * cw_ref[r - t]
                accs[t] = tap if accs[t] is None else accs[t] + tap
        for t in range(T):
            mix_s[gs, t, DV_TOT:] = _silu(_rms(accs[t] + cb_ref[...], cnorm_ref[...]))

    for r in range(n_q):
        sq_ref[r] = sq0_ref[r + T] if r + T < n_q else qkv_ref[:, r + T - n_q, :]
    for r in range(n_c):
        sc_ref[r] = sc0_ref[r + T] if r + T < n_c else cin_ref[:, r + T - n_c, :]

    beta_all, g_all, gc_all = _gates(ba_ref[...].reshape(R, LANES), alog_ref, dtb_ref, T)
    gtail_all = _block_cumsum(g_all, T, reverse=True) - g_all
    _delta_blocks(q_s, k_s, v_s, z_ref, sd_ref, mix_s, onorm_ref, beta_all, gc_all, _transpose_rows(gc_all), gtail_all,
                  R=R, C=T)

    mix = _wdot(mix_s[...].reshape(R, DV_TOT + C_CONV), wout_ref[...])
    h2_ref[...] = h1_ref[...].reshape(R, D_MODEL) + mix


def _sample_mixer(rows, sd0, sq0, sc0, prm, *, first_row, T, G):
    B = sd0.shape[1]
    R = G * T
    rows = [a.reshape(a.shape[0] // T, T, a.shape[1]) for a in rows]
    seq = lambda a: pl.BlockSpec((G, T, a.shape[2]), lambda b: (first_row // R + b, 0, 0))
    tm = lambda a: pl.BlockSpec((a.shape[0], G, a.shape[2]), lambda b: (0, b, 0))
    sd_spec = pl.BlockSpec((None, G) + sd0.shape[2:], lambda b: (0, b, 0, 0, 0))
    cst = lambda a: pl.BlockSpec(a.shape, lambda b: (0,) * a.ndim)
    return pl.pallas_call(
        functools.partial(_sample_mixer_kernel, G=G, T=T),
        grid=(B // G,),
        in_specs=[seq(a) for a in rows] + [sd_spec, tm(sq0), tm(sc0)] + [cst(a) for a in prm],
        out_specs=[pl.BlockSpec((R, D_MODEL), lambda b: (b, 0)), sd_spec, tm(sq0), tm(sc0)],
        out_shape=[jax.ShapeDtypeStruct((B * T, D_MODEL), F32), jax.ShapeDtypeStruct(sd0.shape, F32),
                   jax.ShapeDtypeStruct(sq0.shape, F32), jax.ShapeDtypeStruct(sc0.shape, F32)],
        scratch_shapes=[pltpu.VMEM((G, T, DK_TOT), F32), pltpu.VMEM((G, T, DK_TOT), F32),
                        pltpu.VMEM((G, T, DV_TOT), F32), pltpu.VMEM((G, T, DV_TOT + C_CONV), F32)],
        compiler_params=pltpu.CompilerParams(dimension_semantics=("arbitrary",), vmem_limit_bytes=VMEM_LIMIT),
        name="sample_mixer",
    )(*rows, sd0, sq0, sc0, *prm)


def _mixer(rows, sd0, sq0, sc0, prm, *, first_row, n_seq, seq_len, T, C):
    B, L = n_seq, seq_len
    G, NB, NT = 1, B, L // T
    R = G * T
    seq = lambda a: pl.BlockSpec((T, a.shape[1]), lambda b, t: (first_row // T + b * NT + t, 0))
    st = lambda shape: pl.BlockSpec((None, G) + shape, lambda b, t: (0,) * (2 + len(shape)))
    out_st = lambda shape: pl.BlockSpec((None, G) + shape, lambda b, t: (0, b) + (0,) * len(shape))
    cst = lambda a: pl.BlockSpec(a.shape, lambda b, t: (0,) * a.ndim)
    sd_shape, sq_shape, sc_shape = (N_HEADS, DK, DV), (SHORT_CONV - 1, QKV_DIM), (CONV_WIDTH - 1, C_CONV)
    kern = functools.partial(_mixer_kernel, G=G, T=T, C=C, NT=NT)
    return pl.pallas_call(
        kern,
        grid=(NB, NT),
        in_specs=[seq(a) for a in rows] + [st(sd_shape), st(sq_shape), st(sc_shape)] + [cst(a) for a in prm],
        out_specs=[pl.BlockSpec((T, D_MODEL), lambda b, t: (b * NT + t, 0)),
                   out_st(sd_shape), out_st(sq_shape), out_st(sc_shape)],
        out_shape=[jax.ShapeDtypeStruct((B * L, D_MODEL), F32), jax.ShapeDtypeStruct((1, B) + sd_shape, F32),
                   jax.ShapeDtypeStruct((1, B) + sq_shape, F32), jax.ShapeDtypeStruct((1, B) + sc_shape, F32)],
        scratch_shapes=[pltpu.VMEM((G, QKV_PAD + T, QKV_DIM), F32), pltpu.VMEM((G, CONV_PAD + T, C_CONV), F32),
                        pltpu.VMEM((G, SUBLANES - 1, T + CONV_PAD - SUBLANES, C_CONV), F32),
                        pltpu.VMEM((R, DK_TOT), F32), pltpu.VMEM((R, DK_TOT), F32), pltpu.VMEM((R, DV_TOT), F32),
                        pltpu.VMEM((R, DV_TOT + C_CONV), F32)],
        compiler_params=pltpu.CompilerParams(dimension_semantics=("arbitrary", "arbitrary"),
                                             vmem_limit_bytes=VMEM_LIMIT),
        name="mixer",
    )(*rows, sd0, sq0, sc0, *prm)


def kernel(x_prompt, x_sample, state_delta, state_qkv_conv, state_conv, meta_tokens, ffn1_norm, ffn1_w1, ffn1_w3,
           ffn1_w2, mix_norm, w_in, qkv_conv_w, a_log, dt_bias, o_norm, conv_w, conv_b, conv_norm, w_out, ffn2_norm,
           ffn2_w1, ffn2_w3, ffn2_w2, final_norm):
    bp, seq_len, _ = x_prompt.shape
    bs, dec_len, _ = x_sample.shape
    assert state_delta.shape[0] == 1
    row = lambda v: v.reshape(1, -1).astype(F32)
    mat32 = lambda w: w.reshape(w.shape[1:]).astype(F32)

    w_qkvz, w_glu, w_ba = _split_w_in(jnp.transpose(w_in.reshape(w_in.shape[1:])))
    ffn1_w = (mat32(ffn1_w1), mat32(ffn1_w3), mat32(ffn1_w2))
    ffn2_w = (mat32(ffn2_w1), mat32(ffn2_w3), mat32(ffn2_w2))
    f1 = (row(ffn1_norm), row(mix_norm), w_qkvz, w_glu, w_ba)
    f2 = (row(ffn2_norm), row(final_norm))
    lane_row = lambda v: jnp.zeros((1, LANES), F32).at[0, N_HEADS:2 * N_HEADS].set(v.reshape(-1).astype(F32))
    taps8 = lambda w: jnp.broadcast_to(w.astype(F32).reshape(w.shape[1], 1, w.shape[2]),
                                       (w.shape[1], SUBLANES, w.shape[2]))
    prm = (taps8(qkv_conv_w), lane_row(a_log), lane_row(dt_bias), row(o_norm),
           taps8(conv_w), row(conv_b), row(conv_norm), w_out.reshape(w_out.shape[1:]).astype(BF16))

    n_prompt, n_sample = bp * seq_len, bs * dec_len
    pad = (-(n_sample + N_META)) % FRONT_ROWS
    tail = jnp.concatenate([x_sample.reshape(n_sample, D_MODEL), meta_tokens.astype(F32),
                            jnp.zeros((pad, D_MODEL), F32)], axis=0)
    rows = _front(x_prompt.reshape(n_prompt, D_MODEL), tail, ffn1_w, f1, FRONT_ROWS)

    zeros = lambda *s: jnp.zeros(s, F32)
    _, sd_m, sq_m, sc_m = _mixer(rows, zeros(1, 1, N_HEADS, DK, DV), zeros(1, 1, SHORT_CONV - 1, QKV_DIM),
                                 zeros(1, 1, CONV_WIDTH - 1, C_CONV), prm, first_row=n_prompt + n_sample,
                                 n_seq=1, seq_len=N_META, T=N_META, C=N_META)

    h2p, sd_p, sq_p, sc_p = _mixer(rows, sd_m, sq_m, sc_m, prm, first_row=0, n_seq=bp, seq_len=seq_len,
                                   T=MIX_ROWS, C=CHUNK)

    time_major = lambda a: jnp.transpose(a.reshape(a.shape[1:]), (1, 0, 2))
    seq_major = lambda a: jnp.transpose(a, (1, 0, 2))[None]
    h2s, sd_s, sq_s, sc_s = _sample_mixer(rows, state_delta, time_major(state_qkv_conv), time_major(state_conv), prm,
                                          first_row=n_prompt, T=dec_len, G=SAMPLE_GROUP)

    y_prompt, y_sample = _back(h2p, h2s, ffn2_w, f2, BACK_ROWS)
    return (y_prompt.reshape(bp, seq_len, D_MODEL), y_sample.reshape(bs, dec_len, D_MODEL),
            sd_p, sq_p, sc_p, sd_s, seq_major(sq_s), seq_major(sc_s))
```

```python
import functools

import jax
import jax.numpy as jnp
from jax import lax
from jax.experimental import pallas as pl
from jax.experimental.pallas import tpu as pltpu

D_MODEL = 1024
N_HEADS = 4
DK = 128
DV = 128
DK_TOT = N_HEADS * DK
DV_TOT = N_HEADS * DV
QKV_DIM = 2 * DK_TOT + DV_TOT
C_CONV = 512
SHORT_CONV = 4
CONV_WIDTH = 31
CHUNK = 64
N_META = 16
EPS = 1e-6

LANES = 128
SUBLANES = 8
QKV_PAD = SUBLANES
CONV_PAD = 4 * SUBLANES
FRONT_ROWS = 256
BACK_ROWS = 512
MIX_ROWS = 512
SAMPLE_GROUP = 16
N_CAST = 8
VMEM_LIMIT = 56 * 1024 * 1024

F32 = jnp.float32
BF16 = jnp.bfloat16


def _rms(x, g):
    return x * lax.rsqrt(jnp.mean(x * x, axis=-1, keepdims=True) + EPS) * g


def _silu(x):
    return x * jax.nn.sigmoid(x)


def _wdot(x, w):
    return jnp.dot(x.astype(BF16), w, preferred_element_type=F32)


def _sdot(a, b):
    return jnp.dot(a, b, preferred_element_type=F32)


def _stack(*xs):
    return jnp.concatenate(xs, axis=0)


def _ffn_half(h, g_ref, w1_ref, w3_ref, w2_ref):
    u = _rms(h, g_ref[...]).astype(BF16)
    a = jnp.dot(u, w1_ref[...], preferred_element_type=F32)
    b = jnp.dot(u, w3_ref[...], preferred_element_type=F32)
    hid = (_silu(a) * b).astype(BF16)
    return h + 0.5 * jnp.dot(hid, w2_ref[...], preferred_element_type=F32)


def _cast_ffn_chunk(s, srcs, dsts):
    for src, dst in zip(srcs, dsts):
        rc = src.shape[0]
        dst[pl.ds(pl.multiple_of(s * rc, rc), rc), :] = src[...].astype(BF16)


def _front_kernel(xa_ref, xb_ref, w1f_ref, w3f_ref, w2f_ref, g1_ref, gm_ref, wqkvz_ref, wglu_ref, wba_ref,
                  h1_ref, qkv_ref, z_ref, cin_ref, ba_ref, w1_ref, w3_ref, w2_ref, *, na):
    s = pl.program_id(0)

    @pl.when(s < N_CAST)
    def _():
        _cast_ffn_chunk(s, (w1f_ref, w3f_ref, w2f_ref), (w1_ref, w3_ref, w2_ref))

    def run(x_ref):
        h1 = _ffn_half(x_ref[...], g1_ref, w1_ref, w3_ref, w2_ref)
        h1_ref[...] = h1
        u = _rms(h1, gm_ref[...]).astype(BF16)
        p = jnp.dot(u, wqkvz_ref[...], preferred_element_type=F32)
        qkv_ref[...] = p[:, :QKV_DIM]
        z_ref[...] = p[:, QKV_DIM:]
        glu = jnp.dot(u, wglu_ref[...], preferred_element_type=F32)
        cin_ref[...] = glu[:, :C_CONV] * jax.nn.sigmoid(glu[:, C_CONV:])
        ba_ref[...] = jnp.dot(u, wba_ref[...], preferred_element_type=F32)

    pl.when((s >= N_CAST) & (s - N_CAST < na))(functools.partial(run, xa_ref))
    pl.when(s - N_CAST >= na)(functools.partial(run, xb_ref))


def _back_kernel(ha_ref, hb_ref, w1f_ref, w3f_ref, w2f_ref, g2_ref, gf_ref, ya_ref, yb_ref,
                 w1_ref, w3_ref, w2_ref, *, na):
    s = pl.program_id(0)

    @pl.when(s < N_CAST)
    def _():
        _cast_ffn_chunk(s, (w1f_ref, w3f_ref, w2f_ref), (w1_ref, w3_ref, w2_ref))

    def run(h_ref, y_ref):
        half = h_ref.shape[0] // 2
        for r0 in (0, half):
            rows = slice(r0, r0 + half)
            y_ref[rows, :] = _rms(_ffn_half(h_ref[rows, :], g2_ref, w1_ref, w3_ref, w2_ref), gf_ref[...])

    pl.when((s >= N_CAST) & (s - N_CAST < na))(functools.partial(run, ha_ref, ya_ref))
    pl.when(s - N_CAST >= na)(functools.partial(run, hb_ref, yb_ref))


def _const_spec(shape):
    nd = len(shape)
    return pl.BlockSpec(shape, lambda *_: (0,) * nd, pipeline_mode=pl.Buffered(1))


def _two_source_specs(rows, na):
    first = pl.BlockSpec((rows, D_MODEL), lambda s: (jnp.clip(s - N_CAST, 0, na - 1), 0))
    second = pl.BlockSpec((rows, D_MODEL), lambda s: (jnp.maximum(s - N_CAST - na, 0), 0))
    return first, second


def _ffn_weight_specs(ws):
    chunk = lambda w: pl.BlockSpec((w.shape[0] // N_CAST, w.shape[1]), lambda s: (jnp.minimum(s, N_CAST - 1), 0))
    return [chunk(w) for w in ws], [pltpu.VMEM(w.shape, BF16) for w in ws]


def _front(xa, xb, ffn_w, prm, rows):
    na, nb = xa.shape[0] // rows, xb.shape[0] // rows
    n = xa.shape[0] + xb.shape[0]
    widths = (D_MODEL, QKV_DIM, DV_TOT, C_CONV, LANES)
    w_specs, w_scratch = _ffn_weight_specs(ffn_w)
    return pl.pallas_call(
        functools.partial(_front_kernel, na=na),
        grid=(N_CAST + na + nb,),
        in_specs=list(_two_source_specs(rows, na)) + w_specs + [_const_spec(a.shape) for a in prm],
        out_specs=[pl.BlockSpec((rows, w), lambda s: (jnp.maximum(s - N_CAST, 0), 0)) for w in widths],
        out_shape=[jax.ShapeDtypeStruct((n, w), F32) for w in widths],
        scratch_shapes=w_scratch,
        compiler_params=pltpu.CompilerParams(dimension_semantics=("arbitrary",), vmem_limit_bytes=VMEM_LIMIT),
        name="front",
    )(xa, xb, *ffn_w, *prm)


def _back(ha, hb, ffn_w, prm, rows):
    na, nb = ha.shape[0] // rows, hb.shape[0] // rows
    w_specs, w_scratch = _ffn_weight_specs(ffn_w)
    return pl.pallas_call(
        functools.partial(_back_kernel, na=na),
        grid=(N_CAST + na + nb,),
        in_specs=list(_two_source_specs(rows, na)) + w_specs + [_const_spec(a.shape) for a in prm],
        out_specs=list(_two_source_specs(rows, na)),
        out_shape=[jax.ShapeDtypeStruct(ha.shape, F32), jax.ShapeDtypeStruct(hb.shape, F32)],
        scratch_shapes=w_scratch,
        compiler_params=pltpu.CompilerParams(dimension_semantics=("arbitrary",), vmem_limit_bytes=VMEM_LIMIT),
        name="back",
    )(ha, hb, *ffn_w, *prm)


def _split_w_in_kernel(wt_ref, qkvz_ref, glu_ref, ba_ref):
    n_gate = 2 * N_HEADS
    glu0 = QKV_DIM + DV_TOT + n_gate
    step = 2 * LANES
    for c in range(0, QKV_DIM + DV_TOT, step):
        qkvz_ref[:, c:c + step] = wt_ref[c:c + step, :].T.astype(BF16)
    for c in range(0, 2 * C_CONV, step):
        glu_ref[:, c:c + step] = wt_ref[glu0 + c:glu0 + c + step, :].T.astype(BF16)
    ba = jnp.concatenate([wt_ref[QKV_DIM + DV_TOT:glu0, :], jnp.zeros((LANES - n_gate, D_MODEL), F32)], axis=0)
    ba_ref[...] = ba.T.astype(BF16)


def _split_w_in(wt):
    n, d = wt.shape
    widths = (QKV_DIM + DV_TOT, 2 * C_CONV, LANES)
    return pl.pallas_call(
        _split_w_in_kernel,
        grid=(1,),
        in_specs=[_const_spec(wt.shape)],
        out_specs=[pl.BlockSpec((d, wd), lambda i: (0, 0)) for wd in widths],
        out_shape=[jax.ShapeDtypeStruct((d, wd), BF16) for wd in widths],
        compiler_params=pltpu.CompilerParams(vmem_limit_bytes=VMEM_LIMIT),
        name="split_w_in",
    )(wt)


def _block_cumsum(x, block, reverse=False):
    rows = x.shape[0]
    r = lax.broadcasted_iota(jnp.int32, x.shape, 0) % block
    s = 1
    while s < block:
        if reverse:
            x = x + jnp.where(r < block - s, pltpu.roll(x, rows - s, axis=0), 0.0)
        else:
            x = x + jnp.where(r >= s, pltpu.roll(x, s, axis=0), 0.0)
        s *= 2
    return x


def _tap_sum(read, w_ref, n_taps, rows):
    accs = [None] * (rows // SUBLANES)
    for j in range(n_taps):
        w8 = w_ref[j]
        for i in range(len(accs)):
            tap = read(j, i) * w8
            accs[i] = tap if accs[i] is None else accs[i] + tap
    return accs[0] if len(accs) == 1 else jnp.concatenate(accs, axis=0)


def _transpose_rows(x):
    rows = x.shape[0]
    pad = (-rows) % LANES
    if pad:
        x = jnp.concatenate([x, jnp.zeros((pad, x.shape[1]), x.dtype)], axis=0)
    return x.T[:, :rows]


def _gates(ba, alog_ref, dtb_ref, block):
    beta_all = jax.nn.sigmoid(ba)
    xg = ba + dtb_ref[...]
    g_all = -jnp.exp(alog_ref[...]) * (jnp.maximum(xg, 0.0) + jnp.log1p(jnp.exp(-jnp.abs(xg))))
    return beta_all, g_all, _block_cumsum(g_all, block)


def _lane_heads(cols, offset, n_lanes):
    rows = cols.shape[0]
    lane_head = lax.broadcasted_iota(jnp.int32, (rows, n_lanes), 1) // (n_lanes // N_HEADS)
    out = jnp.broadcast_to(cols[:, offset + N_HEADS - 1:offset + N_HEADS], (rows, n_lanes))
    for h in range(N_HEADS - 2, -1, -1):
        out = jnp.where(lane_head == h, cols[:, offset + h:offset + h + 1], out)
    return out


def _head_blocks(x):
    rows, n = x.shape
    lane_head = lax.broadcasted_iota(jnp.int32, (rows, n), 1) // (n // N_HEADS)
    return jnp.concatenate([jnp.where(lane_head == h, x, 0.0) for h in range(N_HEADS)], axis=0)


def _delta_chain(q_s, k_s, v_s, z_ref, sd_ref, mix_s, onorm_ref, beta_all, gc_all, gc_all_t, anchors, *, R, C):
    HL = N_HEADS * C
    row_i = lax.broadcasted_iota(jnp.int32, (C, HL), 0)
    col_i = lax.broadcasted_iota(jnp.int32, (C, HL), 1) % C
    eye_p, causal_p, strict_p = row_i == col_i, row_i >= col_i, row_i > col_i
    hsl = lambda h: slice(h * DK, (h + 1) * DK)
    nt_dims = (((1,), (1,)), ((), ()))
    chunks = list(range(0, R, C))
    k_t_all = [_transpose_rows(k_s[:, hsl(h)]) for h in range(N_HEADS)]

    kbs, egcs, decays, a_mat, atts = {}, {}, {}, {}, {}
    for c0 in chunks:
        rs = slice(c0, c0 + C)
        gcol = _lane_heads(gc_all[rs], N_HEADS, HL)
        grow = jnp.sum(jnp.where(eye_p, gcol, 0.0), axis=0, keepdims=True)
        decays[c0] = jnp.exp(jnp.where(causal_p, gcol - grow, -jnp.inf))
        beta = beta_all[rs]
        egcs[c0] = jnp.exp(gc_all[rs])
        k = k_s[rs, :]
        kbs[c0] = jnp.concatenate([k[:, hsl(h)] * beta[:, h:h + 1] for h in range(N_HEADS)], axis=1)
        kq = lax.dot_general(_stack(kbs[c0], q_s[rs, :]), _head_blocks(k), nt_dims,
                             preferred_element_type=F32)
        a_mat[c0] = jnp.where(strict_p, kq[:C] * decays[c0], 0.0)
        atts[c0] = kq[C:] * decays[c0]

    eye = eye_p.astype(F32)
    t_inv = {c0: eye - a_mat[c0] for c0 in chunks}
    if C > 2:
        xs = {c0: _sdot(a_mat[c0], _head_blocks(a_mat[c0])) for c0 in chunks}
        m = 2
        while 2 * m < C:
            prod = {c0: _sdot(_stack(xs[c0], t_inv[c0]), _head_blocks(xs[c0])) for c0 in chunks}
            t_inv = {c0: t_inv[c0] + prod[c0][C:] for c0 in chunks}
            xs = {c0: prod[c0][:C] for c0 in chunks}
            m *= 2
        t_inv = {c0: t_inv[c0] + _sdot(t_inv[c0], _head_blocks(xs[c0])) for c0 in chunks}

    o_loc, q_til, n_loc, kws, e_last = {}, {}, {}, {}, {}
    for c0 in chunks:
        rs = slice(c0, c0 + C)
        beta, egc = beta_all[rs], egcs[c0]
        per_head = lambda x, cols, off: jnp.concatenate(
            [x[:, hsl(h)] * cols[:, off + h:off + h + 1] for h in range(N_HEADS)], axis=1)
        vb = per_head(v_s[rs, :], beta, 0)
        kbg = per_head(kbs[c0], egc, N_HEADS)
        uw = _sdot(t_inv[c0], jnp.concatenate([_head_blocks(vb), _head_blocks(kbg)], axis=1))
        u, w = uw[:, :DV_TOT], uw[:, DV_TOT:]
        au = _sdot(atts[c0], jnp.concatenate([_head_blocks(u), _head_blocks(w)], axis=1))
        o_loc[c0] = au[:, :DV_TOT]
        q_til[c0] = per_head(q_s[rs, :], egc, N_HEADS) - au[:, DV_TOT:]
        for h in range(N_HEADS):
            g_last = gc_all[c0 + C - 1:c0 + C, N_HEADS + h:N_HEADS + h + 1]
            k_til_t = k_t_all[h][:, c0:c0 + C] * jnp.exp(g_last - gc_all_t[N_HEADS + h:N_HEADS + h + 1, c0:c0 + C])
            kuw = _sdot(k_til_t, jnp.concatenate([u[:, hsl(h)], w[:, hsl(h)]], axis=1))
            n_loc[c0, h], kws[c0, h], e_last[c0, h] = kuw[:, :DV], kuw[:, DV:], jnp.exp(g_last)

    zero = jnp.zeros((DK, DV), F32)
    for c0 in chunks:
        rs = slice(c0, c0 + C)
        for h0 in range(0, N_HEADS, 2):
            h1 = h0 + 1
            s0, s1 = sd_ref[0, h0], sd_ref[0, h1]
            s_pair = jnp.concatenate([jnp.concatenate([s0, zero], axis=1), jnp.concatenate([zero, s1], axis=1)], axis=0)
            q_pair = q_til[c0][:, h0 * DK:(h1 + 1) * DK] + anchors[c0]
            lhs = _stack(jnp.concatenate([kws[c0, h0], kws[c0, h1]], axis=1), q_pair)
            prod = _sdot(lhs, s_pair)
            for i, (h, s_old) in enumerate(((h0, s0), (h1, s1))):
                o = prod[DK:, i * DV:(i + 1) * DV] + o_loc[c0][:, hsl(h)]
                sd_ref[0, h] = s_old * e_last[c0, h] - prod[:DK, i * DV:(i + 1) * DV] + n_loc[c0, h]
                mix_s[rs, hsl(h)] = _rms(o, onorm_ref[...]) * _silu(z_ref[rs, hsl(h)])


def _delta_blocks(q_s, k_s, v_s, z_ref, sd_ref, mix_s, onorm_ref, beta_all, gc_all, gc_all_t, gtail_all, *, R, C):
    rows2d = lambda ref, cols: ref[:, :, cols].reshape(R, cols.stop - cols.start)
    row_i = lax.broadcasted_iota(jnp.int32, (R, R), 0)
    col_i = lax.broadcasted_iota(jnp.int32, (R, R), 1)
    same = (row_i // C) == (col_i // C)
    causal = same & (row_i >= col_i)
    strict = same & (row_i > col_i)
    eye = (row_i == col_i).astype(F32)
    hsl = lambda h: slice(h * DK, (h + 1) * DK)
    heads = range(N_HEADS)

    gtail_t = _transpose_rows(gtail_all)
    gc_cs, kbs, decays, qs, kq, k_ts = {}, {}, {}, {}, {}, {}
    for h in heads:
        gc_cs[h] = gc_all[:, N_HEADS + h:N_HEADS + h + 1]
        gc_r = gc_all_t[N_HEADS + h:N_HEADS + h + 1, :]
        decays[h] = jnp.exp(jnp.where(causal, gc_cs[h] - gc_r, -jnp.inf))
        k = rows2d(k_s, hsl(h))
        kbs[h] = k * beta_all[:, h:h + 1]
        qs[h] = rows2d(q_s, hsl(h))
        k_ts[h] = _transpose_rows(k)
        kq[h] = _sdot(_stack(kbs[h], qs[h]), k_ts[h])

    a_mat = {h: jnp.where(strict, kq[h][:R] * decays[h], 0.0) for h in heads}
    t_inv = {h: eye - a_mat[h] for h in heads}
    if C > 2:
        xs = {h: _sdot(a_mat[h], a_mat[h]) for h in heads}
        m = 2
        while 2 * m < C:
            prod = {h: _sdot(_stack(xs[h], t_inv[h]), xs[h]) for h in heads}
            t_inv = {h: t_inv[h] + prod[h][R:] for h in heads}
            xs = {h: prod[h][:R] for h in heads}
            m *= 2
        t_inv = {h: t_inv[h] + _sdot(t_inv[h], xs[h]) for h in heads}

    for h in heads:
        egc = jnp.exp(gc_cs[h])
        vb = rows2d(v_s, hsl(h)) * beta_all[:, h:h + 1]
        uw = _sdot(t_inv[h], jnp.concatenate([vb, kbs[h] * egc], axis=1))
        u, w, att, qg = uw[:, :DV], uw[:, DV:], kq[h][R:] * decays[h], qs[h] * egc
        k_til_t = k_ts[h] * jnp.exp(gtail_t[N_HEADS + h:N_HEADS + h + 1, :])
        wss, qss = [], []
        for b in range(R // C):
            bs = slice(b * C, (b + 1) * C)
            wq = _sdot(_stack(w[bs], qg[bs]), sd_ref[b, h])
            wss.append(wq[:C])
            qss.append(wq[C:])
        v_new = u - jnp.concatenate(wss, axis=0)
        o = jnp.concatenate(qss, axis=0) + _sdot(att, v_new)
        for b in range(R // C):
            bs = slice(b * C, (b + 1) * C)
            e_b = jnp.exp(gc_cs[h][(b + 1) * C - 1:(b + 1) * C, :])
            sd_ref[b, h] = sd_ref[b, h] * e_b + _sdot(k_til_t[:, bs], v_new[bs])
        mix_s[:, :, hsl(h)] = (_rms(o, onorm_ref[...]) * _silu(rows2d(z_ref, hsl(h)))).reshape(R // C, C, DV)


def _mixer_kernel(h1_ref, qkv_ref, z_ref, cin_ref, ba_ref, sd0_ref, sq0_ref, sc0_ref,
                  qw_ref, alog_ref, dtb_ref, onorm_ref, cw_ref, cb_ref, cnorm_ref, wout_ref,
                  h2_ref, sd_ref, sq_ref, sc_ref,
                  extq_ref, extc_ref, shc_ref, q_s, k_s, v_s, mix_s,
                  *, G, T, C, NT):
    R = G * T
    t = pl.program_id(1)

    @pl.when(t == 0)
    def _():
        extq_ref[:, QKV_PAD - (SHORT_CONV - 1):QKV_PAD, :] = sq0_ref[...]
        extc_ref[:, CONV_PAD - (CONV_WIDTH - 1):CONV_PAD, :] = sc0_ref[...]
        sd_ref[...] = sd0_ref[...]

    extq_ref[0, QKV_PAD:QKV_PAD + T, :] = qkv_ref[...]
    extc_ref[0, CONV_PAD:CONV_PAD + T, :] = cin_ref[...]

    qk_scale = DK ** -0.5
    rbq = min(2 * SUBLANES, T)
    for g in range(G):
        for r0 in range(0, T, rbq):
            base = QKV_PAD - (SHORT_CONV - 1) + r0
            read = lambda j, i, g=g, base=base: extq_ref[g, pl.ds(base + j + SUBLANES * i, SUBLANES), :]
            s = _silu(_tap_sum(read, qw_ref, SHORT_CONV, rbq))
            rows = slice(g * T + r0, g * T + r0 + rbq)
            for h in range(N_HEADS):
                qh = s[:, h * DK:(h + 1) * DK]
                kh = s[:, DK_TOT + h * DK:DK_TOT + (h + 1) * DK]
                q_s[rows, h * DK:(h + 1) * DK] = qh * (lax.rsqrt(jnp.sum(qh * qh, -1, keepdims=True) + EPS) * qk_scale)
                k_s[rows, h * DK:(h + 1) * DK] = kh * lax.rsqrt(jnp.sum(kh * kh, -1, keepdims=True) + EPS)
            v_s[rows, :] = s[:, 2 * DK_TOT:]
    sq_ref[...] = extq_ref[:, T + QKV_PAD - (SHORT_CONV - 1):T + QKV_PAD, :]

    first = CONV_PAD - (CONV_WIDTH - 1)
    sh_rows = T + CONV_PAD - SUBLANES
    for g in range(G):
        for b in range(1, SUBLANES):
            for r0 in range(0, sh_rows, CONV_PAD):
                rb = min(CONV_PAD, sh_rows - r0)
                shc_ref[g, b - 1, r0:r0 + rb, :] = extc_ref[g, r0 + b:r0 + b + rb, :]
    rbc = min(C, T)
    anchors = {}
    for g in range(G):
        for r0 in range(0, T, rbc):
            def read(j, i, g=g, r0=r0):
                a, b = divmod(j + first, SUBLANES)
                rows = pl.ds(r0 + SUBLANES * (a + i), SUBLANES)
                return extc_ref[g, rows, :] if b == 0 else shc_ref[g, b - 1, rows, :]
            acc = _tap_sum(read, cw_ref, CONV_WIDTH, rbc)
            c_out = _silu(_rms(acc + cb_ref[...], cnorm_ref[...]))
            mix_s[g * T + r0:g * T + r0 + rbc, DV_TOT:] = c_out
            row0 = c_out[0:1, 0:2 * DK]
            anchors[g * T + r0] = row0 - row0
    sc_ref[...] = extc_ref[:, T + CONV_PAD - (CONV_WIDTH - 1):T + CONV_PAD, :]

    if NT > 1:
        extq_ref[:, 0:QKV_PAD, :] = extq_ref[:, T:T + QKV_PAD, :]
        extc_ref[:, 0:CONV_PAD, :] = extc_ref[:, T:T + CONV_PAD, :]

    beta_all, g_all, gc_all = _gates(ba_ref[...], alog_ref, dtb_ref, C)
    _delta_chain(q_s, k_s, v_s, z_ref, sd_ref, mix_s, onorm_ref, beta_all, gc_all, _transpose_rows(gc_all), anchors,
                 R=R, C=C)

    h2_ref[...] = h1_ref[...] + _wdot(mix_s[...], wout_ref[...])


def _sample_mixer_kernel(h1_ref, qkv_ref, z_ref, cin_ref, ba_ref, sd0_ref, sq0_ref, sc0_ref,
                         qw_ref, alog_ref, dtb_ref, onorm_ref, cw_ref, cb_ref, cnorm_ref, wout_ref,
                         h2_ref, sd_ref, sq_ref, sc_ref, q_s, k_s, v_s, mix_s, *, G, T):
    R = G * T
    n_q, n_c = SHORT_CONV - 1, CONV_WIDTH - 1
    sd_ref[...] = sd0_ref[...]
    hsl = lambda h: slice(h * DK, (h + 1) * DK)
    qk_scale = DK ** -0.5
    for gi in range(G // SUBLANES):
        gs = slice(gi * SUBLANES, (gi + 1) * SUBLANES)
        qkv_row = lambda r: sq0_ref[r, gs, :] if r < n_q else qkv_ref[gs, r - n_q, :]
        cin_row = lambda r: sc0_ref[r, gs, :] if r < n_c else cin_ref[gs, r - n_c, :]

        for t in range(T):
            acc = None
            for j in range(SHORT_CONV):
                tap = qkv_row(t + j) * qw_ref[j]
                acc = tap if acc is None else acc + tap
            s = _silu(acc)
            for h in range(N_HEADS):
                qh = s[:, hsl(h)]
                kh = s[:, DK_TOT + h * DK:DK_TOT + (h + 1) * DK]
                q_s[gs, t, hsl(h)] = qh * (lax.rsqrt(jnp.sum(qh * qh, -1, keepdims=True) + EPS) * qk_scale)
                k_s[gs, t, hsl(h)] = kh * lax.rsqrt(jnp.sum(kh * kh, -1, keepdims=True) + EPS)
            v_s[gs, t, :] = s[:, 2 * DK_TOT:]

        accs = [None] * T
        for r in range(n_c + T):
            row = cin_row(r)
            for t in range(max(0, r - n_c), min(T, r + 1)):
                tap = row * cw_ref[r - t]
                accs[t] = tap if accs[t] is None else accs[t] + tap
        for t in range(T):
            mix_s[gs, t, DV_TOT:] = _silu(_rms(accs[t] + cb_ref[...], cnorm_ref[...]))

    for r in range(n_q):
        sq_ref[r] = sq0_ref[r + T] if r + T < n_q else qkv_ref[:, r + T - n_q, :]
    for r in range(n_c):
        sc_ref[r] = sc0_ref[r + T] if r + T < n_c else cin_ref[:, r + T - n_c, :]

    beta_all, g_all, gc_all = _gates(ba_ref[...].reshape(R, LANES), alog_ref, dtb_ref, T)
    gtail_all = _block_cumsum(g_all, T, reverse=True) - g_all
    _delta_blocks(q_s, k_s, v_s, z_ref, sd_ref, mix_s, onorm_ref, beta_all, gc_all, _transpose_rows(gc_all), gtail_all,
                  R=R, C=T)

    mix = _wdot(mix_s[...].reshape(R, DV_TOT + C_CONV), wout_ref[...])
    h2_ref[...] = h1_ref[...].reshape(R, D_MODEL) + mix


def _sample_mixer(rows, sd0, sq0, sc0, prm, *, first_row, T, G):
    B = sd0.shape[1]
    R = G * T
    rows = [a.reshape(a.shape[0] // T, T, a.shape[1]) for a in rows]
    seq = lambda a: pl.BlockSpec((G, T, a.shape[2]), lambda b: (first_row // R + b, 0, 0))
    tm = lambda a: pl.BlockSpec((a.shape[0], G, a.shape[2]), lambda b: (0, b, 0))
    sd_spec = pl.BlockSpec((None, G) + sd0.shape[2:], lambda b: (0, b, 0, 0, 0))
    cst = lambda a: pl.BlockSpec(a.shape, lambda b: (0,) * a.ndim)
    return pl.pallas_call(
        functools.partial(_sample_mixer_kernel, G=G, T=T),
        grid=(B // G,),
        in_specs=[seq(a) for a in rows] + [sd_spec, tm(sq0), tm(sc0)] + [cst(a) for a in prm],
        out_specs=[pl.BlockSpec((R, D_MODEL), lambda b: (b, 0)), sd_spec, tm(sq0), tm(sc0)],
        out_shape=[jax.ShapeDtypeStruct((B * T, D_MODEL), F32), jax.ShapeDtypeStruct(sd0.shape, F32),
                   jax.ShapeDtypeStruct(sq0.shape, F32), jax.ShapeDtypeStruct(sc0.shape, F32)],
        scratch_shapes=[pltpu.VMEM((G, T, DK_TOT), F32), pltpu.VMEM((G, T, DK_TOT), F32),
                        pltpu.VMEM((G, T, DV_TOT), F32), pltpu.VMEM((G, T, DV_TOT + C_CONV), F32)],
        compiler_params=pltpu.CompilerParams(dimension_semantics=("arbitrary",), vmem_limit_bytes=VMEM_LIMIT),
        name="sample_mixer",
    )(*rows, sd0, sq0, sc0, *prm)


def _mixer(rows, sd0, sq0, sc0, prm, *, first_row, n_seq, seq_len, T, C):
    B, L = n_seq, seq_len
    G, NB, NT = 1, B, L // T
    R = G * T
    seq = lambda a: pl.BlockSpec((T, a.shape[1]), lambda b, t: (first_row // T + b * NT + t, 0))
    st = lambda shape: pl.BlockSpec((None, G) + shape, lambda b, t: (0,) * (2 + len(shape)))
    out_st = lambda shape: pl.BlockSpec((None, G) + shape, lambda b, t: (0, b) + (0,) * len(shape))
    cst = lambda a: pl.BlockSpec(a.shape, lambda b, t: (0,) * a.ndim)
    sd_shape, sq_shape, sc_shape = (N_HEADS, DK, DV), (SHORT_CONV - 1, QKV_DIM), (CONV_WIDTH - 1, C_CONV)
    kern = functools.partial(_mixer_kernel, G=G, T=T, C=C, NT=NT)
    return pl.pallas_call(
        kern,
        grid=(NB, NT),
        in_specs=[seq(a) for a in rows] + [st(sd_shape), st(sq_shape), st(sc_shape)] + [cst(a) for a in prm],
        out_specs=[pl.BlockSpec((T, D_MODEL), lambda b, t: (b * NT + t, 0)),
                   out_st(sd_shape), out_st(sq_shape), out_st(sc_shape)],
        out_shape=[jax.ShapeDtypeStruct((B * L, D_MODEL), F32), jax.ShapeDtypeStruct((1, B) + sd_shape, F32),
                   jax.ShapeDtypeStruct((1, B) + sq_shape, F32), jax.ShapeDtypeStruct((1, B) + sc_shape, F32)],
        scratch_shapes=[pltpu.VMEM((G, QKV_PAD + T, QKV_DIM), F32), pltpu.VMEM((G, CONV_PAD + T, C_CONV), F32),
                        pltpu.VMEM((G, SUBLANES - 1, T + CONV_PAD - SUBLANES, C_CONV), F32),
                        pltpu.VMEM((R, DK_TOT), F32), pltpu.VMEM((R, DK_TOT), F32), pltpu.VMEM((R, DV_TOT), F32),
                        pltpu.VMEM((R, DV_TOT + C_CONV), F32)],
        compiler_params=pltpu.CompilerParams(dimension_semantics=("arbitrary", "arbitrary"),
                                             vmem_limit_bytes=VMEM_LIMIT),
        name="mixer",
    )(*rows, sd0, sq0, sc0, *prm)


def kernel(x_prompt, x_sample, state_delta, state_qkv_conv, state_conv, meta_tokens, ffn1_norm, ffn1_w1, ffn1_w3,
           ffn1_w2, mix_norm, w_in, qkv_conv_w, a_log, dt_bias, o_norm, conv_w, conv_b, conv_norm, w_out, ffn2_norm,
           ffn2_w1, ffn2_w3, ffn2_w2, final_norm):
    bp, seq_len, _ = x_prompt.shape
    bs, dec_len, _ = x_sample.shape
    assert state_delta.shape[0] == 1
    row = lambda v: v.reshape(1, -1).astype(F32)
    mat32 = lambda w: w.reshape(w.shape[1:]).astype(F32)

    w_qkvz, w_glu, w_ba = _split_w_in(jnp.transpose(w_in.reshape(w_in.shape[1:])))
    ffn1_w = (mat32(ffn1_w1), mat32(ffn1_w3), mat32(ffn1_w2))
    ffn2_w = (mat32(ffn2_w1), mat32(ffn2_w3), mat32(ffn2_w2))
    f1 = (row(ffn1_norm), row(mix_norm), w_qkvz, w_glu, w_ba)
    f2 = (row(ffn2_norm), row(final_norm))
    lane_row = lambda v: jnp.zeros((1, LANES), F32).at[0, N_HEADS:2 * N_HEADS].set(v.reshape(-1).astype(F32))
    taps8 = lambda w: jnp.broadcast_to(w.astype(F32).reshape(w.shape[1], 1, w.shape[2]),
                                       (w.shape[1], SUBLANES, w.shape[2]))
    prm = (taps8(qkv_conv_w), lane_row(a_log), lane_row(dt_bias), row(o_norm),
           taps8(conv_w), row(conv_b), row(conv_norm), w_out.reshape(w_out.shape[1:]).astype(BF16))

    n_prompt, n_sample = bp * seq_len, bs * dec_len
    pad = (-(n_sample + N_META)) % FRONT_ROWS
    tail = jnp.concatenate([x_sample.reshape(n_sample, D_MODEL), meta_tokens.astype(F32),
                            jnp.zeros((pad, D_MODEL), F32)], axis=0)
    rows = _front(x_prompt.reshape(n_prompt, D_MODEL), tail, ffn1_w, f1, FRONT_ROWS)

    zeros = lambda *s: jnp.zeros(s, F32)
    _, sd_m, sq_m, sc_m = _mixer(rows, zeros(1, 1, N_HEADS, DK, DV), zeros(1, 1, SHORT_CONV - 1, QKV_DIM),
                                 zeros(1, 1, CONV_WIDTH - 1, C_CONV), prm, first_row=n_prompt + n_sample,
                                 n_seq=1, seq_len=N_META, T=N_META, C=N_META)

    h2p, sd_p, sq_p, sc_p = _mixer(rows, sd_m, sq_m, sc_m, prm, first_row=0, n_seq=bp, seq_len=seq_len,
                                   T=MIX_ROWS, C=CHUNK)

    time_major = lambda a: jnp.transpose(a.reshape(a.shape[1:]), (1, 0, 2))
    seq_major = lambda a: jnp.transpose(a, (1, 0, 2))[None]
    h2s, sd_s, sq_s, sc_s = _sample_mixer(rows, state_delta, time_major(state_qkv_conv), time_major(state_conv), prm,
                                          first_row=n_prompt, T=dec_len, G=SAMPLE_GROUP)

    y_prompt, y_sample = _back(h2p, h2s, ffn2_w, f2, BACK_ROWS)
    return (y_prompt.reshape(bp, seq_len, D_MODEL), y_sample.reshape(bs, dec_len, D_MODEL),
            sd_p, sq_p, sc_p, sd_s, seq_major(sq_s), seq_major(sc_s))
```

```python
import functools

import jax
import jax.numpy as jnp
from jax import lax
from jax.experimental import pallas as pl
from jax.experimental.pallas import tpu as pltpu

D_MODEL = 1024
N_HEADS = 4
DK = 128
DV = 128
DK_TOT = N_HEADS * DK
DV_TOT = N_HEADS * DV
QKV_DIM = 2 * DK_TOT + DV_TOT
C_CONV = 512
SHORT_CONV = 4
CONV_WIDTH = 31
CHUNK = 64
N_META = 16
EPS = 1e-6

LANES = 128
SUBLANES = 8
QKV_PAD = SUBLANES
CONV_PAD = 4 * SUBLANES
FRONT_ROWS = 256
BACK_ROWS = 512
MIX_ROWS = 512
SAMPLE_GROUP = 16
N_CAST = 8
VMEM_LIMIT = 56 * 1024 * 1024

F32 = jnp.float32
BF16 = jnp.bfloat16


def _rms(x, g):
    return x * lax.rsqrt(jnp.mean(x * x, axis=-1, keepdims=True) + EPS) * g


def _silu(x):
    return x * jax.nn.sigmoid(x)


def _wdot(x, w):
    return jnp.dot(x.astype(BF16), w, preferred_element_type=F32)


def _sdot(a, b):
    return jnp.dot(a, b, preferred_element_type=F32)


def _stack(*xs):
    return jnp.concatenate(xs, axis=0)


def _ffn_half(h, g_ref, w1_ref, w3_ref, w2_ref):
    u = _rms(h, g_ref[...]).astype(BF16)
    a = jnp.dot(u, w1_ref[...], preferred_element_type=F32)
    b = jnp.dot(u, w3_ref[...], preferred_element_type=F32)
    hid = (_silu(a) * b).astype(BF16)
    return h + 0.5 * jnp.dot(hid, w2_ref[...], preferred_element_type=F32)


def _cast_ffn_chunk(s, srcs, dsts):
    for src, dst in zip(srcs, dsts):
        rc = src.shape[0]
        dst[pl.ds(pl.multiple_of(s * rc, rc), rc), :] = src[...].astype(BF16)


def _front_kernel(xa_ref, xb_ref, w1f_ref, w3f_ref, w2f_ref, g1_ref, gm_ref, wqkvz_ref, wglu_ref, wba_ref,
                  h1_ref, qkv_ref, z_ref, cin_ref, ba_ref, w1_ref, w3_ref, w2_ref, *, na):
    s = pl.program_id(0)

    @pl.when(s < N_CAST)
    def _():
        _cast_ffn_chunk(s, (w1f_ref, w3f_ref, w2f_ref), (w1_ref, w3_ref, w2_ref))

    def run(x_ref):
        h1 = _ffn_half(x_ref[...], g1_ref, w1_ref, w3_ref, w2_ref)
        h1_ref[...] = h1
        u = _rms(h1, gm_ref[...]).astype(BF16)
        p = jnp.dot(u, wqkvz_ref[...], preferred_element_type=F32)
        qkv_ref[...] = p[:, :QKV_DIM]
        z_ref[...] = p[:, QKV_DIM:]
        glu = jnp.dot(u, wglu_ref[...], preferred_element_type=F32)
        cin_ref[...] = glu[:, :C_CONV] * jax.nn.sigmoid(glu[:, C_CONV:])
        ba_ref[...] = jnp.dot(u, wba_ref[...], preferred_element_type=F32)

    pl.when((s >= N_CAST) & (s - N_CAST < na))(functools.partial(run, xa_ref))
    pl.when(s - N_CAST >= na)(functools.partial(run, xb_ref))


def _back_kernel(ha_ref, hb_ref, w1f_ref, w3f_ref, w2f_ref, g2_ref, gf_ref, ya_ref, yb_ref,
                 w1_ref, w3_ref, w2_ref, *, na):
    s = pl.program_id(0)

    @pl.when(s < N_CAST)
    def _():
        _cast_ffn_chunk(s, (w1f_ref, w3f_ref, w2f_ref), (w1_ref, w3_ref, w2_ref))

    def run(h_ref, y_ref):
        half = h_ref.shape[0] // 2
        for r0 in (0, half):
            rows = slice(r0, r0 + half)
            y_ref[rows, :] = _rms(_ffn_half(h_ref[rows, :], g2_ref, w1_ref, w3_ref, w2_ref), gf_ref[...])

    pl.when((s >= N_CAST) & (s - N_CAST < na))(functools.partial(run, ha_ref, ya_ref))
    pl.when(s - N_CAST >= na)(functools.partial(run, hb_ref, yb_ref))


def _const_spec(shape):
    nd = len(shape)
    return pl.BlockSpec(shape, lambda *_: (0,) * nd, pipeline_mode=pl.Buffered(1))


def _two_source_specs(rows, na):
    first = pl.BlockSpec((rows, D_MODEL), lambda s: (jnp.clip(s - N_CAST, 0, na - 1), 0))
    second = pl.BlockSpec((rows, D_MODEL), lambda s: (jnp.maximum(s - N_CAST - na, 0), 0))
    return first, second


def _ffn_weight_specs(ws):
    chunk = lambda w: pl.BlockSpec((w.shape[0] // N_CAST, w.shape[1]), lambda s: (jnp.minimum(s, N_CAST - 1), 0))
    return [chunk(w) for w in ws], [pltpu.VMEM(w.shape, BF16) for w in ws]


def _front(xa, xb, ffn_w, prm, rows):
    na, nb = xa.shape[0] // rows, xb.shape[0] // rows
    n = xa.shape[0] + xb.shape[0]
    widths = (D_MODEL, QKV_DIM, DV_TOT, C_CONV, LANES)
    w_specs, w_scratch = _ffn_weight_specs(ffn_w)
    return pl.pallas_call(
        functools.partial(_front_kernel, na=na),
        grid=(N_CAST + na + nb,),
        in_specs=list(_two_source_specs(rows, na)) + w_specs + [_const_spec(a.shape) for a in prm],
        out_specs=[pl.BlockSpec((rows, w), lambda s: (jnp.maximum(s - N_CAST, 0), 0)) for w in widths],
        out_shape=[jax.ShapeDtypeStruct((n, w), F32) for w in widths],
        scratch_shapes=w_scratch,
        compiler_params=pltpu.CompilerParams(dimension_semantics=("arbitrary",), vmem_limit_bytes=VMEM_LIMIT),
        name="front",
    )(xa, xb, *ffn_w, *prm)


def _back(ha, hb, ffn_w, prm, rows):
    na, nb = ha.shape[0] // rows, hb.shape[0] // rows
    w_specs, w_scratch = _ffn_weight_specs(ffn_w)
    return pl.pallas_call(
        functools.partial(_back_kernel, na=na),
        grid=(N_CAST + na + nb,),
        in_specs=list(_two_source_specs(rows, na)) + w_specs + [_const_spec(a.shape) for a in prm],
        out_specs=list(_two_source_specs(rows, na)),
        out_shape=[jax.ShapeDtypeStruct(ha.shape, F32), jax.ShapeDtypeStruct(hb.shape, F32)],
        scratch_shapes=w_scratch,
        compiler_params=pltpu.CompilerParams(dimension_semantics=("arbitrary",), vmem_limit_bytes=VMEM_LIMIT),
        name="back",
    )(ha, hb, *ffn_w, *prm)


def _split_w_in_kernel(wt_ref, qkvz_ref, glu_ref, ba_ref):
    n_gate = 2 * N_HEADS
    glu0 = QKV_DIM + DV_TOT + n_gate
    step = 2 * LANES
    for c in range(0, QKV_DIM + DV_TOT, step):
        qkvz_ref[:, c:c + step] = wt_ref[c:c + step, :].T.astype(BF16)
    for c in range(0, 2 * C_CONV, step):
        glu_ref[:, c:c + step] = wt_ref[glu0 + c:glu0 + c + step, :].T.astype(BF16)
    ba = jnp.concatenate([wt_ref[QKV_DIM + DV_TOT:glu0, :], jnp.zeros((LANES - n_gate, D_MODEL), F32)], axis=0)
    ba_ref[...] = ba.T.astype(BF16)


def _split_w_in(wt):
    n, d = wt.shape
    widths = (QKV_DIM + DV_TOT, 2 * C_CONV, LANES)
    return pl.pallas_call(
        _split_w_in_kernel,
        grid=(1,),
        in_specs=[_const_spec(wt.shape)],
        out_specs=[pl.BlockSpec((d, wd), lambda i: (0, 0)) for wd in widths],
        out_shape=[jax.ShapeDtypeStruct((d, wd), BF16) for wd in widths],
        compiler_params=pltpu.CompilerParams(vmem_limit_bytes=VMEM_LIMIT),
        name="split_w_in",
    )(wt)


def _block_cumsum(x, block, reverse=False):
    rows = x.shape[0]
    r = lax.broadcasted_iota(jnp.int32, x.shape, 0) % block
    s = 1
    while s < block:
        if reverse:
            x = x + jnp.where(r < block - s, pltpu.roll(x, rows - s, axis=0), 0.0)
        else:
            x = x + jnp.where(r >= s, pltpu.roll(x, s, axis=0), 0.0)
        s *= 2
    return x


def _tap_sum(read, w_ref, n_taps, rows):
    accs = [None] * (rows // SUBLANES)
    for j in range(n_taps):
        w8 = w_ref[j]
        for i in range(len(accs)):
            tap = read(j, i) * w8
            accs[i] = tap if accs[i] is None else accs[i] + tap
    return accs[0] if len(accs) == 1 else jnp.concatenate(accs, axis=0)


def _transpose_rows(x):
    rows = x.shape[0]
    pad = (-rows) % LANES
    if pad:
        x = jnp.concatenate([x, jnp.zeros((pad, x.shape[1]), x.dtype)], axis=0)
    return x.T[:, :rows]


def _gates(ba, alog_ref, dtb_ref, block):
    beta_all = jax.nn.sigmoid(ba)
    xg = ba + dtb_ref[...]
    g_all = -jnp.exp(alog_ref[...]) * (jnp.maximum(xg, 0.0) + jnp.log1p(jnp.exp(-jnp.abs(xg))))
    return beta_all, g_all, _block_cumsum(g_all, block)


def _lane_heads(cols, offset, n_lanes):
    rows = cols.shape[0]
    lane_head = lax.broadcasted_iota(jnp.int32, (rows, n_lanes), 1) // (n_lanes // N_HEADS)
    out = jnp.broadcast_to(cols[:, offset + N_HEADS - 1:offset + N_HEADS], (rows, n_lanes))
    for h in range(N_HEADS - 2, -1, -1):
        out = jnp.where(lane_head == h, cols[:, offset + h:offset + h + 1], out)
    return out


def _head_blocks(x):
    rows, n = x.shape
    lane_head = lax.broadcasted_iota(jnp.int32, (rows, n), 1) // (n // N_HEADS)
    return jnp.concatenate([jnp.where(lane_head == h, x, 0.0) for h in range(N_HEADS)], axis=0)


def _delta_chain(q_s, k_s, v_s, z_ref, sd_ref, mix_s, onorm_ref, beta_all, gc_all, gc_all_t, anchors, *, R, C):
    HL = N_HEADS * C
    row_i = lax.broadcasted_iota(jnp.int32, (C, HL), 0)
    col_i = lax.broadcasted_iota(jnp.int32, (C, HL), 1) % C
    eye_p, causal_p, strict_p = row_i == col_i, row_i >= col_i, row_i > col_i
    hsl = lambda h: slice(h * DK, (h + 1) * DK)
    nt_dims = (((1,), (1,)), ((), ()))
    chunks = list(range(0, R, C))
    k_t_all = [_transpose_rows(k_s[:, hsl(h)]) for h in range(N_HEADS)]

    kbs, egcs, decays, a_mat, atts = {}, {}, {}, {}, {}
    for c0 in chunks:
        rs = slice(c0, c0 + C)
        gcol = _lane_heads(gc_all[rs], N_HEADS, HL)
        grow = jnp.sum(jnp.where(eye_p, gcol, 0.0), axis=0, keepdims=True)
        decays[c0] = jnp.exp(jnp.where(causal_p, gcol - grow, -jnp.inf))
        beta = beta_all[rs]
        egcs[c0] = jnp.exp(gc_all[rs])
        k = k_s[rs, :]
        kbs[c0] = jnp.concatenate([k[:, hsl(h)] * beta[:, h:h + 1] for h in range(N_HEADS)], axis=1)
        kq = lax.dot_general(_stack(kbs[c0], q_s[rs, :]), _head_blocks(k), nt_dims,
                             preferred_element_type=F32)
        a_mat[c0] = jnp.where(strict_p, kq[:C] * decays[c0], 0.0)
        atts[c0] = kq[C:] * decays[c0]

    eye = eye_p.astype(F32)
    t_inv = {c0: eye - a_mat[c0] for c0 in chunks}
    if C > 2:
        xs = {c0: _sdot(a_mat[c0], _head_blocks(a_mat[c0])) for c0 in chunks}
        m = 2
        while 2 * m < C:
            prod = {c0: _sdot(_stack(xs[c0], t_inv[c0]), _head_blocks(xs[c0])) for c0 in chunks}
            t_inv = {c0: t_inv[c0] + prod[c0][C:] for c0 in chunks}
            xs = {c0: prod[c0][:C] for c0 in chunks}
            m *= 2
        t_inv = {c0: t_inv[c0] + _sdot(t_inv[c0], _head_blocks(xs[c0])) for c0 in chunks}

    o_loc, q_til, n_loc, kws, e_last = {}, {}, {}, {}, {}
    for c0 in chunks:
        rs = slice(c0, c0 + C)
        beta, egc = beta_all[rs], egcs[c0]
        per_head = lambda x, cols, off: jnp.concatenate(
            [x[:, hsl(h)] * cols[:, off + h:off + h + 1] for h in range(N_HEADS)], axis=1)
        vb = per_head(v_s[rs, :], beta, 0)
        kbg = per_head(kbs[c0], egc, N_HEADS)
        uw = _sdot(t_inv[c0], jnp.concatenate([_head_blocks(vb), _head_blocks(kbg)], axis=1))
        u, w = uw[:, :DV_TOT], uw[:, DV_TOT:]
        au = _sdot(atts[c0], jnp.concatenate([_head_blocks(u), _head_blocks(w)], axis=1))
        o_loc[c0] = au[:, :DV_TOT]
        q_til[c0] = per_head(q_s[rs, :], egc, N_HEADS) - au[:, DV_TOT:]
        for h in range(N_HEADS):
            g_last = gc_all[c0 + C - 1:c0 + C, N_HEADS + h:N_HEADS + h + 1]
            k_til_t = k_t_all[h][:, c0:c0 + C] * jnp.exp(g_last - gc_all_t[N_HEADS + h:N_HEADS + h + 1, c0:c0 + C])
            kuw = _sdot(k_til_t, jnp.concatenate([u[:, hsl(h)], w[:, hsl(h)]], axis=1))
            n_loc[c0, h], kws[c0, h], e_last[c0, h] = kuw[:, :DV], kuw[:, DV:], jnp.exp(g_last)

    zero = jnp.zeros((DK, DV), F32)
    for c0 in chunks:
        rs = slice(c0, c0 + C)
        for h0 in range(0, N_HEADS, 2):
            h1 = h0 + 1
            s0, s1 = sd_ref[0, h0], sd_ref[0, h1]
            s_pair = jnp.concatenate([jnp.concatenate([s0, zero], axis=1), jnp.concatenate([zero, s1], axis=1)], axis=0)
            q_pair = q_til[c0][:, h0 * DK:(h1 + 1) * DK] + anchors[c0]
            lhs = _stack(jnp.concatenate([kws[c0, h0], kws[c0, h1]], axis=1), q_pair)
            prod = _sdot(lhs, s_pair)
            for i, (h, s_old) in enumerate(((h0, s0), (h1, s1))):
                o = prod[DK:, i * DV:(i + 1) * DV] + o_loc[c0][:, hsl(h)]
                sd_ref[0, h] = s_old * e_last[c0, h] - prod[:DK, i * DV:(i + 1) * DV] + n_loc[c0, h]
                mix_s[rs, hsl(h)] = _rms(o, onorm_ref[...]) * _silu(z_ref[rs, hsl(h)])


def _delta_blocks(q_s, k_s, v_s, z_ref, sd_ref, mix_s, onorm_ref, beta_all, gc_all, gc_all_t, gtail_all, *, R, C):
    rows2d = lambda ref, cols: ref[:, :, cols].reshape(R, cols.stop - cols.start)
    row_i = lax.broadcasted_iota(jnp.int32, (R, R), 0)
    col_i = lax.broadcasted_iota(jnp.int32, (R, R), 1)
    same = (row_i // C) == (col_i // C)
    causal = same & (row_i >= col_i)
    strict = same & (row_i > col_i)
    eye = (row_i == col_i).astype(F32)
    hsl = lambda h: slice(h * DK, (h + 1) * DK)
    heads = range(N_HEADS)

    gtail_t = _transpose_rows(gtail_all)
    gc_cs, kbs, decays, qs, kq, k_ts = {}, {}, {}, {}, {}, {}
    for h in heads:
        gc_cs[h] = gc_all[:, N_HEADS + h:N_HEADS + h + 1]
        gc_r = gc_all_t[N_HEADS + h:N_HEADS + h + 1, :]
        decays[h] = jnp.exp(jnp.where(causal, gc_cs[h] - gc_r, -jnp.inf))
        k = rows2d(k_s, hsl(h))
        kbs[h] = k * beta_all[:, h:h + 1]
        qs[h] = rows2d(q_s, hsl(h))
        k_ts[h] = _transpose_rows(k)
        kq[h] = _sdot(_stack(kbs[h], qs[h]), k_ts[h])

    a_mat = {h: jnp.where(strict, kq[h][:R] * decays[h], 0.0) for h in heads}
    t_inv = {h: eye - a_mat[h] for h in heads}
    if C > 2:
        xs = {h: _sdot(a_mat[h], a_mat[h]) for h in heads}
        m = 2
        while 2 * m < C:
            prod = {h: _sdot(_stack(xs[h], t_inv[h]), xs[h]) for h in heads}
            t_inv = {h: t_inv[h] + prod[h][R:] for h in heads}
            xs = {h: prod[h][:R] for h in heads}
            m *= 2
        t_inv = {h: t_inv[h] + _sdot(t_inv[h], xs[h]) for h in heads}

    for h in heads:
        egc = jnp.exp(gc_cs[h])
        vb = rows2d(v_s, hsl(h)) * beta_all[:, h:h + 1]
        uw = _sdot(t_inv[h], jnp.concatenate([vb, kbs[h] * egc], axis=1))
        u, w, att, qg = uw[:, :DV], uw[:, DV:], kq[h][R:] * decays[h], qs[h] * egc
        k_til_t = k_ts[h] * jnp.exp(gtail_t[N_HEADS + h:N_HEADS + h + 1, :])
        wss, qss = [], []
        for b in range(R // C):
            bs = slice(b * C, (b + 1) * C)
            wq = _sdot(_stack(w[bs], qg[bs]), sd_ref[b, h])
            wss.append(wq[:C])
            qss.append(wq[C:])
        v_new = u - jnp.concatenate(wss, axis=0)
        o = jnp.concatenate(qss, axis=0) + _sdot(att, v_new)
        for b in range(R // C):
            bs = slice(b * C, (b + 1) * C)
            e_b = jnp.exp(gc_cs[h][(b + 1) * C - 1:(b + 1) * C, :])
            sd_ref[b, h] = sd_ref[b, h] * e_b + _sdot(k_til_t[:, bs], v_new[bs])
        mix_s[:, :, hsl(h)] = (_rms(o, onorm_ref[...]) * _silu(rows2d(z_ref, hsl(h)))).reshape(R // C, C, DV)


def _mixer_kernel(h1_ref, qkv_ref, z_ref, cin_ref, ba_ref, sd0_ref, sq0_ref, sc0_ref,
                  qw_ref, alog_ref, dtb_ref, onorm_ref, cw_ref, cb_ref, cnorm_ref, wout_ref,
                  h2_ref, sd_ref, sq_ref, sc_ref,
                  extq_ref, extc_ref, shc_ref, q_s, k_s, v_s, mix_s,
                  *, G, T, C, NT):
    R = G * T
    t = pl.program_id(1)

    @pl.when(t == 0)
    def _():
        extq_ref[:, QKV_PAD - (SHORT_CONV - 1):QKV_PAD, :] = sq0_ref[...]
        extc_ref[:, CONV_PAD - (CONV_WIDTH - 1):CONV_PAD, :] = sc0_ref[...]
        sd_ref[...] = sd0_ref[...]

    extq_ref[0, QKV_PAD:QKV_PAD + T, :] = qkv_ref[...]
    extc_ref[0, CONV_PAD:CONV_PAD + T, :] = cin_ref[...]

    qk_scale = DK ** -0.5
    rbq = min(2 * SUBLANES, T)
    for g in range(G):
        for r0 in range(0, T, rbq):
            base = QKV_PAD - (SHORT_CONV - 1) + r0
            read = lambda j, i, g=g, base=base: extq_ref[g, pl.ds(base + j + SUBLANES * i, SUBLANES), :]
            s = _silu(_tap_sum(read, qw_ref, SHORT_CONV, rbq))
            rows = slice(g * T + r0, g * T + r0 + rbq)
            for h in range(N_HEADS):
                qh = s[:, h * DK:(h + 1) * DK]
                kh = s[:, DK_TOT + h * DK:DK_TOT + (h + 1) * DK]
                q_s[rows, h * DK:(h + 1) * DK] = qh * (lax.rsqrt(jnp.sum(qh * qh, -1, keepdims=True) + EPS) * qk_scale)
                k_s[rows, h * DK:(h + 1) * DK] = kh * lax.rsqrt(jnp.sum(kh * kh, -1, keepdims=True) + EPS)
            v_s[rows, :] = s[:, 2 * DK_TOT:]
    sq_ref[...] = extq_ref[:, T + QKV_PAD - (SHORT_CONV - 1):T + QKV_PAD, :]

    first = CONV_PAD - (CONV_WIDTH - 1)
    sh_rows = T + CONV_PAD - SUBLANES
    for g in range(G):
        for b in range(1, SUBLANES):
            for r0 in range(0, sh_rows, CONV_PAD):
                rb = min(CONV_PAD, sh_rows - r0)
                shc_ref[g, b - 1, r0:r0 + rb, :] = extc_ref[g, r0 + b:r0 + b + rb, :]
    rbc = min(C, T)
    anchors = {}
    for g in range(G):
        for r0 in range(0, T, rbc):
            def read(j, i, g=g, r0=r0):
                a, b = divmod(j + first, SUBLANES)
                rows = pl.ds(r0 + SUBLANES * (a + i), SUBLANES)
                return extc_ref[g, rows, :] if b == 0 else shc_ref[g, b - 1, rows, :]
            acc = _tap_sum(read, cw_ref, CONV_WIDTH, rbc)
            c_out = _silu(_rms(acc + cb_ref[...], cnorm_ref[...]))
            mix_s[g * T + r0:g * T + r0 + rbc, DV_TOT:] = c_out
            bits = lax.bitcast_convert_type(c_out[0:1, 0:2 * DK], jnp.uint32)
            anchors[g * T + r0] = ((bits >> 16) >> 16).astype(F32)
    sc_ref[...] = extc_ref[:, T + CONV_PAD - (CONV_WIDTH - 1):T + CONV_PAD, :]

    if NT > 1:
        extq_ref[:, 0:QKV_PAD, :] = extq_ref[:, T:T + QKV_PAD, :]
        extc_ref[:, 0:CONV_PAD, :] = extc_ref[:, T:T + CONV_PAD, :]

    beta_all, g_all, gc_all = _gates(ba_ref[...], alog_ref, dtb_ref, C)
    _delta_chain(q_s, k_s, v_s, z_ref, sd_ref, mix_s, onorm_ref, beta_all, gc_all, _transpose_rows(gc_all), anchors,
                 R=R, C=C)

    h2_ref[...] = h1_ref[...] + _wdot(mix_s[...], wout_ref[...])


def _sample_mixer_kernel(h1_ref, qkv_ref, z_ref, cin_ref, ba_ref, sd0_ref, sq0_ref, sc0_ref,
                         qw_ref, alog_ref, dtb_ref, onorm_ref, cw_ref, cb_ref, cnorm_ref, wout_ref,
                         h2_ref, sd_ref, sq_ref, sc_ref, q_s, k_s, v_s, mix_s, *, G, T):
    R = G * T
    n_q, n_c = SHORT_CONV - 1, CONV_WIDTH - 1
    sd_ref[...] = sd0_ref[...]
    hsl = lambda h: slice(h * DK, (h + 1) * DK)
    qk_scale = DK ** -0.5
    for gi in range(G // SUBLANES):
        gs = slice(gi * SUBLANES, (gi + 1) * SUBLANES)
        qkv_row = lambda r: sq0_ref[r, gs, :] if r < n_q else qkv_ref[gs, r - n_q, :]
        cin_row = lambda r: sc0_ref[r, gs, :] if r < n_c else cin_ref[gs, r - n_c, :]

        for t in range(T):
            acc = None
            for j in range(SHORT_CONV):
                tap = qkv_row(t + j) * qw_ref[j]
                acc = tap if acc is None else acc + tap
            s = _silu(acc)
            for h in range(N_HEADS):
                qh = s[:, hsl(h)]
                kh = s[:, DK_TOT + h * DK:DK_TOT + (h + 1) * DK]
                q_s[gs, t, hsl(h)] = qh * (lax.rsqrt(jnp.sum(qh * qh, -1, keepdims=True) + EPS) * qk_scale)
                k_s[gs, t, hsl(h)] = kh * lax.rsqrt(jnp.sum(kh * kh, -1, keepdims=True) + EPS)
            v_s[gs, t, :] = s[:, 2 * DK_TOT:]

        accs = [None] * T
        for r in range(n_c + T):
            row = cin_row(r)
            for t in range(max(0, r - n_c), min(T, r + 1)):
                tap = row * cw_ref[r - t]
                accs[t] = tap if accs[t] is None else accs[t] + tap
        for t in range(T):
            mix_s[gs, t, DV_TOT:] = _silu(_rms(accs[t] + cb_ref[...], cnorm_ref[...]))

    for r in range(n_q):
        sq_ref[r] = sq0_ref[r + T] if r + T < n_q else qkv_ref[:, r + T - n_q, :]
    for r in range(n_c):
        sc_ref[r] = sc0_ref[r + T] if r + T < n_c else cin_ref[:, r + T - n_c, :]

    beta_all, g_all, gc_all = _gates(ba_ref[...].reshape(R, LANES), alog_ref, dtb_ref, T)
    gtail_all = _block_cumsum(g_all, T, reverse=True) - g_all
    _delta_blocks(q_s, k_s, v_s, z_ref, sd_ref, mix_s, onorm_ref, beta_all, gc_all, _transpose_rows(gc_all), gtail_all,
                  R=R, C=T)

    mix = _wdot(mix_s[...].reshape(R, DV_TOT + C_CONV), wout_ref[...])
    h2_ref[...] = h1_ref[...].reshape(R, D_MODEL) + mix


def _sample_mixer(rows, sd0, sq0, sc0, prm, *, first_row, T, G):
    B = sd0.shape[1]
    R = G * T
    rows = [a.reshape(a.shape[0] // T, T, a.shape[1]) for a in rows]
    seq = lambda a: pl.BlockSpec((G, T, a.shape[2]), lambda b: (first_row // R + b, 0, 0))
    tm = lambda a: pl.BlockSpec((a.shape[0], G, a.shape[2]), lambda b: (0, b, 0))
    sd_spec = pl.BlockSpec((None, G) + sd0.shape[2:], lambda b: (0, b, 0, 0, 0))
    cst = lambda a: pl.BlockSpec(a.shape, lambda b: (0,) * a.ndim)
    return pl.pallas_call(
        functools.partial(_sample_mixer_kernel, G=G, T=T),
        grid=(B // G,),
        in_specs=[seq(a) for a in rows] + [sd_spec, tm(sq0), tm(sc0)] + [cst(a) for a in prm],
        out_specs=[pl.BlockSpec((R, D_MODEL), lambda b: (b, 0)), sd_spec, tm(sq0), tm(sc0)],
        out_shape=[jax.ShapeDtypeStruct((B * T, D_MODEL), F32), jax.ShapeDtypeStruct(sd0.shape, F32),
                   jax.ShapeDtypeStruct(sq0.shape, F32), jax.ShapeDtypeStruct(sc0.shape, F32)],
        scratch_shapes=[pltpu.VMEM((G, T, DK_TOT), F32), pltpu.VMEM((G, T, DK_TOT), F32),
                        pltpu.VMEM((G, T, DV_TOT), F32), pltpu.VMEM((G, T, DV_TOT + C_CONV), F32)],
        compiler_params=pltpu.CompilerParams(dimension_semantics=("arbitrary",), vmem_limit_bytes=VMEM_LIMIT),
        name="sample_mixer",
    )(*rows, sd0, sq0, sc0, *prm)


def _mixer(rows, sd0, sq0, sc0, prm, *, first_row, n_seq, seq_len, T, C):
    B, L = n_seq, seq_len
    G, NB, NT = 1, B, L // T
    R = G * T
    seq = lambda a: pl.BlockSpec((T, a.shape[1]), lambda b, t: (first_row // T + b * NT + t, 0))
    st = lambda shape: pl.BlockSpec((None, G) + shape, lambda b, t: (0,) * (2 + len(shape)))
    out_st = lambda shape: pl.BlockSpec((None, G) + shape, lambda b, t: (0, b) + (0,) * len(shape))
    cst = lambda a: pl.BlockSpec(a.shape, lambda b, t: (0,) * a.ndim)
    sd_shape, sq_shape, sc_shape = (N_HEADS, DK, DV), (SHORT_CONV - 1, QKV_DIM), (CONV_WIDTH - 1, C_CONV)
    kern = functools.partial(_mixer_kernel, G=G, T=T, C=C, NT=NT)
    return pl.pallas_call(
        kern,
        grid=(NB, NT),
        in_specs=[seq(a) for a in rows] + [st(sd_shape), st(sq_shape), st(sc_shape)] + [cst(a) for a in prm],
        out_specs=[pl.BlockSpec((T, D_MODEL), lambda b, t: (b * NT + t, 0)),
                   out_st(sd_shape), out_st(sq_shape), out_st(sc_shape)],
        out_shape=[jax.ShapeDtypeStruct((B * L, D_MODEL), F32), jax.ShapeDtypeStruct((1, B) + sd_shape, F32),
                   jax.ShapeDtypeStruct((1, B) + sq_shape, F32), jax.ShapeDtypeStruct((1, B) + sc_shape, F32)],
        scratch_shapes=[pltpu.VMEM((G, QKV_PAD + T, QKV_DIM), F32), pltpu.VMEM((G, CONV_PAD + T, C_CONV), F32),
                        pltpu.VMEM((G, SUBLANES - 1, T + CONV_PAD - SUBLANES, C_CONV), F32),
                        pltpu.VMEM((R, DK_TOT), F32), pltpu.VMEM((R, DK_TOT), F32), pltpu.VMEM((R, DV_TOT), F32),
                        pltpu.VMEM((R, DV_TOT + C_CONV), F32)],
        compiler_params=pltpu.CompilerParams(dimension_semantics=("arbitrary", "arbitrary"),
                                             vmem_limit_bytes=VMEM_LIMIT),
        name="mixer",
    )(*rows, sd0, sq0, sc0, *prm)


def kernel(x_prompt, x_sample, state_delta, state_qkv_conv, state_conv, meta_tokens, ffn1_norm, ffn1_w1, ffn1_w3,
           ffn1_w2, mix_norm, w_in, qkv_conv_w, a_log, dt_bias, o_norm, conv_w, conv_b, conv_norm, w_out, ffn2_norm,
           ffn2_w1, ffn2_w3, ffn2_w2, final_norm):
    bp, seq_len, _ = x_prompt.shape
    bs, dec_len, _ = x_sample.shape
    assert state_delta.shape[0] == 1
    row = lambda v: v.reshape(1, -1).astype(F32)
    mat32 = lambda w: w.reshape(w.shape[1:]).astype(F32)

    w_qkvz, w_glu, w_ba = _split_w_in(jnp.transpose(w_in.reshape(w_in.shape[1:])))
    ffn1_w = (mat32(ffn1_w1), mat32(ffn1_w3), mat32(ffn1_w2))
    ffn2_w = (mat32(ffn2_w1), mat32(ffn2_w3), mat32(ffn2_w2))
    f1 = (row(ffn1_norm), row(mix_norm), w_qkvz, w_glu, w_ba)
    f2 = (row(ffn2_norm), row(final_norm))
    lane_row = lambda v: jnp.zeros((1, LANES), F32).at[0, N_HEADS:2 * N_HEADS].set(v.reshape(-1).astype(F32))
    taps8 = lambda w: jnp.broadcast_to(w.astype(F32).reshape(w.shape[1], 1, w.shape[2]),
                                       (w.shape[1], SUBLANES, w.shape[2]))
    prm = (taps8(qkv_conv_w), lane_row(a_log), lane_row(dt_bias), row(o_norm),
           taps8(conv_w), row(conv_b), row(conv_norm), w_out.reshape(w_out.shape[1:]).astype(BF16))

    n_prompt, n_sample = bp * seq_len, bs * dec_len
    pad = (-(n_sample + N_META)) % FRONT_ROWS
    tail = jnp.concatenate([x_sample.reshape(n_sample, D_MODEL), meta_tokens.astype(F32),
                            jnp.zeros((pad, D_MODEL), F32)], axis=0)
    rows = _front(x_prompt.reshape(n_prompt, D_MODEL), tail, ffn1_w, f1, FRONT_ROWS)

    zeros = lambda *s: jnp.zeros(s, F32)
    _, sd_m, sq_m, sc_m = _mixer(rows, zeros(1, 1, N_HEADS, DK, DV), zeros(1, 1, SHORT_CONV - 1, QKV_DIM),
                                 zeros(1, 1, CONV_WIDTH - 1, C_CONV), prm, first_row=n_prompt + n_sample,
                                 n_seq=1, seq_len=N_META, T=N_META, C=N_META)

    h2p, sd_p, sq_p, sc_p = _mixer(rows, sd_m, sq_m, sc_m, prm, first_row=0, n_seq=bp, seq_len=seq_len,
                                   T=MIX_ROWS, C=CHUNK)

    time_major = lambda a: jnp.transpose(a.reshape(a.shape[1:]), (1, 0, 2))
    seq_major = lambda a: jnp.transpose(a, (1, 0, 2))[None]
    h2s, sd_s, sq_s, sc_s = _sample_mixer(rows, state_delta, time_major(state_qkv_conv), time_major(state_conv), prm,
                                          first_row=n_prompt, T=dec_len, G=SAMPLE_GROUP)

    y_prompt, y_sample = _back(h2p, h2s, ffn2_w, f2, BACK_ROWS)
    return (y_prompt.reshape(bp, seq_len, D_MODEL), y_sample.reshape(bs, dec_len, D_MODEL),
            sd_p, sq_p, sc_p, sd_s, seq_major(sq_s), seq_major(sc_s))
```

```python
import functools

import jax
import jax.numpy as jnp
from jax import lax
from jax.experimental import pallas as pl
from jax.experimental.pallas import tpu as pltpu

D_MODEL = 1024
N_HEADS = 4
DK = 128
DV = 128
DK_TOT = N_HEADS * DK
DV_TOT = N_HEADS * DV
QKV_DIM = 2 * DK_TOT + DV_TOT
C_CONV = 512
SHORT_CONV = 4
CONV_WIDTH = 31
CHUNK = 64
N_META = 16
EPS = 1e-6

LANES = 128
SUBLANES = 8
QKV_PAD = SUBLANES
CONV_PAD = 4 * SUBLANES
FRONT_ROWS = 256
BACK_ROWS = 512
MIX_ROWS = 512
SAMPLE_GROUP = 16
N_CAST = 8
VMEM_LIMIT = 56 * 1024 * 1024

F32 = jnp.float32
BF16 = jnp.bfloat16


def _rms(x, g):
    return x * lax.rsqrt(jnp.mean(x * x, axis=-1, keepdims=True) + EPS) * g


def _silu(x):
    return x * jax.nn.sigmoid(x)


def _wdot(x, w):
    return jnp.dot(x.astype(BF16), w, preferred_element_type=F32)


def _sdot(a, b):
    return jnp.dot(a, b, preferred_element_type=F32)


def _stack(*xs):
    return jnp.concatenate(xs, axis=0)


def _ffn_half(h, g_ref, w1_ref, w3_ref, w2_ref):
    u = _rms(h, g_ref[...]).astype(BF16)
    a = jnp.dot(u, w1_ref[...], preferred_element_type=F32)
    b = jnp.dot(u, w3_ref[...], preferred_element_type=F32)
    hid = (_silu(a) * b).astype(BF16)
    return h + 0.5 * jnp.dot(hid, w2_ref[...], preferred_element_type=F32)


def _cast_ffn_chunk(s, srcs, dsts):
    for src, dst in zip(srcs, dsts):
        rc = src.shape[0]
        dst[pl.ds(pl.multiple_of(s * rc, rc), rc), :] = src[...].astype(BF16)


def _front_kernel(xa_ref, xb_ref, w1f_ref, w3f_ref, w2f_ref, g1_ref, gm_ref, wqkvz_ref, wglu_ref, wba_ref,
                  h1_ref, qkv_ref, z_ref, cin_ref, ba_ref, w1_ref, w3_ref, w2_ref, *, na):
    s = pl.program_id(0)

    @pl.when(s < N_CAST)
    def _():
        _cast_ffn_chunk(s, (w1f_ref, w3f_ref, w2f_ref), (w1_ref, w3_ref, w2_ref))

    def run(x_ref):
        h1 = _ffn_half(x_ref[...], g1_ref, w1_ref, w3_ref, w2_ref)
        h1_ref[...] = h1
        u = _rms(h1, gm_ref[...]).astype(BF16)
        p = jnp.dot(u, wqkvz_ref[...], preferred_element_type=F32)
        qkv_ref[...] = p[:, :QKV_DIM]
        z_ref[...] = p[:, QKV_DIM:]
        glu = jnp.dot(u, wglu_ref[...], preferred_element_type=F32)
        cin_ref[...] = glu[:, :C_CONV] * jax.nn.sigmoid(glu[:, C_CONV:])
        ba_ref[...] = jnp.dot(u, wba_ref[...], preferred_element_type=F32)

    pl.when((s >= N_CAST) & (s - N_CAST < na))(functools.partial(run, xa_ref))
    pl.when(s - N_CAST >= na)(functools.partial(run, xb_ref))


def _back_kernel(ha_ref, hb_ref, w1f_ref, w3f_ref, w2f_ref, g2_ref, gf_ref, ya_ref, yb_ref,
                 w1_ref, w3_ref, w2_ref, *, na):
    s = pl.program_id(0)

    @pl.when(s < N_CAST)
    def _():
        _cast_ffn_chunk(s, (w1f_ref, w3f_ref, w2f_ref), (w1_ref, w3_ref, w2_ref))

    def run(h_ref, y_ref):
        half = h_ref.shape[0] // 2
        for r0 in (0, half):
            rows = slice(r0, r0 + half)
            y_ref[rows, :] = _rms(_ffn_half(h_ref[rows, :], g2_ref, w1_ref, w3_ref, w2_ref), gf_ref[...])

    pl.when((s >= N_CAST) & (s - N_CAST < na))(functools.partial(run, ha_ref, ya_ref))
    pl.when(s - N_CAST >= na)(functools.partial(run, hb_ref, yb_ref))


def _const_spec(shape):
    nd = len(shape)
    return pl.BlockSpec(shape, lambda *_: (0,) * nd, pipeline_mode=pl.Buffered(1))


def _two_source_specs(rows, na):
    first = pl.BlockSpec((rows, D_MODEL), lambda s: (jnp.clip(s - N_CAST, 0, na - 1), 0))
    second = pl.BlockSpec((rows, D_MODEL), lambda s: (jnp.maximum(s - N_CAST - na, 0), 0))
    return first, second


def _ffn_weight_specs(ws):
    chunk = lambda w: pl.BlockSpec((w.shape[0] // N_CAST, w.shape[1]), lambda s: (jnp.minimum(s, N_CAST - 1), 0))
    return [chunk(w) for w in ws], [pltpu.VMEM(w.shape, BF16) for w in ws]


def _front(xa, xb, ffn_w, prm, rows):
    na, nb = xa.shape[0] // rows, xb.shape[0] // rows
    n = xa.shape[0] + xb.shape[0]
    widths = (D_MODEL, QKV_DIM, DV_TOT, C_CONV, LANES)
    w_specs, w_scratch = _ffn_weight_specs(ffn_w)
    return pl.pallas_call(
        functools.partial(_front_kernel, na=na),
        grid=(N_CAST + na + nb,),
        in_specs=list(_two_source_specs(rows, na)) + w_specs + [_const_spec(a.shape) for a in prm],
        out_specs=[pl.BlockSpec((rows, w), lambda s: (jnp.maximum(s - N_CAST, 0), 0)) for w in widths],
        out_shape=[jax.ShapeDtypeStruct((n, w), F32) for w in widths],
        scratch_shapes=w_scratch,
        compiler_params=pltpu.CompilerParams(dimension_semantics=("arbitrary",), vmem_limit_bytes=VMEM_LIMIT),
        name="front",
    )(xa, xb, *ffn_w, *prm)


def _back(ha, hb, ffn_w, prm, rows):
    na, nb = ha.shape[0] // rows, hb.shape[0] // rows
    w_specs, w_scratch = _ffn_weight_specs(ffn_w)
    return pl.pallas_call(
        functools.partial(_back_kernel, na=na),
        grid=(N_CAST + na + nb,),
        in_specs=list(_two_source_specs(rows, na)) + w_specs + [_const_spec(a.shape) for a in prm],
        out_specs=list(_two_source_specs(rows, na)),
        out_shape=[jax.ShapeDtypeStruct(ha.shape, F32), jax.ShapeDtypeStruct(hb.shape, F32)],
        scratch_shapes=w_scratch,
        compiler_params=pltpu.CompilerParams(dimension_semantics=("arbitrary",), vmem_limit_bytes=VMEM_LIMIT),
        name="back",
    )(ha, hb, *ffn_w, *prm)


def _split_w_in_kernel(wt_ref, qkvz_ref, glu_ref, ba_ref):
    n_gate = 2 * N_HEADS
    glu0 = QKV_DIM + DV_TOT + n_gate
    step = 2 * LANES
    for c in range(0, QKV_DIM + DV_TOT, step):
        qkvz_ref[:, c:c + step] = wt_ref[c:c + step, :].T.astype(BF16)
    for c in range(0, 2 * C_CONV, step):
        glu_ref[:, c:c + step] = wt_ref[glu0 + c:glu0 + c + step, :].T.astype(BF16)
    ba = jnp.concatenate([wt_ref[QKV_DIM + DV_TOT:glu0, :], jnp.zeros((LANES - n_gate, D_MODEL), F32)], axis=0)
    ba_ref[...] = ba.T.astype(BF16)


def _split_w_in(wt):
    n, d = wt.shape
    widths = (QKV_DIM + DV_TOT, 2 * C_CONV, LANES)
    return pl.pallas_call(
        _split_w_in_kernel,
        grid=(1,),
        in_specs=[_const_spec(wt.shape)],
        out_specs=[pl.BlockSpec((d, wd), lambda i: (0, 0)) for wd in widths],
        out_shape=[jax.ShapeDtypeStruct((d, wd), BF16) for wd in widths],
        compiler_params=pltpu.CompilerParams(vmem_limit_bytes=VMEM_LIMIT),
        name="split_w_in",
    )(wt)


def _block_cumsum(x, block, reverse=False):
    rows = x.shape[0]
    r = lax.broadcasted_iota(jnp.int32, x.shape, 0) % block
    s = 1
    while s < block:
        if reverse:
            x = x + jnp.where(r < block - s, pltpu.roll(x, rows - s, axis=0), 0.0)
        else:
            x = x + jnp.where(r >= s, pltpu.roll(x, s, axis=0), 0.0)
        s *= 2
    return x


def _tap_sum(read, w_ref, n_taps, rows):
    accs = [None] * (rows // SUBLANES)
    for j in range(n_taps):
        w8 = w_ref[j]
        for i in range(len(accs)):
            tap = read(j, i) * w8
            accs[i] = tap if accs[i] is None else accs[i] + tap
    return accs[0] if len(accs) == 1 else jnp.concatenate(accs, axis=0)


def _transpose_rows(x):
    rows = x.shape[0]
    pad = (-rows) % LANES
    if pad:
        x = jnp.concatenate([x, jnp.zeros((pad, x.shape[1]), x.dtype)], axis=0)
    return x.T[:, :rows]


def _gates(ba, alog_ref, dtb_ref, block):
    beta_all = jax.nn.sigmoid(ba)
    xg = ba + dtb_ref[...]
    g_all = -jnp.exp(alog_ref[...]) * (jnp.maximum(xg, 0.0) + jnp.log1p(jnp.exp(-jnp.abs(xg))))
    return beta_all, g_all, _block_cumsum(g_all, block)


def _lane_heads(cols, offset, n_lanes):
    rows = cols.shape[0]
    lane_head = lax.broadcasted_iota(jnp.int32, (rows, n_lanes), 1) // (n_lanes // N_HEADS)
    out = jnp.broadcast_to(cols[:, offset + N_HEADS - 1:offset + N_HEADS], (rows, n_lanes))
    for h in range(N_HEADS - 2, -1, -1):
        out = jnp.where(lane_head == h, cols[:, offset + h:offset + h + 1], out)
    return out


def _head_blocks(x):
    rows, n = x.shape
    lane_head = lax.broadcasted_iota(jnp.int32, (rows, n), 1) // (n // N_HEADS)
    return jnp.concatenate([jnp.where(lane_head == h, x, 0.0) for h in range(N_HEADS)], axis=0)


def _delta_chain(q_s, k_s, v_s, z_ref, sd_ref, mix_s, onorm_ref, beta_all, gc_all, gc_all_t, anchors, *, R, C):
    HL = N_HEADS * C
    _sdot = lambda a, b: jnp.dot(a.astype(BF16), b.astype(BF16), preferred_element_type=F32)
    row_i = lax.broadcasted_iota(jnp.int32, (C, HL), 0)
    col_i = lax.broadcasted_iota(jnp.int32, (C, HL), 1) % C
    eye_p, causal_p, strict_p = row_i == col_i, row_i >= col_i, row_i > col_i
    hsl = lambda h: slice(h * DK, (h + 1) * DK)
    nt_dims = (((1,), (1,)), ((), ()))
    chunks = list(range(0, R, C))
    k_t_all = [_transpose_rows(k_s[:, hsl(h)]) for h in range(N_HEADS)]

    kbs, egcs, decays, a_mat, atts = {}, {}, {}, {}, {}
    for c0 in chunks:
        rs = slice(c0, c0 + C)
        gcol = _lane_heads(gc_all[rs], N_HEADS, HL)
        grow = jnp.sum(jnp.where(eye_p, gcol, 0.0), axis=0, keepdims=True)
        decays[c0] = jnp.exp(jnp.where(causal_p, gcol - grow, -jnp.inf))
        beta = beta_all[rs]
        egcs[c0] = jnp.exp(gc_all[rs])
        k = k_s[rs, :]
        kbs[c0] = jnp.concatenate([k[:, hsl(h)] * beta[:, h:h + 1] for h in range(N_HEADS)], axis=1)
        kq = lax.dot_general(_stack(kbs[c0], q_s[rs, :]), _head_blocks(k), nt_dims,
                             preferred_element_type=F32)
        a_mat[c0] = jnp.where(strict_p, kq[:C] * decays[c0], 0.0)
        atts[c0] = kq[C:] * decays[c0]

    eye = eye_p.astype(F32)
    t_inv = {c0: eye - a_mat[c0] for c0 in chunks}
    if C > 2:
        xs = {c0: _sdot(a_mat[c0], _head_blocks(a_mat[c0])) for c0 in chunks}
        m = 2
        while 2 * m < C:
            prod = {c0: _sdot(_stack(xs[c0], t_inv[c0]), _head_blocks(xs[c0])) for c0 in chunks}
            t_inv = {c0: t_inv[c0] + prod[c0][C:] for c0 in chunks}
            xs = {c0: prod[c0][:C] for c0 in chunks}
            m *= 2
        t_inv = {c0: t_inv[c0] + _sdot(t_inv[c0], _head_blocks(xs[c0])) for c0 in chunks}

    o_loc, q_til, n_loc, kws, e_last = {}, {}, {}, {}, {}
    for c0 in chunks:
        rs = slice(c0, c0 + C)
        beta, egc = beta_all[rs], egcs[c0]
        per_head = lambda x, cols, off: jnp.concatenate(
            [x[:, hsl(h)] * cols[:, off + h:off + h + 1] for h in range(N_HEADS)], axis=1)
        vb = per_head(v_s[rs, :], beta, 0)
        kbg = per_head(kbs[c0], egc, N_HEADS)
        uw = _sdot(t_inv[c0], jnp.concatenate([_head_blocks(vb), _head_blocks(kbg)], axis=1))
        u, w = uw[:, :DV_TOT], uw[:, DV_TOT:]
        au = _sdot(atts[c0], jnp.concatenate([_head_blocks(u), _head_blocks(w)], axis=1))
        o_loc[c0] = au[:, :DV_TOT]
        q_til[c0] = per_head(q_s[rs, :], egc, N_HEADS) - au[:, DV_TOT:]
        for h in range(N_HEADS):
            g_last = gc_all[c0 + C - 1:c0 + C, N_HEADS + h:N_HEADS + h + 1]
            k_til_t = k_t_all[h][:, c0:c0 + C] * jnp.exp(g_last - gc_all_t[N_HEADS + h:N_HEADS + h + 1, c0:c0 + C])
            kuw = _sdot(k_til_t, jnp.concatenate([u[:, hsl(h)], w[:, hsl(h)]], axis=1))
            n_loc[c0, h], kws[c0, h], e_last[c0, h] = kuw[:, :DV], kuw[:, DV:], jnp.exp(g_last)

    zero = jnp.zeros((DK, DV), F32)
    for c0 in chunks:
        rs = slice(c0, c0 + C)
        for h0 in range(0, N_HEADS, 2):
            h1 = h0 + 1
            s0, s1 = sd_ref[0, h0], sd_ref[0, h1]
            s_pair = jnp.concatenate([jnp.concatenate([s0, zero], axis=1), jnp.concatenate([zero, s1], axis=1)], axis=0)
            q_pair = q_til[c0][:, h0 * DK:(h1 + 1) * DK] + anchors[c0]
            lhs = _stack(jnp.concatenate([kws[c0, h0], kws[c0, h1]], axis=1), q_pair)
            prod = _sdot(lhs, s_pair)
            for i, (h, s_old) in enumerate(((h0, s0), (h1, s1))):
                o = prod[DK:, i * DV:(i + 1) * DV] + o_loc[c0][:, hsl(h)]
                sd_ref[0, h] = s_old * e_last[c0, h] - prod[:DK, i * DV:(i + 1) * DV] + n_loc[c0, h]
                mix_s[rs, hsl(h)] = _rms(o, onorm_ref[...]) * _silu(z_ref[rs, hsl(h)])


def _delta_blocks(q_s, k_s, v_s, z_ref, sd_ref, mix_s, onorm_ref, beta_all, gc_all, gc_all_t, gtail_all, *, R, C):
    rows2d = lambda ref, cols: ref[:, :, cols].reshape(R, cols.stop - cols.start)
    row_i = lax.broadcasted_iota(jnp.int32, (R, R), 0)
    col_i = lax.broadcasted_iota(jnp.int32, (R, R), 1)
    same = (row_i // C) == (col_i // C)
    causal = same & (row_i >= col_i)
    strict = same & (row_i > col_i)
    eye = (row_i == col_i).astype(F32)
    hsl = lambda h: slice(h * DK, (h + 1) * DK)
    heads = range(N_HEADS)

    gtail_t = _transpose_rows(gtail_all)
    gc_cs, kbs, decays, qs, kq, k_ts = {}, {}, {}, {}, {}, {}
    for h in heads:
        gc_cs[h] = gc_all[:, N_HEADS + h:N_HEADS + h + 1]
        gc_r = gc_all_t[N_HEADS + h:N_HEADS + h + 1, :]
        decays[h] = jnp.exp(jnp.where(causal, gc_cs[h] - gc_r, -jnp.inf))
        k = rows2d(k_s, hsl(h))
        kbs[h] = k * beta_all[:, h:h + 1]
        qs[h] = rows2d(q_s, hsl(h))
        k_ts[h] = _transpose_rows(k)
        kq[h] = _sdot(_stack(kbs[h], qs[h]), k_ts[h])

    a_mat = {h: jnp.where(strict, kq[h][:R] * decays[h], 0.0) for h in heads}
    t_inv = {h: eye - a_mat[h] for h in heads}
    if C > 2:
        xs = {h: _sdot(a_mat[h], a_mat[h]) for h in heads}
        m = 2
        while 2 * m < C:
            prod = {h: _sdot(_stack(xs[h], t_inv[h]), xs[h]) for h in heads}
            t_inv = {h: t_inv[h] + prod[h][R:] for h in heads}
            xs = {h: prod[h][:R] for h in heads}
            m *= 2
        t_inv = {h: t_inv[h] + _sdot(t_inv[h], xs[h]) for h in heads}

    for h in heads:
        egc = jnp.exp(gc_cs[h])
        vb = rows2d(v_s, hsl(h)) * beta_all[:, h:h + 1]
        uw = _sdot(t_inv[h], jnp.concatenate([vb, kbs[h] * egc], axis=1))
        u, w, att, qg = uw[:, :DV], uw[:, DV:], kq[h][R:] * decays[h], qs[h] * egc
        k_til_t = k_ts[h] * jnp.exp(gtail_t[N_HEADS + h:N_HEADS + h + 1, :])
        wss, qss = [], []
        for b in range(R // C):
            bs = slice(b * C, (b + 1) * C)
            wq = _sdot(_stack(w[bs], qg[bs]), sd_ref[b, h])
            wss.append(wq[:C])
            qss.append(wq[C:])
        v_new = u - jnp.concatenate(wss, axis=0)
        o = jnp.concatenate(qss, axis=0) + _sdot(att, v_new)
        for b in range(R // C):
            bs = slice(b * C, (b + 1) * C)
            e_b = jnp.exp(gc_cs[h][(b + 1) * C - 1:(b + 1) * C, :])
            sd_ref[b, h] = sd_ref[b, h] * e_b + _sdot(k_til_t[:, bs], v_new[bs])
        mix_s[:, :, hsl(h)] = (_rms(o, onorm_ref[...]) * _silu(rows2d(z_ref, hsl(h)))).reshape(R // C, C, DV)


def _mixer_kernel(h1_ref, qkv_ref, z_ref, cin_ref, ba_ref, sd0_ref, sq0_ref, sc0_ref,
                  qw_ref, alog_ref, dtb_ref, onorm_ref, cw_ref, cb_ref, cnorm_ref, wout_ref,
                  h2_ref, sd_ref, sq_ref, sc_ref,
                  extq_ref, extc_ref, shc_ref, q_s, k_s, v_s, mix_s,
                  *, G, T, C, NT):
    R = G * T
    t = pl.program_id(1)

    @pl.when(t == 0)
    def _():
        extq_ref[:, QKV_PAD - (SHORT_CONV - 1):QKV_PAD, :] = sq0_ref[...]
        extc_ref[:, CONV_PAD - (CONV_WIDTH - 1):CONV_PAD, :] = sc0_ref[...]
        sd_ref[...] = sd0_ref[...]

    extq_ref[0, QKV_PAD:QKV_PAD + T, :] = qkv_ref[...]
    extc_ref[0, CONV_PAD:CONV_PAD + T, :] = cin_ref[...]

    qk_scale = DK ** -0.5
    rbq = min(2 * SUBLANES, T)
    for g in range(G):
        for r0 in range(0, T, rbq):
            base = QKV_PAD - (SHORT_CONV - 1) + r0
            read = lambda j, i, g=g, base=base: extq_ref[g, pl.ds(base + j + SUBLANES * i, SUBLANES), :]
            s = _silu(_tap_sum(read, qw_ref, SHORT_CONV, rbq))
            rows = slice(g * T + r0, g * T + r0 + rbq)
            for h in range(N_HEADS):
                qh = s[:, h * DK:(h + 1) * DK]
                kh = s[:, DK_TOT + h * DK:DK_TOT + (h + 1) * DK]
                q_s[rows, h * DK:(h + 1) * DK] = qh * (lax.rsqrt(jnp.sum(qh * qh, -1, keepdims=True) + EPS) * qk_scale)
                k_s[rows, h * DK:(h + 1) * DK] = kh * lax.rsqrt(jnp.sum(kh * kh, -1, keepdims=True) + EPS)
            v_s[rows, :] = s[:, 2 * DK_TOT:]
    sq_ref[...] = extq_ref[:, T + QKV_PAD - (SHORT_CONV - 1):T + QKV_PAD, :]

    first = CONV_PAD - (CONV_WIDTH - 1)
    sh_rows = T + CONV_PAD - SUBLANES
    for g in range(G):
        for b in range(1, SUBLANES):
            for r0 in range(0, sh_rows, CONV_PAD):
                rb = min(CONV_PAD, sh_rows - r0)
                shc_ref[g, b - 1, r0:r0 + rb, :] = extc_ref[g, r0 + b:r0 + b + rb, :]
    rbc = min(C, T)
    anchors = {}
    for g in range(G):
        for r0 in range(0, T, rbc):
            def read(j, i, g=g, r0=r0):
                a, b = divmod(j + first, SUBLANES)
                rows = pl.ds(r0 + SUBLANES * (a + i), SUBLANES)
                return extc_ref[g, rows, :] if b == 0 else shc_ref[g, b - 1, rows, :]
            acc = _tap_sum(read, cw_ref, CONV_WIDTH, rbc)
            c_out = _silu(_rms(acc + cb_ref[...], cnorm_ref[...]))
            mix_s[g * T + r0:g * T + r0 + rbc, DV_TOT:] = c_out
            bits = lax.bitcast_convert_type(c_out[0:1, 0:2 * DK], jnp.uint32)
            anchors[g * T + r0] = ((bits >> 16) >> 16).astype(F32)
    sc_ref[...] = extc_ref[:, T + CONV_PAD - (CONV_WIDTH - 1):T + CONV_PAD, :]

    if NT > 1:
        extq_ref[:, 0:QKV_PAD, :] = extq_ref[:, T:T + QKV_PAD, :]
        extc_ref[:, 0:CONV_PAD, :] = extc_ref[:, T:T + CONV_PAD, :]

    beta_all, g_all, gc_all = _gates(ba_ref[...], alog_ref, dtb_ref, C)
    _delta_chain(q_s, k_s, v_s, z_ref, sd_ref, mix_s, onorm_ref, beta_all, gc_all, _transpose_rows(gc_all), anchors,
                 R=R, C=C)

    h2_ref[...] = h1_ref[...] + _wdot(mix_s[...], wout_ref[...])


def _sample_mixer_kernel(h1_ref, qkv_ref, z_ref, cin_ref, ba_ref, sd0_ref, sq0_ref, sc0_ref,
                         qw_ref, alog_ref, dtb_ref, onorm_ref, cw_ref, cb_ref, cnorm_ref, wout_ref,
                         h2_ref, sd_ref, sq_ref, sc_ref, q_s, k_s, v_s, mix_s, *, G, T):
    R = G * T
    n_q, n_c = SHORT_CONV - 1, CONV_WIDTH - 1
    sd_ref[...] = sd0_ref[...]
    hsl = lambda h: slice(h * DK, (h + 1) * DK)
    qk_scale = DK ** -0.5
    for gi in range(G // SUBLANES):
        gs = slice(gi * SUBLANES, (gi + 1) * SUBLANES)
        qkv_row = lambda r: sq0_ref[r, gs, :] if r < n_q else qkv_ref[gs, r - n_q, :]
        cin_row = lambda r: sc0_ref[r, gs, :] if r < n_c else cin_ref[gs, r - n_c, :]

        for t in range(T):
            acc = None
            for j in range(SHORT_CONV):
                tap = qkv_row(t + j) * qw_ref[j]
                acc = tap if acc is None else acc + tap
            s = _silu(acc)
            for h in range(N_HEADS):
                qh = s[:, hsl(h)]
                kh = s[:, DK_TOT + h * DK:DK_TOT + (h + 1) * DK]
                q_s[gs, t, hsl(h)] = qh * (lax.rsqrt(jnp.sum(qh * qh, -1, keepdims=True) + EPS) * qk_scale)
                k_s[gs, t, hsl(h)] = kh * lax.rsqrt(jnp.sum(kh * kh, -1, keepdims=True) + EPS)
            v_s[gs, t, :] = s[:, 2 * DK_TOT:]

        accs = [None] * T
        for r in range(n_c + T):
            row = cin_row(r)
            for t in range(max(0, r - n_c), min(T, r + 1)):
                tap = row * cw_ref[r - t]
                accs[t] = tap if accs[t] is None else accs[t] + tap
        for t in range(T):
            mix_s[gs, t, DV_TOT:] = _silu(_rms(accs[t] + cb_ref[...], cnorm_ref[...]))

    for r in range(n_q):
        sq_ref[r] = sq0_ref[r + T] if r + T < n_q else qkv_ref[:, r + T - n_q, :]
    for r in range(n_c):
        sc_ref[r] = sc0_ref[r + T] if r + T < n_c else cin_ref[:, r + T - n_c, :]

    beta_all, g_all, gc_all = _gates(ba_ref[...].reshape(R, LANES), alog_ref, dtb_ref, T)
    gtail_all = _block_cumsum(g_all, T, reverse=True) - g_all
    _delta_blocks(q_s, k_s, v_s, z_ref, sd_ref, mix_s, onorm_ref, beta_all, gc_all, _transpose_rows(gc_all), gtail_all,
                  R=R, C=T)

    mix = _wdot(mix_s[...].reshape(R, DV_TOT + C_CONV), wout_ref[...])
    h2_ref[...] = h1_ref[...].reshape(R, D_MODEL) + mix


def _sample_mixer(rows, sd0, sq0, sc0, prm, *, first_row, T, G):
    B = sd0.shape[1]
    R = G * T
    rows = [a.reshape(a.shape[0] // T, T, a.shape[1]) for a in rows]
    seq = lambda a: pl.BlockSpec((G, T, a.shape[2]), lambda b: (first_row // R + b, 0, 0))
    tm = lambda a: pl.BlockSpec((a.shape[0], G, a.shape[2]), lambda b: (0, b, 0))
    sd_spec = pl.BlockSpec((None, G) + sd0.shape[2:], lambda b: (0, b, 0, 0, 0))
    cst = lambda a: pl.BlockSpec(a.shape, lambda b: (0,) * a.ndim)
    return pl.pallas_call(
        functools.partial(_sample_mixer_kernel, G=G, T=T),
        grid=(B // G,),
        in_specs=[seq(a) for a in rows] + [sd_spec, tm(sq0), tm(sc0)] + [cst(a) for a in prm],
        out_specs=[pl.BlockSpec((R, D_MODEL), lambda b: (b, 0)), sd_spec, tm(sq0), tm(sc0)],
        out_shape=[jax.ShapeDtypeStruct((B * T, D_MODEL), F32), jax.ShapeDtypeStruct(sd0.shape, F32),
                   jax.ShapeDtypeStruct(sq0.shape, F32), jax.ShapeDtypeStruct(sc0.shape, F32)],
        scratch_shapes=[pltpu.VMEM((G, T, DK_TOT), F32), pltpu.VMEM((G, T, DK_TOT), F32),
                        pltpu.VMEM((G, T, DV_TOT), F32), pltpu.VMEM((G, T, DV_TOT + C_CONV), F32)],
        compiler_params=pltpu.CompilerParams(dimension_semantics=("arbitrary",), vmem_limit_bytes=VMEM_LIMIT),
        name="sample_mixer",
    )(*rows, sd0, sq0, sc0, *prm)


def _mixer(rows, sd0, sq0, sc0, prm, *, first_row, n_seq, seq_len, T, C):
    B, L = n_seq, seq_len
    G, NB, NT = 1, B, L // T
    R = G * T
    seq = lambda a: pl.BlockSpec((T, a.shape[1]), lambda b, t: (first_row // T + b * NT + t, 0))
    st = lambda shape: pl.BlockSpec((None, G) + shape, lambda b, t: (0,) * (2 + len(shape)))
    out_st = lambda shape: pl.BlockSpec((None, G) + shape, lambda b, t: (0, b) + (0,) * len(shape))
    cst = lambda a: pl.BlockSpec(a.shape, lambda b, t: (0,) * a.ndim)
    sd_shape, sq_shape, sc_shape = (N_HEADS, DK, DV), (SHORT_CONV - 1, QKV_DIM), (CONV_WIDTH - 1, C_CONV)
    kern = functools.partial(_mixer_kernel, G=G, T=T, C=C, NT=NT)
    return pl.pallas_call(
        kern,
        grid=(NB, NT),
        in_specs=[seq(a) for a in rows] + [st(sd_shape), st(sq_shape), st(sc_shape)] + [cst(a) for a in prm],
        out_specs=[pl.BlockSpec((T, D_MODEL), lambda b, t: (b * NT + t, 0)),
                   out_st(sd_shape), out_st(sq_shape), out_st(sc_shape)],
        out_shape=[jax.ShapeDtypeStruct((B * L, D_MODEL), F32), jax.ShapeDtypeStruct((1, B) + sd_shape, F32),
                   jax.ShapeDtypeStruct((1, B) + sq_shape, F32), jax.ShapeDtypeStruct((1, B) + sc_shape, F32)],
        scratch_shapes=[pltpu.VMEM((G, QKV_PAD + T, QKV_DIM), F32), pltpu.VMEM((G, CONV_PAD + T, C_CONV), F32),
                        pltpu.VMEM((G, SUBLANES - 1, T + CONV_PAD - SUBLANES, C_CONV), F32),
                        pltpu.VMEM((R, DK_TOT), F32), pltpu.VMEM((R, DK_TOT), F32), pltpu.VMEM((R, DV_TOT), F32),
                        pltpu.VMEM((R, DV_TOT + C_CONV), F32)],
        compiler_params=pltpu.CompilerParams(dimension_semantics=("arbitrary", "arbitrary"),
                                             vmem_limit_bytes=VMEM_LIMIT),
        name="mixer",
    )(*rows, sd0, sq0, sc0, *prm)


def kernel(x_prompt, x_sample, state_delta, state_qkv_conv, state_conv, meta_tokens, ffn1_norm, ffn1_w1, ffn1_w3,
           ffn1_w2, mix_norm, w_in, qkv_conv_w, a_log, dt_bias, o_norm, conv_w, conv_b, conv_norm, w_out, ffn2_norm,
           ffn2_w1, ffn2_w3, ffn2_w2, final_norm):
    bp, seq_len, _ = x_prompt.shape
    bs, dec_len, _ = x_sample.shape
    assert state_delta.shape[0] == 1
    row = lambda v: v.reshape(1, -1).astype(F32)
    mat32 = lambda w: w.reshape(w.shape[1:]).astype(F32)

    w_qkvz, w_glu, w_ba = _split_w_in(jnp.transpose(w_in.reshape(w_in.shape[1:])))
    ffn1_w = (mat32(ffn1_w1), mat32(ffn1_w3), mat32(ffn1_w2))
    ffn2_w = (mat32(ffn2_w1), mat32(ffn2_w3), mat32(ffn2_w2))
    f1 = (row(ffn1_norm), row(mix_norm), w_qkvz, w_glu, w_ba)
    f2 = (row(ffn2_norm), row(final_norm))
    lane_row = lambda v: jnp.zeros((1, LANES), F32).at[0, N_HEADS:2 * N_HEADS].set(v.reshape(-1).astype(F32))
    taps8 = lambda w: jnp.broadcast_to(w.astype(F32).reshape(w.shape[1], 1, w.shape[2]),
                                       (w.shape[1], SUBLANES, w.shape[2]))
    prm = (taps8(qkv_conv_w), lane_row(a_log), lane_row(dt_bias), row(o_norm),
           taps8(conv_w), row(conv_b), row(conv_norm), w_out.reshape(w_out.shape[1:]).astype(BF16))

    n_prompt, n_sample = bp * seq_len, bs * dec_len
    pad = (-(n_sample + N_META)) % FRONT_ROWS
    tail = jnp.concatenate([x_sample.reshape(n_sample, D_MODEL), meta_tokens.astype(F32),
                            jnp.zeros((pad, D_MODEL), F32)], axis=0)
    rows = _front(x_prompt.reshape(n_prompt, D_MODEL), tail, ffn1_w, f1, FRONT_ROWS)

    zeros = lambda *s: jnp.zeros(s, F32)
    _, sd_m, sq_m, sc_m = _mixer(rows, zeros(1, 1, N_HEADS, DK, DV), zeros(1, 1, SHORT_CONV - 1, QKV_DIM),
                                 zeros(1, 1, CONV_WIDTH - 1, C_CONV), prm, first_row=n_prompt + n_sample,
                                 n_seq=1, seq_len=N_META, T=N_META, C=N_META)

    h2p, sd_p, sq_p, sc_p = _mixer(rows, sd_m, sq_m, sc_m, prm, first_row=0, n_seq=bp, seq_len=seq_len,
                                   T=MIX_ROWS, C=CHUNK)

    time_major = lambda a: jnp.transpose(a.reshape(a.shape[1:]), (1, 0, 2))
    seq_major = lambda a: jnp.transpose(a, (1, 0, 2))[None]
    h2s, sd_s, sq_s, sc_s = _sample_mixer(rows, state_delta, time_major(state_qkv_conv), time_major(state_conv), prm,
                                          first_row=n_prompt, T=dec_len, G=SAMPLE_GROUP)

    y_prompt, y_sample = _back(h2p, h2s, ffn2_w, f2, BACK_ROWS)
    return (y_prompt.reshape(bp, seq_len, D_MODEL), y_sample.reshape(bs, dec_len, D_MODEL),
            sd_p, sq_p, sc_p, sd_s, seq_major(sq_s), seq_major(sc_s))
```
